```python
import math
import jax
import jax.numpy as jnp
from jax import lax
import numpy as np

D_MODEL = 4096
BATCH = 4
SEQ = 2048
DEPTH = 4
DEC_BATCH = 128
DEC_SEQ = 4
PAST_LEN = 16384
PAGE_SIZE = 128

N_EVEN = (DEPTH + 1) // 2
N_ODD = DEPTH // 2
W_A = D_MODEL // 2
S5_GROUP = 16
G_A = W_A // S5_GROUP
S5_P = 64
W_B = D_MODEL // 2
HD_B = 64
H_B = W_B // HD_B
NG_B = 4
N_B = 128
SSD_CONV = 4
SSD_CHUNK = 128
CONV_DIM_B = W_B + 2 * NG_B * N_B
IN_EVEN = W_A + W_B + CONV_DIM_B + H_B
W_C = D_MODEL
CONF_K = 31
N_EGROUPS = 4
EXP_PER_GROUP = 8
N_EXPERTS = N_EGROUPS * EXP_PER_GROUP
TOP_K = 2
D_EXPERT = 512
MOE_BLOCK = 128
DEEPNORM_ALPHA = (2.0 * DEPTH) ** 0.25
DEEPNORM_BETA = (8.0 * DEPTH) ** -0.25
LN_EPS = 1e-5

kernel_name = 'hybrid_s5_ssd_conformer_hmoe_step'


def layer_norm(x, g, b):
    xf = x.astype(jnp.float32)
    mu = jnp.mean(xf, axis=-1, keepdims=True)
    var = jnp.mean(jnp.square(xf - mu), axis=-1, keepdims=True)
    return ((xf - mu) * lax.rsqrt(var + LN_EPS)).astype(x.dtype) * g + b


def causal_dwconv(x, hist, w, b):
    xp = jnp.concatenate([hist.astype(x.dtype), x], axis=1)
    k, c = w.shape
    y = lax.conv_general_dilated(xp, w.astype(x.dtype)[:, None, :], window_strides=(1,), padding='VALID',
                                 dimension_numbers=('NWC', 'WIO', 'NWC'), feature_group_count=c)
    return y + b.astype(x.dtype), xp[:, xp.shape[1] - (k - 1):]


def cmul(ar, ai, br, bi):
    return ar * br - ai * bi, ar * bi + ai * br


def s5_mixer(u, s_re, s_im, lam_re, lam_im, log_dt, b_re, b_im, c_re, c_im, d_skip, w_glu, b_glu):
    f32 = jnp.float32
    bsz, t, _ = u.shape
    uf = u.astype(f32).reshape(bsz, t, G_A, S5_GROUP)
    lr, li = lam_re.astype(f32), lam_im.astype(f32)
    dt = jnp.exp(log_dt.astype(f32))[:, None]
    mag = jnp.exp(lr * dt)
    ab_re, ab_im = mag * jnp.cos(li * dt), mag * jnp.sin(li * dt)
    den = lr * lr + li * li
    q_re = ((ab_re - 1.0) * lr + ab_im * li) / den
    q_im = (ab_im * lr - (ab_re - 1.0) * li) / den
    bb_re, bb_im = cmul(q_re[..., None], q_im[..., None], b_re.astype(f32), b_im.astype(f32))
    v_re = jnp.einsum('btgc,gpc->btgp', uf, bb_re)
    v_im = jnp.einsum('btgc,gpc->btgp', uf, bb_im)
    init_re, init_im = cmul(ab_re, ab_im, s_re.astype(f32), s_im.astype(f32))
    v_re = v_re.at[:, 0].add(init_re)
    v_im = v_im.at[:, 0].add(init_im)
    a_re = jnp.broadcast_to(ab_re, (1, t, G_A, S5_P))
    a_im = jnp.broadcast_to(ab_im, (1, t, G_A, S5_P))

    def combine(e1, e2):
        a1r, a1i, b1r, b1i = e1
        a2r, a2i, b2r, b2i = e2
        ar, ai = cmul(a2r, a2i, a1r, a1i)
        br, bi = cmul(a2r, a2i, b1r, b1i)
        return ar, ai, br + b2r, bi + b2i

    _, _, st_re, st_im = lax.associative_scan(combine, (a_re, a_im, v_re, v_im), axis=1)
    y = (jnp.einsum('btgp,gcp->btgc', st_re, c_re.astype(f32))
         - jnp.einsum('btgp,gcp->btgc', st_im, c_im.astype(f32)))
    y = y.reshape(bsz, t, W_A) + d_skip.astype(f32) * u.astype(f32)
    h = jax.nn.gelu(y).astype(u.dtype)
    out = h * jax.nn.sigmoid(h @ w_glu + b_glu)
    return out, st_re[:, -1], st_im[:, -1]


def ssd_scan(x, dt, a, bm, cm, h0):
    f32 = jnp.float32
    bsz, t, nh, hp = x.shape
    q = min(SSD_CHUNK, t)
    nc = -(-t // q)
    pad = nc * q - t

    def chunked(v):
        v = jnp.pad(v, [(0, 0), (0, pad)] + [(0, 0)] * (v.ndim - 2))
        return v.reshape((bsz, nc, q) + v.shape[2:])

    rep = H_B // NG_B
    xdt = chunked(x.astype(f32) * dt[..., None])
    adt = chunked(dt * a)
    bh = chunked(jnp.repeat(bm.astype(f32), rep, axis=2))
    ch = chunked(jnp.repeat(cm.astype(f32), rep, axis=2))
    acum = jnp.cumsum(adt, axis=2)
    seg = acum[:, :, :, None, :] - acum[:, :, None, :, :]
    causal = jnp.tril(jnp.ones((q, q), dtype=bool))[None, None, :, :, None]
    lmat = jnp.exp(jnp.where(causal, seg, -jnp.inf))
    scores = jnp.einsum('bclhn,bcshn->bclsh', ch, bh) * lmat
    y_diag = jnp.einsum('bclsh,bcshp->bclhp', scores, xdt)
    decay_to_end = jnp.exp(acum[:, :, -1:, :] - acum)
    chunk_states = jnp.einsum('bclhn,bclh,bclhp->bchpn', bh, decay_to_end, xdt)
    chunk_decay = jnp.exp(acum[:, :, -1, :])

    def step(hc, inp):
        st, dec = inp
        return dec[:, :, None, None] * hc + st, hc

    h_final, h_start = lax.scan(step, h0.astype(f32),
                                (jnp.swapaxes(chunk_states, 0, 1), jnp.swapaxes(chunk_decay, 0, 1)))
    h_start = jnp.swapaxes(h_start, 0, 1)
    y_off = jnp.einsum('bclhn,bchpn,bclh->bclhp', ch, h_start, jnp.exp(acum))
    y = (y_diag + y_off).reshape(bsz, nc * q, nh, hp)[:, :t]
    return y, h_final


def ssd_mixer(z, xbc, dt_raw, h0, conv_hist, conv_w, conv_b, dt_bias, a_log, d_skip, norm_w):
    f32 = jnp.float32
    bsz, t, _ = z.shape
    xbc_c, new_hist = causal_dwconv(xbc, conv_hist, conv_w, conv_b)
    xbc_c = jax.nn.silu(xbc_c)
    xs = xbc_c[..., :W_B].reshape(bsz, t, H_B, HD_B)
    bm = xbc_c[..., W_B:W_B + NG_B * N_B].reshape(bsz, t, NG_B, N_B)
    cm = xbc_c[..., W_B + NG_B * N_B:].reshape(bsz, t, NG_B, N_B)
    dt = jax.nn.softplus(dt_raw.astype(f32) + dt_bias.astype(f32))
    a = -jnp.exp(a_log.astype(f32))
    y, h_final = ssd_scan(xs, dt, a, bm, cm, h0)
    y = y + d_skip.astype(f32)[:, None] * xs.astype(f32)
    y = y.reshape(bsz, t, W_B) * jax.nn.silu(z.astype(f32))
    yg = y.reshape(bsz, t, NG_B, W_B // NG_B)
    yg = yg * lax.rsqrt(jnp.mean(jnp.square(yg), axis=-1, keepdims=True) + LN_EPS)
    return yg.reshape(bsz, t, W_B).astype(z.dtype) * norm_w, h_final, new_hist


def even_mixer(x, s_re, s_im, ssm, ssd_hist, p, i):
    u = x @ p['w_in_even'][i]
    o1 = W_A
    o2 = o1 + W_B
    o3 = o2 + CONV_DIM_B
    u_a, z, xbc, dt_raw = u[..., :o1], u[..., o1:o2], u[..., o2:o3], u[..., o3:]
    ya, sr, si = s5_mixer(u_a, s_re, s_im, p['s5_lam_re'][i], p['s5_lam_im'][i], p['s5_log_dt'][i],
                          p['s5_b_re'][i], p['s5_b_im'][i], p['s5_c_re'][i], p['s5_c_im'][i],
                          p['s5_d'][i], p['s5_w_glu'][i], p['s5_b_glu'][i])
    yb, h, hist = ssd_mixer(z, xbc, dt_raw, ssm, ssd_hist, p['ssd_conv_w'][i], p['ssd_conv_b'][i],
                            p['ssd_dt_bias'][i], p['ssd_a_log'][i], p['ssd_d'][i], p['ssd_norm_w'][i])
    mix = jnp.concatenate([ya, yb], axis=-1) @ p['w_out_even'][i]
    return mix, sr, si, h, hist


def conformer_conv(x, hist, p, i):
    h = x @ p['conf_w_pw1'][i] + p['conf_b_pw1'][i]
    g = h[..., :W_C] * jax.nn.sigmoid(h[..., W_C:])
    c, new_hist = causal_dwconv(g, hist, p['conf_dw_w'][i], p['conf_dw_b'][i])
    c = jax.nn.silu(layer_norm(c, p['conf_ln_g'][i], p['conf_ln_b'][i]))
    return c @ p['conf_w_pw2'][i] + p['conf_b_pw2'][i], new_hist


def hier_moe(x, w_group, b_group, w_expert, b_expert, w_gate, w_up, w_down):
    shp = x.shape
    xt = x.reshape(-1, D_MODEL)
    t = xt.shape[0]
    pg = jax.nn.softmax((xt @ w_group + b_group).astype(jnp.float32), axis=-1)
    pg_top, grp = lax.top_k(pg, 1)
    le = (jnp.einsum('td,gde->tge', xt, w_expert) + b_expert).astype(jnp.float32)
    le_sel = jnp.take_along_axis(le, grp[:, :, None], axis=1)[:, 0]
    lt, idx = lax.top_k(le_sel, TOP_K)
    gates = pg_top * jax.nn.softmax(lt, axis=-1)
    eid = (grp * EXP_PER_GROUP + idx).reshape(-1).astype(jnp.int32)
    n_assign = t * TOP_K
    order = jnp.argsort(eid)
    e_sorted = eid[order]
    counts = jnp.bincount(eid, length=N_EXPERTS)
    padded = (counts + MOE_BLOCK - 1) // MOE_BLOCK * MOE_BLOCK
    pad_end = jnp.cumsum(padded)
    pad_start = pad_end - padded
    start = jnp.cumsum(counts) - counts
    slot = pad_start[e_sorted] + (jnp.arange(n_assign, dtype=jnp.int32) - start[e_sorted])
    n_blocks = -(-n_assign // MOE_BLOCK) + N_EXPERTS
    cap = n_blocks * MOE_BLOCK
    tok_buf = jnp.full((cap,), t, jnp.int32).at[slot].set((order // TOP_K).astype(jnp.int32))
    gate_buf = jnp.zeros((cap,), x.dtype).at[slot].set(gates.reshape(-1)[order].astype(x.dtype))
    blk_expert = jnp.minimum(jnp.searchsorted(pad_end, jnp.arange(n_blocks, dtype=jnp.int32) * MOE_BLOCK,
                                              side='right'), N_EXPERTS - 1)
    x_pad = jnp.concatenate([xt, jnp.zeros((1, D_MODEL), xt.dtype)], axis=0)
    xb = x_pad[tok_buf].reshape(n_blocks, MOE_BLOCK, D_MODEL)

    def run_block(args):
        xblk, e = args
        hblk = jax.nn.silu(xblk @ w_gate[e]) * (xblk @ w_up[e])
        return hblk @ w_down[e]

    yb = lax.map(run_block, (xb, blk_expert)).reshape(cap, D_MODEL)
    y = jnp.zeros((t + 1, D_MODEL), x.dtype).at[tok_buf].add(yb * gate_buf[:, None])[:t]
    return y.reshape(shp)


def run_trunk(x, s5_re, s5_im, ssm, ssd_hist, conf_hist, p):
    new_re, new_im, new_ssm, new_sh, new_ch = [], [], [], [], []
    for layer in range(DEPTH):
        i = layer // 2
        if layer % 2 == 0:
            mix, sr, si, h, sh = even_mixer(x, s5_re[i], s5_im[i], ssm[i], ssd_hist[i], p, i)
            new_re.append(sr)
            new_im.append(si)
            new_ssm.append(h)
            new_sh.append(sh)
        else:
            mix, chh = conformer_conv(x, conf_hist[i], p, i)
            new_ch.append(chh)
        x = layer_norm(DEEPNORM_ALPHA * x + mix, p['ln_g'][layer, 0], p['ln_b'][layer, 0])
        ff = hier_moe(x, p['moe_w_group'][layer], p['moe_b_group'][layer], p['moe_w_expert'][layer],
                      p['moe_b_expert'][layer], p['moe_w_gate'][layer], p['moe_w_up'][layer], p['moe_w_down'][layer])
        x = layer_norm(DEEPNORM_ALPHA * x + ff, p['ln_g'][layer, 1], p['ln_b'][layer, 1])
    return x, jnp.stack(new_re), jnp.stack(new_im), jnp.stack(new_ssm), jnp.stack(new_sh), jnp.stack(new_ch)


def setup_inputs(seed: int = 0) -> dict:
    key = jax.random.key(seed)
    keys = jax.random.split(key, 64)
    counter = [0]
    f32 = jnp.float32

    def nxt():
        counter[0] += 1
        return keys[counter[0] - 1]

    def nrm(shape, scale):
        return jax.random.normal(nxt(), shape, f32) * scale

    n_idx = jnp.arange(S5_P, dtype=f32)
    dt0 = jnp.exp(jax.random.uniform(nxt(), (N_EVEN, H_B), f32, math.log(1e-3), math.log(1e-1)))
    return {
        'x_prompt': nrm((BATCH, SEQ, D_MODEL), 1.0),
        'x_sample': nrm((DEC_BATCH, DEC_SEQ, D_MODEL), 1.0),
        'state_s5_re': nrm((N_EVEN, DEC_BATCH, G_A, S5_P), 0.1),
        'state_s5_im': nrm((N_EVEN, DEC_BATCH, G_A, S5_P), 0.1),
        'state_ssm': nrm((N_EVEN, DEC_BATCH, H_B, HD_B, N_B), 0.1),
        'state_ssd_conv': nrm((N_EVEN, DEC_BATCH, SSD_CONV - 1, CONV_DIM_B), 1.0),
        'state_conformer_conv': nrm((N_ODD, DEC_BATCH, CONF_K - 1, W_C), 0.5),
        'w_in_even': nrm((N_EVEN, D_MODEL, IN_EVEN), D_MODEL ** -0.5),
        's5_lam_re': -0.5 + nrm((N_EVEN, G_A, S5_P), 0.01),
        's5_lam_im': math.pi * n_idx + nrm((N_EVEN, G_A, S5_P), 0.01),
        's5_log_dt': jax.random.uniform(nxt(), (N_EVEN, G_A), f32, math.log(1e-3), math.log(1e-1)),
        's5_b_re': nrm((N_EVEN, G_A, S5_P, S5_GROUP), S5_GROUP ** -0.5),
        's5_b_im': nrm((N_EVEN, G_A, S5_P, S5_GROUP), S5_GROUP ** -0.5),
        's5_c_re': nrm((N_EVEN, G_A, S5_GROUP, S5_P), S5_P ** -0.5),
        's5_c_im': nrm((N_EVEN, G_A, S5_GROUP, S5_P), S5_P ** -0.5),
        's5_d': nrm((N_EVEN, W_A), 1.0),
        's5_w_glu': nrm((N_EVEN, W_A, W_A), W_A ** -0.5),
        's5_b_glu': nrm((N_EVEN, W_A), 0.02),
        'ssd_conv_w': nrm((N_EVEN, SSD_CONV, CONV_DIM_B), SSD_CONV ** -0.5),
        'ssd_conv_b': nrm((N_EVEN, CONV_DIM_B), 0.02),
        'ssd_dt_bias': dt0 + jnp.log(-jnp.expm1(-dt0)),
        'ssd_a_log': jnp.log(jax.random.uniform(nxt(), (N_EVEN, H_B), f32, 1.0, 16.0)),
        'ssd_d': 1.0 + nrm((N_EVEN, H_B), 0.1),
        'ssd_norm_w': 1.0 + nrm((N_EVEN, W_B), 0.05),
        'w_out_even': nrm((N_EVEN, W_A + W_B, D_MODEL), (W_A + W_B) ** -0.5 * DEEPNORM_BETA),
        'conf_w_pw1': nrm((N_ODD, D_MODEL, 2 * W_C), D_MODEL ** -0.5),
        'conf_b_pw1': nrm((N_ODD, 2 * W_C), 0.02),
        'conf_dw_w': nrm((N_ODD, CONF_K, W_C), CONF_K ** -0.5),
        'conf_dw_b': nrm((N_ODD, W_C), 0.02),
        'conf_ln_g': 1.0 + nrm((N_ODD, W_C), 0.05),
        'conf_ln_b': nrm((N_ODD, W_C), 0.02),
        'conf_w_pw2': nrm((N_ODD, W_C, D_MODEL), W_C ** -0.5 * DEEPNORM_BETA),
        'conf_b_pw2': nrm((N_ODD, D_MODEL), 0.02),
        'ln_g': 1.0 + nrm((DEPTH, 2, D_MODEL), 0.05),
        'ln_b': nrm((DEPTH, 2, D_MODEL), 0.02),
        'moe_w_group': nrm((DEPTH, D_MODEL, N_EGROUPS), D_MODEL ** -0.5),
        'moe_b_group': nrm((DEPTH, N_EGROUPS), 0.01),
        'moe_w_expert': nrm((DEPTH, N_EGROUPS, D_MODEL, EXP_PER_GROUP), D_MODEL ** -0.5),
        'moe_b_expert': nrm((DEPTH, N_EGROUPS, EXP_PER_GROUP), 0.01),
        'moe_w_gate': nrm((DEPTH, N_EXPERTS, D_MODEL, D_EXPERT), D_MODEL ** -0.5),
        'moe_w_up': nrm((DEPTH, N_EXPERTS, D_MODEL, D_EXPERT), D_MODEL ** -0.5),
        'moe_w_down': nrm((DEPTH, N_EXPERTS, D_EXPERT, D_MODEL), D_EXPERT ** -0.5 * DEEPNORM_BETA),
    }


def reference(x_prompt, x_sample, state_s5_re, state_s5_im, state_ssm, state_ssd_conv, state_conformer_conv,
              w_in_even, s5_lam_re, s5_lam_im, s5_log_dt, s5_b_re, s5_b_im, s5_c_re, s5_c_im, s5_d,
              s5_w_glu, s5_b_glu, ssd_conv_w, ssd_conv_b, ssd_dt_bias, ssd_a_log, ssd_d, ssd_norm_w,
              w_out_even, conf_w_pw1, conf_b_pw1, conf_dw_w, conf_dw_b, conf_ln_g, conf_ln_b, conf_w_pw2,
              conf_b_pw2, ln_g, ln_b, moe_w_group, moe_b_group, moe_w_expert, moe_b_expert, moe_w_gate,
              moe_w_up, moe_w_down):
    p = dict(w_in_even=w_in_even, s5_lam_re=s5_lam_re, s5_lam_im=s5_lam_im, s5_log_dt=s5_log_dt,
             s5_b_re=s5_b_re, s5_b_im=s5_b_im, s5_c_re=s5_c_re, s5_c_im=s5_c_im, s5_d=s5_d,
             s5_w_glu=s5_w_glu, s5_b_glu=s5_b_glu, ssd_conv_w=ssd_conv_w, ssd_conv_b=ssd_conv_b,
             ssd_dt_bias=ssd_dt_bias, ssd_a_log=ssd_a_log, ssd_d=ssd_d, ssd_norm_w=ssd_norm_w,
             w_out_even=w_out_even, conf_w_pw1=conf_w_pw1, conf_b_pw1=conf_b_pw1, conf_dw_w=conf_dw_w,
             conf_dw_b=conf_dw_b, conf_ln_g=conf_ln_g, conf_ln_b=conf_ln_b, conf_w_pw2=conf_w_pw2,
             conf_b_pw2=conf_b_pw2, ln_g=ln_g, ln_b=ln_b, moe_w_group=moe_w_group, moe_b_group=moe_b_group,
             moe_w_expert=moe_w_expert, moe_b_expert=moe_b_expert, moe_w_gate=moe_w_gate, moe_w_up=moe_w_up,
             moe_w_down=moe_w_down)
    z_re = jnp.zeros((N_EVEN, BATCH, G_A, S5_P), state_s5_re.dtype)
    z_ssm = jnp.zeros((N_EVEN, BATCH, H_B, HD_B, N_B), state_ssm.dtype)
    z_sh = jnp.zeros((N_EVEN, BATCH, SSD_CONV - 1, CONV_DIM_B), state_ssd_conv.dtype)
    z_ch = jnp.zeros((N_ODD, BATCH, CONF_K - 1, W_C), state_conformer_conv.dtype)
    y_prompt, s5_re_p, s5_im_p, ssm_p, ssd_conv_p, conf_conv_p = run_trunk(
        x_prompt, z_re, z_re, z_ssm, z_sh, z_ch, p)
    y_sample, s5_re_s, s5_im_s, ssm_s, ssd_conv_s, conf_conv_s = run_trunk(
        x_sample, state_s5_re, state_s5_im, state_ssm, state_ssd_conv, state_conformer_conv, p)
    return (y_prompt, y_sample, s5_re_p, s5_im_p, ssm_p, ssd_conv_p, conf_conv_p,
            s5_re_s, s5_im_s, ssm_s, ssd_conv_s, conf_conv_s)
```

```python
import functools
import math

import jax
import jax.numpy as jnp
from jax import lax
from jax.experimental import pallas as pl
from jax.experimental.pallas import tpu as pltpu

F32 = jnp.float32
BF16 = jnp.bfloat16
HIGHEST = lax.Precision.HIGHEST

LN_EPS = 1e-5
V7X_VMEM_LIMIT_BYTES = 56 * 1024 * 1024
LANES = 128
SUBLANES = 8

S5_GROUP = 16
SSD_HEAD_DIM = 64
SSD_GROUPS = 4
SSD_CHUNK = 128
TOP_K = 2

MM_TM = 512
MM_TN = 512
LN_TM = 256
MOE_BM = 256
MOE_TF = 256
MOE_TN = 1024
S5_CHUNK = 16
S5_GB = 8
CONF_TT = 256
CONF_RT = 32
CONF_HALO = 32
SAMPLE_SEQ_ROWS = 8


def _cparams(sem):
    return pltpu.CompilerParams(dimension_semantics=sem, vmem_limit_bytes=V7X_VMEM_LIMIT_BYTES)


def _dot(a, b):
    return jnp.dot(a, b, preferred_element_type=F32)


def _dot_hi(a, b):
    return jnp.dot(a, b, preferred_element_type=F32, precision=HIGHEST)


def _dot_nt(a, b):
    return lax.dot_general(a, b, (((1,), (1,)), ((), ())), preferred_element_type=F32)


def _dot_tn(a, b):
    return lax.dot_general(a, b, (((0,), (0,)), ((), ())), preferred_element_type=F32)


def _sigmoid(x):
    return 1.0 / (1.0 + jnp.exp(-x))


def _silu(x):
    return x * _sigmoid(x)


def _mm_kernel(*refs, n_lhs, n_col, has_bias, mode):
    pos = 0
    x_refs = refs[pos:pos + n_lhs]; pos += n_lhs
    w_refs = refs[pos:pos + n_lhs * n_col]; pos += n_lhs * n_col
    b_refs = ()
    if has_bias:
        b_refs = refs[pos:pos + n_col]; pos += n_col
    gate_ref = None
    if mode == "gate":
        gate_ref = refs[pos]; pos += 1
    out_ref = refs[pos]; pos += 1
    wbf_refs = refs[pos:]

    @pl.when(pl.program_id(1) == 0)
    def _():
        for w_ref, wbf_ref in zip(w_refs, wbf_refs):
            wbf_ref[...] = w_ref[...].astype(BF16)

    accs = []
    for c in range(n_col):
        acc = None
        for l in range(n_lhs):
            part = _dot(x_refs[l][...].astype(BF16), wbf_refs[l * n_col + c][...])
            acc = part if acc is None else acc + part
        if has_bias:
            acc = acc + b_refs[c][...]
        accs.append(acc)
    if mode == "glu":
        res = accs[0] * _sigmoid(accs[1])
    elif mode == "gate":
        res = gate_ref[...] * _sigmoid(accs[0])
    else:
        res = accs[0]
    out_ref[...] = res.astype(out_ref.dtype)


def _matmul(xs, w, layer, bias=None, mode="plain", gate=None, n_out=None, out_dtype=F32,
            tm=MM_TM, tn=MM_TN):
    n_lhs = len(xs)
    m, k = xs[0].shape
    n_total = w.shape[2]
    n_col = 2 if mode == "glu" else 1
    if n_out is None:
        n_out = n_total // n_col
    assert m % tm == 0 and w.shape[1] == n_lhs * k
    nj = pl.cdiv(n_out, tn)
    glu_off = (n_total // 2) // tn if mode == "glu" else 0
    if mode == "glu":
        assert (n_total // 2) % tn == 0

    in_specs = [pl.BlockSpec((tm, k), lambda j, i: (i, 0)) for _ in range(n_lhs)]
    args = list(xs)
    for l in range(n_lhs):
        for c in range(n_col):
            in_specs.append(pl.BlockSpec((None, k, tn),
                                         functools.partial(lambda j, i, l, c: (layer, l, j + c * glu_off), l=l, c=c)))
            args.append(w)
    if bias is not None:
        b3 = bias.reshape(bias.shape[0], 1, bias.shape[1])
        for c in range(n_col):
            in_specs.append(pl.BlockSpec((None, 1, tn),
                                         functools.partial(lambda j, i, c: (layer, 0, j + c * glu_off), c=c)))
            args.append(b3)
    if mode == "gate":
        in_specs.append(pl.BlockSpec((tm, tn), lambda j, i: (i, j)))
        args.append(gate)
    kern = functools.partial(_mm_kernel, n_lhs=n_lhs, n_col=n_col, has_bias=bias is not None, mode=mode)
    return pl.pallas_call(
        kern,
        grid=(nj, m // tm),
        in_specs=in_specs,
        out_specs=pl.BlockSpec((tm, tn), lambda j, i: (i, j)),
        out_shape=jax.ShapeDtypeStruct((m, n_out), out_dtype),
        scratch_shapes=[pltpu.VMEM((k, tn), BF16) for _ in range(n_lhs * n_col)],
        compiler_params=_cparams(("arbitrary", "arbitrary")),
        name=f"mm_{mode}",
    )(*args)


def _ln_rows(y, g, b):
    mu = jnp.mean(y, axis=-1, keepdims=True)
    yc = y - mu
    var = jnp.mean(yc * yc, axis=-1, keepdims=True)
    return yc * lax.rsqrt(var + LN_EPS) * g + b


def _first_argmax(v, lane, big):
    m = jnp.max(v, axis=-1, keepdims=True)
    idx = jnp.min(jnp.where(v == m, lane, big), axis=-1, keepdims=True)
    return m, idx


def _ln_kernel(*refs, alpha, route, n_groups, per_group):
    if route:
        x_ref, mix_ref, g_ref, b_ref, wr_ref, br_ref, out_ref, outbf_ref, eid_ref, gate_ref = refs
    else:
        x_ref, mix_ref, g_ref, b_ref, out_ref, outbf_ref = refs
    y = alpha * x_ref[...] + mix_ref[...]
    out = _ln_rows(y, g_ref[...], b_ref[...])
    out_ref[...] = out
    outbf_ref[...] = out.astype(BF16)
    if route:
        lg = _dot_hi(out, wr_ref[...]) + br_ref[...]
        lane = lax.broadcasted_iota(jnp.int32, lg.shape, 1)
        neg = jnp.float32(-jnp.inf)
        big = jnp.int32(LANES)
        gl = jnp.where(lane < n_groups, lg, neg)
        gmax, grp = _first_argmax(gl, lane, big)
        pg_top = 1.0 / jnp.sum(jnp.exp(gl - gmax), axis=-1, keepdims=True)
        lo = n_groups + grp * per_group
        el = jnp.where((lane >= lo) & (lane < lo + per_group), lg, neg)
        m1, i1 = _first_argmax(el, lane, big)
        el2 = jnp.where(lane == i1, neg, el)
        m2, i2 = _first_argmax(el2, lane, big)
        e2 = jnp.exp(m2 - m1)
        g1 = pg_top / (1.0 + e2)
        g2 = pg_top * e2 / (1.0 + e2)
        eid_ref[...] = jnp.where(lane == 0, i1 - n_groups, jnp.where(lane == 1, i2 - n_groups, 0))
        gate_ref[...] = jnp.where(lane == 0, g1, jnp.where(lane == 1, g2, 0.0))


def _ln_residual(x, mix, ln_g, ln_b, layer, which, alpha, router=None, tm=LN_TM):
    m, d = x.shape
    assert m % tm == 0
    row = pl.BlockSpec((tm, d), lambda i: (i, 0))
    par = pl.BlockSpec((None, None, 1, d), lambda i: (layer, which, 0, 0))
    g4 = ln_g.reshape(ln_g.shape[0], ln_g.shape[1], 1, d)
    b4 = ln_b.reshape(ln_b.shape[0], ln_b.shape[1], 1, d)
    in_specs = [row, row, par, par]
    args = [x, mix, g4, b4]
    out_specs = [row, row]
    out_shape = [jax.ShapeDtypeStruct((m, d), F32), jax.ShapeDtypeStruct((m, d), BF16)]
    route = router is not None
    n_groups = per_group = 0
    if route:
        wr, br, n_groups, per_group = router
        in_specs += [pl.BlockSpec((d, LANES), lambda i: (0, 0)), pl.BlockSpec((1, LANES), lambda i: (0, 0))]
        args += [wr, br]
        small = pl.BlockSpec((tm, LANES), lambda i: (i, 0))
        out_specs += [small, small]
        out_shape += [jax.ShapeDtypeStruct((m, LANES), jnp.int32), jax.ShapeDtypeStruct((m, LANES), F32)]
    kern = functools.partial(_ln_kernel, alpha=alpha, route=route, n_groups=n_groups, per_group=per_group)
    return pl.pallas_call(
        kern, grid=(m // tm,), in_specs=in_specs, out_specs=out_specs, out_shape=out_shape,
        compiler_params=_cparams(("parallel",)), name="ln_router" if route else "ln",
    )(*args)


def _moe_up_kernel(be_ref, nu_ref, xs_ref, wg_ref, wu_ref, h_ref, wgbf_ref, wubf_ref):
    i = pl.program_id(1)
    prev = be_ref[jnp.maximum(i - 1, 0)]
    fresh = (i == 0) | (be_ref[i] != prev)

    @pl.when(fresh)
    def _():
        wgbf_ref[...] = wg_ref[...].astype(BF16)
        wubf_ref[...] = wu_ref[...].astype(BF16)

    @pl.when(i < nu_ref[0])
    def _():
        x = xs_ref[...]
        h = _silu(_dot(x, wgbf_ref[...])) * _dot(x, wubf_ref[...])
        h_ref[...] = h.astype(BF16)

    @pl.when(i >= nu_ref[0])
    def _():
        h_ref[...] = jnp.zeros(h_ref.shape, BF16)


def _moe_down_kernel(be_ref, nu_ref, h_ref, wd_ref, gate_ref, out_ref, wdbf_ref):
    i = pl.program_id(1)
    prev = be_ref[jnp.maximum(i - 1, 0)]
    fresh = (i == 0) | (be_ref[i] != prev)

    @pl.when(fresh)
    def _():
        wdbf_ref[...] = wd_ref[...].astype(BF16)

    @pl.when(i < nu_ref[0])
    def _():
        out_ref[...] = _dot(h_ref[...], wdbf_ref[...]) * gate_ref[...]

    @pl.when(i >= nu_ref[0])
    def _():
        out_ref[...] = jnp.zeros(out_ref.shape, F32)


def _moe_experts(xs, blk_expert, n_used, gate_buf, w_gate, w_up, w_down, layer,
                 bm=MOE_BM, tf=MOE_TF, tn=MOE_TN):
    cap, d = xs.shape
    f = w_gate.shape[3]
    n_blocks = cap // bm
    tf = min(tf, f)
    tn = min(tn, d)
    hidden = pl.pallas_call(
        _moe_up_kernel,
        grid_spec=pltpu.PrefetchScalarGridSpec(
            num_scalar_prefetch=2,
            grid=(f // tf, n_blocks),
            in_specs=[
                pl.BlockSpec((bm, d), lambda j, i, be, nu: (i, 0)),
                pl.BlockSpec((None, None, d, tf), lambda j, i, be, nu: (layer, be[i], 0, j)),
                pl.BlockSpec((None, None, d, tf), lambda j, i, be, nu: (layer, be[i], 0, j)),
            ],
            out_specs=pl.BlockSpec((bm, tf), lambda j, i, be, nu: (i, j)),
            scratch_shapes=[pltpu.VMEM((d, tf), BF16), pltpu.VMEM((d, tf), BF16)],
        ),
        out_shape=jax.ShapeDtypeStruct((cap, f), BF16),
        compiler_params=_cparams(("arbitrary", "arbitrary")),
        name="moe_up",
    )(blk_expert, n_used, xs, w_gate, w_up)
    return pl.pallas_call(
        _moe_down_kernel,
        grid_spec=pltpu.PrefetchScalarGridSpec(
            num_scalar_prefetch=2,
            grid=(d // tn, n_blocks),
            in_specs=[
                pl.BlockSpec((bm, f), lambda j, i, be, nu: (i, 0)),
                pl.BlockSpec((None, None, f, tn), lambda j, i, be, nu: (layer, be[i], 0, j)),
                pl.BlockSpec((bm, 1), lambda j, i, be, nu: (i, 0)),
            ],
            out_specs=pl.BlockSpec((bm, tn), lambda j, i, be, nu: (i, j)),
            scratch_shapes=[pltpu.VMEM((f, tn), BF16)],
        ),
        out_shape=jax.ShapeDtypeStruct((cap, d), F32),
        compiler_params=_cparams(("arbitrary", "arbitrary")),
        name="moe_down",
    )(blk_expert, n_used, hidden, w_down, gate_buf)


def _moe_dispatch(eid, gates, n_experts, bm):
    t = eid.shape[0]
    n_assign = t * TOP_K
    flat_e = eid.reshape(-1)
    onehot = (flat_e[:, None] == jnp.arange(n_experts, dtype=jnp.int32)[None, :]).astype(jnp.int32)
    csum = jnp.cumsum(onehot, axis=0)
    rank = jnp.sum(csum * onehot, axis=1) - 1
    counts = csum[-1]
    padded = (counts + bm - 1) // bm * bm
    pad_end = jnp.cumsum(padded)
    pad_start = pad_end - padded
    slot = (pad_start[flat_e] + rank).astype(jnp.int32)
    n_blocks = -(-n_assign // bm) + n_experts
    cap = n_blocks * bm
    tok_buf = jnp.zeros((cap,), jnp.int32).at[slot].set(jnp.arange(n_assign, dtype=jnp.int32) // TOP_K)
    gate_buf = jnp.zeros((cap,), F32).at[slot].set(gates.reshape(-1))
    blk_expert = jnp.minimum(
        jnp.searchsorted(pad_end, jnp.arange(n_blocks, dtype=jnp.int32) * bm, side="right"),
        n_experts - 1).astype(jnp.int32)
    n_used = (pad_end[-1] // bm).astype(jnp.int32).reshape(1)
    return slot.reshape(t, TOP_K), tok_buf, gate_buf.reshape(cap, 1), blk_expert, n_used


def _hier_moe(xbf, eid, gates, w_gate, w_up, w_down, layer):
    n_experts = w_gate.shape[1]
    slot, tok_buf, gate_buf, blk_expert, n_used = _moe_dispatch(eid, gates, n_experts, MOE_BM)
    xs = jnp.take(xbf, tok_buf, axis=0)
    yb = _moe_experts(xs, blk_expert, n_used, gate_buf, w_gate, w_up, w_down, layer)
    return jnp.take(yb, slot[:, 0], axis=0) + jnp.take(yb, slot[:, 1], axis=0)


def _s5_prepare(lam_re, lam_im, log_dt, b_re, b_im, c_re, c_im, d_skip, lc):
    g, p, c = b_re.shape
    dt = jnp.exp(log_dt.astype(F32))[:, None]
    lr, li = lam_re.astype(F32), lam_im.astype(F32)
    mag = jnp.exp(lr * dt)
    ab_re, ab_im = mag * jnp.cos(li * dt), mag * jnp.sin(li * dt)
    den = lr * lr + li * li
    q_re = ((ab_re - 1.0) * lr + ab_im * li) / den
    q_im = (ab_im * lr - (ab_re - 1.0) * li) / den
    bb_re = q_re[..., None] * b_re - q_im[..., None] * b_im
    bb_im = q_re[..., None] * b_im + q_im[..., None] * b_re
    ks = jnp.arange(lc + 1, dtype=F32)[:, None, None]
    pmag = jnp.exp(lr * dt * ks)
    pw_re, pw_im = pmag * jnp.cos(li * dt * ks), pmag * jnp.sin(li * dt * ks)
    t_re = pw_re[:lc, :, :, None] * bb_re[None] - pw_im[:lc, :, :, None] * bb_im[None]
    t_im = pw_re[:lc, :, :, None] * bb_im[None] + pw_im[:lc, :, :, None] * bb_re[None]
    kt = (jnp.einsum("gop,kgpi->gkoi", c_re, t_re, precision=HIGHEST)
          - jnp.einsum("gop,kgpi->gkoi", c_im, t_im, precision=HIGHEST))
    s_idx = jnp.arange(lc)[:, None]
    t_idx = jnp.arange(lc)[None, :]
    tau = t_idx - s_idx
    toe = kt[:, jnp.clip(tau, 0, lc - 1)]
    toe = jnp.where((tau >= 0)[None, :, :, None, None], toe, 0.0)
    m_intra = jnp.transpose(toe, (0, 1, 4, 2, 3)).reshape(g, lc * c, lc * c)
    rev = pw_re[:lc][::-1], pw_im[:lc][::-1]
    wst_re = rev[0][:, :, :, None] * bb_re[None] - rev[1][:, :, :, None] * bb_im[None]
    wst_im = rev[0][:, :, :, None] * bb_im[None] + rev[1][:, :, :, None] * bb_re[None]
    wst_re = jnp.transpose(wst_re, (1, 0, 3, 2)).reshape(g, lc * c, p)
    wst_im = jnp.transpose(wst_im, (1, 0, 3, 2)).reshape(g, lc * c, p)
    half = (jnp.arange(g) % 2)[:, None] == jnp.arange(2)[None, :]

    def place_cols(wm):
        return jnp.where(half[:, None, :, None], wm[:, :, None, :], 0.0).reshape(g, lc * c, 2 * p)

    wy_re = c_re[:, None] * pw_re[1:].transpose(1, 0, 2)[:, :, None, :] \
        - c_im[:, None] * pw_im[1:].transpose(1, 0, 2)[:, :, None, :]
    wy_im = -(c_re[:, None] * pw_im[1:].transpose(1, 0, 2)[:, :, None, :]
              + c_im[:, None] * pw_re[1:].transpose(1, 0, 2)[:, :, None, :])
    wy_re = jnp.transpose(wy_re, (0, 3, 1, 2)).reshape(g, p, lc * c)
    wy_im = jnp.transpose(wy_im, (0, 3, 1, 2)).reshape(g, p, lc * c)

    def place_rows(wm):
        return jnp.where(half[:, :, None, None], wm[:, None, :, :], 0.0).reshape(g, 2 * p, lc * c)

    al_re = pw_re[lc].reshape(g // 2, 1, 2 * p)
    al_im = pw_im[lc].reshape(g // 2, 1, 2 * p)
    d_ch = jnp.tile(d_skip.astype(F32).reshape(g, 1, c), (1, lc, 1)).reshape(g, 1, lc * c)
    return dict(m=m_intra.astype(BF16), wst_re=place_cols(wst_re).astype(BF16),
                wst_im=place_cols(wst_im).astype(BF16), wy_re=place_rows(wy_re).astype(BF16),
                wy_im=place_rows(wy_im).astype(BF16), al_re=al_re, al_im=al_im, d=d_ch)


def _s5_kernel(uc_ref, m_ref, wsr_ref, wsi_ref, wyr_ref, wyi_ref, alr_ref, ali_ref, d_ref, s0r_ref, s0i_ref,
               h_ref, sfr_ref, sfi_ref, locr_ref, loci_ref, str_ref, sti_ref, *, gb, nb, nk):
    for pair in range(gb // 2):
        g0, g1 = 2 * pair, 2 * pair + 1
        u0 = uc_ref[g0]
        u1 = uc_ref[g1]
        ub0 = u0.astype(BF16)
        ub1 = u1.astype(BF16)
        loc_re = _dot(ub0, wsr_ref[g0]) + _dot(ub1, wsr_ref[g1])
        loc_im = _dot(ub0, wsi_ref[g0]) + _dot(ub1, wsi_ref[g1])
        a_re = alr_ref[pair]
        a_im = ali_ref[pair]
        s_re = s0r_ref[pair]
        s_im = s0i_ref[pair]
        if nk == 1:
            st_re, st_im = s_re, s_im
            f_re = a_re * s_re - a_im * s_im + loc_re
            f_im = a_re * s_im + a_im * s_re + loc_im
        else:
            locr_ref[...] = loc_re
            loci_ref[...] = loc_im

            def step(k, carry):
                c_re, c_im = carry
                rows = pl.ds(k, nb, stride=nk)
                str_ref[rows, :] = c_re
                sti_ref[rows, :] = c_im
                l_re = locr_ref[rows, :]
                l_im = loci_ref[rows, :]
                return (a_re * c_re - a_im * c_im + l_re, a_re * c_im + a_im * c_re + l_im)

            f_re, f_im = lax.fori_loop(0, nk, step, (s_re, s_im))
            st_re = str_ref[...]
            st_im = sti_ref[...]
        sfr_ref[pair] = f_re
        sfi_ref[pair] = f_im
        sb_re = st_re.astype(BF16)
        sb_im = st_im.astype(BF16)
        for gi, u, ub in ((g0, u0, ub0), (g1, u1, ub1)):
            y = (_dot(ub, m_ref[gi]) + _dot(sb_re, wyr_ref[gi]) + _dot(sb_im, wyi_ref[gi])
                 + d_ref[gi] * u)
            h_ref[gi] = jax.nn.gelu(y)


def _s5_mixer(uc, prep, s0_re, s0_im, nb, nk, gb=S5_GB):
    g, r, w = uc.shape
    p2 = prep["al_re"].shape[2]
    grp = lambda *shape: pl.BlockSpec((gb,) + shape, lambda i: (i, 0, 0))
    pr = lambda *shape: pl.BlockSpec((gb // 2,) + shape, lambda i: (i, 0, 0))
    kern = functools.partial(_s5_kernel, gb=gb, nb=nb, nk=nk)
    return pl.pallas_call(
        kern,
        grid=(g // gb,),
        in_specs=[grp(r, w), grp(w, w), grp(w, p2), grp(w, p2), grp(p2, w), grp(p2, w),
                  pr(1, p2), pr(1, p2), grp(1, w), pr(nb, p2), pr(nb, p2)],
        out_specs=[grp(r, w), pr(nb, p2), pr(nb, p2)],
        out_shape=[jax.ShapeDtypeStruct((g, r, w), F32),
                   jax.ShapeDtypeStruct((g // 2, nb, p2), F32),
                   jax.ShapeDtypeStruct((g // 2, nb, p2), F32)],
        scratch_shapes=[pltpu.VMEM((r, p2), F32) for _ in range(4)],
        compiler_params=_cparams(("parallel",)),
        name="s5_mixer",
    )(uc, prep["m"], prep["wst_re"], prep["wst_im"], prep["wy_re"], prep["wy_im"],
      prep["al_re"], prep["al_im"], prep["d"], s0_re, s0_im)


def _to_chunks(u2d, nb, nk, lc, g):
    c = u2d.shape[1] // g
    u5 = u2d.reshape(nb, nk, lc, g, c)
    return jnp.transpose(u5, (3, 0, 1, 2, 4)).reshape(g, nb * nk, lc * c)


def _from_chunks(hc, nb, nk, lc, g):
    c = hc.shape[2] // lc
    h5 = hc.reshape(g, nb, nk, lc, c)
    return jnp.transpose(h5, (1, 2, 3, 0, 4)).reshape(nb * nk * lc, g * c)


def _state_to_pairs(s):
    b, g, p = s.shape
    return jnp.transpose(s.reshape(b, g // 2, 2 * p), (1, 0, 2))


def _pairs_to_state(f):
    g2, b, p2 = f.shape
    return jnp.transpose(f, (1, 0, 2)).reshape(b, g2 * 2, p2 // 2)


def _softplus(x):
    return jnp.maximum(x, 0.0) + jnp.log1p(jnp.exp(-jnp.abs(x)))


def _ssd_kernel(*refs, prompt, q, n_heads, hd):
    it = iter(refs)
    z_ref, xs_ref, bm_ref, cm_ref, dt_ref = (next(it) for _ in range(5))
    if prompt:
        hx_ref, hb_ref, hc_ref = (next(it) for _ in range(3))
    cwx_ref, cwb_ref, cwc_ref, cbx_ref, cbb_ref, cbc_ref = (next(it) for _ in range(6))
    dtb_ref, a_ref, dsk_ref, nw_ref, sel_ref, e_ref, et_ref = (next(it) for _ in range(7))
    if not prompt:
        sin_ref = next(it)
    y_ref, sout_ref = next(it), next(it)
    padx_ref, padb_ref, padc_ref, yacc_ref = (next(it) for _ in range(4))

    r = z_ref.shape[0]
    nseq = r // q
    hpg = xs_ref.shape[1] // hd
    row = lax.broadcasted_iota(jnp.int32, (r, 1), 0)
    if prompt:
        first = pl.program_id(2) == 0
        keep = jnp.where(first, 0.0, 1.0)
        live = None
    else:
        live = (row % q) >= (q // 2)

    def conv(x_ref, halo_ref, pad_ref, w_ref, b_ref):
        if prompt:
            pad_ref[0:SUBLANES, :] = halo_ref[...] * keep
        else:
            pad_ref[0:SUBLANES, :] = jnp.zeros((SUBLANES, pad_ref.shape[1]), F32)
        pad_ref[SUBLANES:SUBLANES + r, :] = x_ref[...]
        kk = w_ref.shape[0]
        acc = b_ref[...]
        for k in range(kk):
            acc = acc + w_ref[k:k + 1, :] * pad_ref[pl.ds(SUBLANES - (kk - 1) + k, r), :]
        return _silu(acc)

    xs = conv(xs_ref, hx_ref if prompt else None, padx_ref, cwx_ref, cbx_ref)
    bc = conv(bm_ref, hb_ref if prompt else None, padb_ref, cwb_ref, cbb_ref)
    cc = conv(cm_ref, hc_ref if prompt else None, padc_ref, cwc_ref, cbc_ref)

    lane = lax.broadcasted_iota(jnp.int32, (r, LANES), 1)
    dtv = jnp.where(lane < n_heads, _softplus(dt_ref[...] + dtb_ref[...]), 0.0)
    if not prompt:
        dtv = jnp.where(live, dtv, 0.0)
    sel = sel_ref[...]
    dt8 = _dot_hi(dtv, sel)
    adt8 = _dot_hi(dtv * a_ref[...], sel)

    ri = lax.broadcasted_iota(jnp.int32, (r, r), 0)
    ci = lax.broadcasted_iota(jnp.int32, (r, r), 1)
    same = (ri // q) == (ci // q)
    tri = same & (ci <= ri)
    acum = _dot_hi(tri.astype(F32), adt8)
    atot = _dot_hi(same.astype(F32), adt8)
    acum_t = acum.T
    expand = _dot_hi(jnp.concatenate([jnp.exp(acum), dt8, jnp.exp(atot - acum)], axis=0), e_ref[...])
    eac_x, dt_x, dte_x = expand[0:r], expand[r:2 * r], expand[2 * r:3 * r]
    dec = _dot_hi(et_ref[...], jnp.exp(atot).T)

    cb = _dot_nt(cc.astype(BF16), bc.astype(BF16))
    xdt = xs * dt_x
    neg = jnp.float32(-jnp.inf)
    lane_hd = lax.broadcasted_iota(jnp.int32, (r, 2 * hd), 1)
    for pair in range(hpg // 2):
        xp = xdt[:, pair * 2 * hd:(pair + 1) * 2 * hd]
        acc = None
        for half in range(2):
            hl = 2 * pair + half
            seg = acum[:, hl:hl + 1] - acum_t[hl:hl + 1, :]
            scores = cb * jnp.exp(jnp.where(tri, seg, neg))
            mask = (lane_hd < hd) if half == 0 else (lane_hd >= hd)
            part = _dot(scores.astype(BF16), jnp.where(mask, xp, 0.0).astype(BF16))
            acc = part if acc is None else acc + part
        yacc_ref[:, pair * 2 * hd:(pair + 1) * 2 * hd] = acc

    wgt = (xdt * dte_x).astype(BF16)
    bcb = bc.astype(BF16)
    ccb = cc.astype(BF16)
    if prompt:
        @pl.when(first)
        def _():
            sout_ref[...] = jnp.zeros(sout_ref.shape, F32)

        s_old = sout_ref[...]
        yacc_ref[...] += _dot_nt(ccb, s_old.astype(BF16)) * eac_x
        sout_ref[...] = dec[:, 0:1] * s_old + _dot_tn(wgt, bcb)
    else:
        for j in range(nseq):
            rows = slice(j * q, (j + 1) * q)
            s_old = sin_ref[j]
            yacc_ref[rows, :] += _dot_nt(ccb[rows], s_old.astype(BF16)) * eac_x[rows]
            wj = jnp.where((row // q) == j, wgt, jnp.zeros_like(wgt))
            sout_ref[j] = dec[:, j * q:j * q + 1] * s_old + _dot_tn(wj, bcb)

    y = yacc_ref[...] + dsk_ref[...] * xs
    y = y * _silu(z_ref[...])
    y = y * lax.rsqrt(jnp.mean(y * y, axis=-1, keepdims=True) + LN_EPS)
    y_ref[...] = (y * nw_ref[...]).astype(y_ref.dtype)


def _ssd_constants(n_heads, hd, n_groups):
    hpg = n_heads // n_groups
    lane = jnp.arange(LANES)
    sel = (lane[None, :, None] == (jnp.arange(n_groups)[:, None, None] * hpg + lane[None, None, :])) \
        & (lane[None, None, :] < hpg)
    e = (lane[:, None] == (jnp.arange(hpg * hd) // hd)[None, :])
    return sel.astype(F32), e.astype(F32), e.T.astype(F32)


def _pad_lanes(v):
    return jnp.pad(v, ((0, 0), (0, LANES - v.shape[1])))[:, None, :]


def _ssd_params(conv_w, conv_b, dt_bias, a_log, d_skip, norm_w):
    n_heads = dt_bias.shape[1]
    hd = norm_w.shape[1] // n_heads
    return dict(
        ssd_conv_w=conv_w, ssd_conv_b3=conv_b[:, None, :],
        ssd_dt_bias3=_pad_lanes(dt_bias.astype(F32)),
        ssd_a3=_pad_lanes(-jnp.exp(a_log.astype(F32))),
        ssd_d3=jnp.repeat(d_skip.astype(F32), hd, axis=1)[:, None, :],
        ssd_norm3=norm_w[:, None, :],
        ssd_consts=_ssd_constants(n_heads, hd, SSD_GROUPS))


def _ssd_mixer(u, layer, prm, cols, *, prompt, nb, t, state_in=None, out_rows=None):
    o_z, o_x, o_dt, w_b, n_st = cols
    n_groups = SSD_GROUPS
    hd = SSD_HEAD_DIM
    n_heads = w_b // hd
    gw = w_b // n_groups
    r = SSD_CHUNK
    sel, e, et = prm["ssd_consts"]
    o_b = o_x + w_b
    o_c = o_b + n_groups * n_st
    assert o_z % gw == 0 and o_x % gw == 0 and o_b % n_st == 0 and o_dt % LANES == 0 and n_st == LANES
    cw, cb = prm["ssd_conv_w"], prm["ssd_conv_b3"]

    if prompt:
        nc = t // r
        grid = (nb, n_groups, nc)
        rowblk = lambda b, g, c: b * nc + c
        halo = lambda b, g, c: jnp.maximum((b * t + c * r) // SUBLANES - 1, 0)
        im = lambda colf: (lambda b, g, c: (rowblk(b, g, c), colf(g)))
        hm = lambda colf: (lambda b, g, c: (halo(b, g, c), colf(g)))
        pm = lambda f: (lambda b, g, c: f(g))
        q = r
    else:
        q = SAMPLE_SEQ_ROWS
        grid = (nb * q // r, n_groups)
        im = lambda colf: (lambda i, g: (i, colf(g)))
        pm = lambda f: (lambda i, g: f(g))
    col_z = lambda g: o_z // gw + g
    col_x = lambda g: o_x // gw + g
    col_b = lambda g: o_b // n_st + g
    col_c = lambda g: o_c // n_st + g
    col_dt = lambda g: o_dt // LANES

    in_specs = [pl.BlockSpec((r, gw), im(col_z)), pl.BlockSpec((r, gw), im(col_x)),
                pl.BlockSpec((r, n_st), im(col_b)), pl.BlockSpec((r, n_st), im(col_c)),
                pl.BlockSpec((r, LANES), im(col_dt))]
    args = [u, u, u, u, u]
    if prompt:
        in_specs += [pl.BlockSpec((SUBLANES, gw), hm(col_x)), pl.BlockSpec((SUBLANES, n_st), hm(col_b)),
                     pl.BlockSpec((SUBLANES, n_st), hm(col_c))]
        args += [u, u, u]
    kk = cw.shape[1]
    in_specs += [
        pl.BlockSpec((None, kk, gw), pm(lambda g: (layer, 0, g))),
        pl.BlockSpec((None, kk, n_st), pm(lambda g: (layer, 0, w_b // n_st + g))),
        pl.BlockSpec((None, kk, n_st), pm(lambda g: (layer, 0, w_b // n_st + n_groups + g))),
        pl.BlockSpec((None, 1, gw), pm(lambda g: (layer, 0, g))),
        pl.BlockSpec((None, 1, n_st), pm(lambda g: (layer, 0, w_b // n_st + g))),
        pl.BlockSpec((None, 1, n_st), pm(lambda g: (layer, 0, w_b // n_st + n_groups + g))),
        pl.BlockSpec((None, 1, LANES), pm(lambda g: (layer, 0, 0))),
        pl.BlockSpec((None, 1, LANES), pm(lambda g: (layer, 0, 0))),
        pl.BlockSpec((None, 1, gw), pm(lambda g: (layer, 0, g))),
        pl.BlockSpec((None, 1, gw), pm(lambda g: (layer, 0, g))),
        pl.BlockSpec((None, LANES, LANES), pm(lambda g: (g, 0, 0))),
        pl.BlockSpec((LANES, gw), pm(lambda g: (0, 0))),
        pl.BlockSpec((gw, LANES), pm(lambda g: (0, 0))),
    ]
    args += [cw, cw, cw, cb, cb, cb, prm["ssd_dt_bias3"], prm["ssd_a3"], prm["ssd_d3"], prm["ssd_norm3"],
             sel, e, et]
    if prompt:
        y_rows = out_rows
        out_specs = [pl.BlockSpec((r, gw), lambda b, g, c: (b * nc + c, g)),
                     pl.BlockSpec((None, gw, n_st), lambda b, g, c: (b, g, 0))]
        s_shape = (nb, w_b, n_st)
    else:
        nseq = r // q
        s_off = layer * (nb // nseq)
        in_specs.append(pl.BlockSpec((nseq, gw, n_st), lambda i, g: (i + s_off, g, 0)))
        args.append(state_in)
        y_rows = nb * q
        out_specs = [pl.BlockSpec((r, gw), lambda i, g: (i, g)),
                     pl.BlockSpec((nseq, gw, n_st), lambda i, g: (i, g, 0))]
        s_shape = (nb, w_b, n_st)
    kern = functools.partial(_ssd_kernel, prompt=prompt, q=q, n_heads=n_heads, hd=hd)
    return pl.pallas_call(
        kern, grid=grid, in_specs=in_specs, out_specs=out_specs,
        out_shape=[jax.ShapeDtypeStruct((y_rows, w_b), BF16), jax.ShapeDtypeStruct(s_shape, F32)],
        scratch_shapes=[pltpu.VMEM((r + SUBLANES, gw), F32), pltpu.VMEM((r + SUBLANES, n_st), F32),
                        pltpu.VMEM((r + SUBLANES, n_st), F32), pltpu.VMEM((r, gw), F32)],
        compiler_params=_cparams(("parallel", "parallel", "arbitrary") if prompt else ("parallel", "parallel")),
        name="ssd_prompt" if prompt else "ssd_sample",
    )(*args)


def _conf_prompt_kernel(g_ref, halo_ref, w_ref, b_ref, lg_ref, lb_ref, out_ref, pad_ref, conv_ref, *, rt):
    tt, d = g_ref.shape
    halo = halo_ref.shape[0]
    kk = w_ref.shape[1]
    nlt = d // LANES
    keep = jnp.where(pl.program_id(1) == 0, 0.0, 1.0)
    for lt in range(nlt):
        cols = slice(lt * LANES, (lt + 1) * LANES)
        pad_ref[lt, 0:halo, :] = halo_ref[:, cols] * keep
        pad_ref[lt, halo:halo + tt, :] = g_ref[:, cols]
    base = halo - (kk - 1)

    def lane_tile(lt, carry):
        wt = w_ref[lt]
        bt = b_ref[lt]
        for r0 in range(0, tt, rt):
            acc = jnp.broadcast_to(bt, (rt, LANES))
            for k in range(kk):
                acc = acc + wt[k:k + 1, :] * pad_ref[lt, base + r0 + k:base + r0 + k + rt, :]
            conv_ref[lt, r0:r0 + rt, :] = acc
        return carry

    lax.fori_loop(0, nlt, lane_tile, 0)

    s1 = conv_ref[0]
    for lt in range(1, nlt):
        s1 = s1 + conv_ref[lt]
    mu = jnp.sum(s1, axis=-1, keepdims=True) * (1.0 / d)
    s2 = None
    for lt in range(nlt):
        dv = conv_ref[lt] - mu
        s2 = dv * dv if s2 is None else s2 + dv * dv
    rstd = lax.rsqrt(jnp.sum(s2, axis=-1, keepdims=True) * (1.0 / d) + LN_EPS)
    for lt in range(nlt):
        cols = slice(lt * LANES, (lt + 1) * LANES)
        v = (conv_ref[lt] - mu) * rstd * lg_ref[:, cols] + lb_ref[:, cols]
        out_ref[:, cols] = _silu(v).astype(out_ref.dtype)


def _conf_prompt(g, layer, dw_w, dw_b3, ln_g3, ln_b3, nb, t, out_rows, tt=CONF_TT, rt=CONF_RT, halo=CONF_HALO):
    d = g.shape[1]
    n_layers, kk, _ = dw_w.shape
    assert t % tt == 0 and tt % rt == 0 and halo >= kk - 1 and tt % halo == 0
    nt = t // tt
    nlt = d // LANES
    w_tiles = jnp.transpose(dw_w.reshape(n_layers, kk, nlt, LANES), (0, 2, 1, 3))
    b_tiles = dw_b3.reshape(n_layers, nlt, 1, LANES)
    par = pl.BlockSpec((None, 1, d), lambda b, i: (layer, 0, 0))
    return pl.pallas_call(
        functools.partial(_conf_prompt_kernel, rt=rt),
        grid=(nb, nt),
        in_specs=[pl.BlockSpec((tt, d), lambda b, i: (b * nt + i, 0)),
                  pl.BlockSpec((halo, d), lambda b, i: (jnp.maximum((b * t + i * tt) // halo - 1, 0), 0)),
                  pl.BlockSpec((None, nlt, kk, LANES), lambda b, i: (layer, 0, 0, 0)),
                  pl.BlockSpec((None, nlt, 1, LANES), lambda b, i: (layer, 0, 0, 0)), par, par],
        out_specs=pl.BlockSpec((tt, d), lambda b, i: (b * nt + i, 0)),
        out_shape=jax.ShapeDtypeStruct((out_rows, d), BF16),
        scratch_shapes=[pltpu.VMEM((d // LANES, halo + tt, LANES), F32), pltpu.VMEM((d // LANES, tt, LANES), F32)],
        compiler_params=_cparams(("parallel", "parallel")),
        name="conf_prompt",
    )(g, g, w_tiles, b_tiles, ln_g3, ln_b3)


def _conf_sample_kernel(ext_ref, w_ref, b_ref, lg_ref, lb_ref, out_ref):
    kk = w_ref.shape[0]
    t_new = out_ref.shape[0]
    for t in range(t_new):
        acc = b_ref[...] + w_ref[0:1, :] * ext_ref[t]
        for k in range(1, kk):
            acc = acc + w_ref[k:k + 1, :] * ext_ref[t + k]
        out_ref[t] = _silu(_ln_rows(acc, lg_ref[...], lb_ref[...])).astype(out_ref.dtype)


def _conf_sample(ext_t, layer, dw_w, dw_b3, ln_g3, ln_b3, nbt=16):
    rows, nb, d = ext_t.shape
    kk = dw_w.shape[1]
    t_new = rows - (kk - 1)
    par = pl.BlockSpec((None, 1, d), lambda i: (layer, 0, 0))
    return pl.pallas_call(
        _conf_sample_kernel,
        grid=(nb // nbt,),
        in_specs=[pl.BlockSpec((rows, nbt, d), lambda i: (0, i, 0)),
                  pl.BlockSpec((None, kk, d), lambda i: (layer, 0, 0)), par, par, par],
        out_specs=pl.BlockSpec((t_new, nbt, d), lambda i: (0, i, 0)),
        out_shape=jax.ShapeDtypeStruct((t_new, nb, d), BF16),
        compiler_params=_cparams(("parallel",)),
        name="conf_sample",
    )(ext_t, dw_w, dw_b3, ln_g3, ln_b3)


def kernel(x_prompt, x_sample, state_s5_re, state_s5_im, state_ssm, state_ssd_conv, state_conformer_conv, w_in_even, s5_lam_re, s5_lam_im, s5_log_dt, s5_b_re, s5_b_im, s5_c_re, s5_c_im, s5_d, s5_w_glu, s5_b_glu, ssd_conv_w, ssd_conv_b, ssd_dt_bias, ssd_a_log, ssd_d, ssd_norm_w, w_out_even, conf_w_pw1, conf_b_pw1, conf_dw_w, conf_dw_b, conf_ln_g, conf_ln_b, conf_w_pw2, conf_b_pw2, ln_g, ln_b, moe_w_group, moe_b_group, moe_w_expert, moe_b_expert, moe_w_gate, moe_w_up, moe_w_down):
    bsz, seq, d = x_prompt.shape
    nb_s, t_s, _ = x_sample.shape
    depth = ln_g.shape[0]
    alpha = (2.0 * depth) ** 0.25
    mp, ms = bsz * seq, nb_s * t_s
    x = jnp.concatenate([x_prompt.reshape(mp, d), x_sample.reshape(ms, d)], axis=0)
    xbf = x.astype(BF16)

    g_a = s5_lam_re.shape[1]
    w_a = s5_d.shape[1]
    w_b = ssd_norm_w.shape[1]
    n_heads = ssd_dt_bias.shape[1]
    conv_dim = ssd_conv_w.shape[2]
    n_st = (conv_dim - w_b) // (2 * SSD_GROUPS)
    o_z, o_x = w_a, w_a + w_b
    o_dt = o_x + conv_dim
    in_even = o_dt + n_heads
    cols = (o_z, o_x, o_dt, w_b, n_st)
    ssd_prm = _ssd_params(ssd_conv_w, ssd_conv_b, ssd_dt_bias, ssd_a_log, ssd_d, ssd_norm_w)
    ssm_in = state_ssm.reshape(state_ssm.shape[0] * nb_s, w_b, n_st)

    n_eg, per_g = moe_w_expert.shape[1], moe_w_expert.shape[3]
    n_route = n_eg + n_eg * per_g
    w_route = jnp.concatenate(
        [moe_w_group, jnp.transpose(moe_w_expert, (0, 2, 1, 3)).reshape(depth, d, n_eg * per_g),
         jnp.zeros((depth, d, LANES - n_route), F32)], axis=-1)
    b_route = jnp.concatenate(
        [moe_b_group, moe_b_expert.reshape(depth, n_eg * per_g), jnp.zeros((depth, LANES - n_route), F32)],
        axis=-1)[:, None, :]

    conf_dw_b3 = conf_dw_b[:, None, :]
    conf_ln_g3 = conf_ln_g[:, None, :]
    conf_ln_b3 = conf_ln_b[:, None, :]
    s5_zero = jnp.zeros((g_a // 2, bsz, 2 * s5_lam_re.shape[2]), F32)

    out = dict(re_p=[], im_p=[], ssm_p=[], sh_p=[], ch_p=[], re_s=[], im_s=[], ssm_s=[], sh_s=[], ch_s=[])
    for layer in range(depth):
        i = layer // 2
        if layer % 2 == 0:
            u = _matmul([xbf], w_in_even, i, n_out=in_even)
            prm = (s5_lam_re[i], s5_lam_im[i], s5_log_dt[i], s5_b_re[i], s5_b_im[i], s5_c_re[i], s5_c_im[i],
                   s5_d[i])
            nk = seq // S5_CHUNK
            hc_p, fr_p, fi_p = _s5_mixer(_to_chunks(u[:mp, :w_a], bsz, nk, S5_CHUNK, g_a),
                                         _s5_prepare(*prm, S5_CHUNK), s5_zero, s5_zero, bsz, nk)
            hc_s, fr_s, fi_s = _s5_mixer(_to_chunks(u[mp:, :w_a], nb_s, 1, t_s, g_a),
                                         _s5_prepare(*prm, t_s), _state_to_pairs(state_s5_re[i]),
                                         _state_to_pairs(state_s5_im[i]), nb_s, 1)
            h = jnp.concatenate([_from_chunks(hc_p, bsz, nk, S5_CHUNK, g_a),
                                 _from_chunks(hc_s, nb_s, 1, t_s, g_a)], axis=0)
            ya = _matmul([h], s5_w_glu, i, bias=s5_b_glu, mode="gate", gate=h, out_dtype=BF16)
            out["re_p"].append(_pairs_to_state(fr_p)); out["im_p"].append(_pairs_to_state(fi_p))
            out["re_s"].append(_pairs_to_state(fr_s)); out["im_s"].append(_pairs_to_state(fi_s))
            yb, ssm_p = _ssd_mixer(u, i, ssd_prm, cols, prompt=True, nb=bsz, t=seq, out_rows=mp + ms)
            us3 = u[mp:].reshape(nb_s, t_s, in_even)
            dead = SAMPLE_SEQ_ROWS - t_s
            hist = state_ssd_conv[i]
            ext = jnp.concatenate([jnp.zeros((nb_s, dead, in_even), F32), us3], axis=1)
            ext = ext.at[:, dead - hist.shape[1]:dead, o_x:o_x + conv_dim].set(hist)
            yb_s, ssm_s = _ssd_mixer(ext.reshape(nb_s * SAMPLE_SEQ_ROWS, in_even), i, ssd_prm, cols, prompt=False,
                                     nb=nb_s, t=t_s, state_in=ssm_in)
            yb_s = yb_s.reshape(nb_s, SAMPLE_SEQ_ROWS, w_b)[:, dead:].reshape(ms, w_b)
            yb = lax.dynamic_update_slice(yb, yb_s, (mp, 0))
            mix = _matmul([ya, yb], w_out_even, i)
            kh = hist.shape[1]
            out["ssm_p"].append(ssm_p.reshape(bsz, n_heads, SSD_HEAD_DIM, n_st))
            out["ssm_s"].append(ssm_s.reshape(nb_s, n_heads, SSD_HEAD_DIM, n_st))
            out["sh_p"].append(u[:mp].reshape(bsz, seq, in_even)[:, seq - kh:, o_x:o_x + conv_dim])
            out["sh_s"].append(jnp.concatenate([hist, us3[:, :, o_x:o_x + conv_dim]], axis=1)[:, t_s:])
        else:
            gg = _matmul([xbf], conf_w_pw1, i, bias=conf_b_pw1, mode="glu")
            c = _conf_prompt(gg, i, conf_dw_w, conf_dw_b3, conf_ln_g3, conf_ln_b3, bsz, seq, mp + ms)
            hist = state_conformer_conv[i]
            g_s = gg[mp:].reshape(nb_s, t_s, d)
            ext = jnp.concatenate([hist, g_s], axis=1)
            c_s = _conf_sample(jnp.transpose(ext, (1, 0, 2)), i, conf_dw_w, conf_dw_b3, conf_ln_g3, conf_ln_b3)
            c = lax.dynamic_update_slice(c, jnp.transpose(c_s, (1, 0, 2)).reshape(ms, d), (mp, 0))
            mix = _matmul([c], conf_w_pw2, i, bias=conf_b_pw2)
            kh = hist.shape[1]
            out["ch_p"].append(gg[:mp].reshape(bsz, seq, d)[:, seq - kh:])
            out["ch_s"].append(ext[:, t_s:])
        x, xbf, eid, gates = _ln_residual(x, mix, ln_g, ln_b, layer, 0, alpha,
                                          router=(w_route[layer], b_route[layer], n_eg, per_g))
        ff = _hier_moe(xbf, eid[:, :TOP_K], gates[:, :TOP_K], moe_w_gate, moe_w_up, moe_w_down, layer)
        x, xbf = _ln_residual(x, ff, ln_g, ln_b, layer, 1, alpha)

    st = lambda k: jnp.stack(out[k])
    return (x[:mp].reshape(bsz, seq, d), x[mp:].reshape(nb_s, t_s, d),
            st("re_p"), st("im_p"), st("ssm_p"), st("sh_p"), st("ch_p"),
            st("re_s"), st("im_s"), st("ssm_s"), st("sh_s"), st("ch_s"))
```

```python
import functools
import math

import jax
import jax.numpy as jnp
from jax import lax
from jax.experimental import pallas as pl
from jax.experimental.pallas import tpu as pltpu

F32 = jnp.float32
BF16 = jnp.bfloat16
HIGHEST = lax.Precision.HIGHEST

LN_EPS = 1e-5
V7X_VMEM_LIMIT_BYTES = 56 * 1024 * 1024
LANES = 128
SUBLANES = 8

S5_GROUP = 16
SSD_HEAD_DIM = 64
SSD_GROUPS = 4
SSD_CHUNK = 128
TOP_K = 2

MM_TM = 512
MM_TN = 512
LN_TM = 256
MOE_BM = 256
MOE_TF = 512
MOE_TN = 4096
S5_CHUNK = 16
S5_GB = 8
CONF_TT = 256
CONF_RT = 32
CONF_HALO = 32
SAMPLE_SEQ_ROWS = 8


def _cparams(sem):
    return pltpu.CompilerParams(dimension_semantics=sem, vmem_limit_bytes=V7X_VMEM_LIMIT_BYTES)


def _dot(a, b):
    return jnp.dot(a, b, preferred_element_type=F32)


def _dot_hi(a, b):
    return jnp.dot(a, b, preferred_element_type=F32, precision=HIGHEST)


def _dot_nt(a, b):
    return lax.dot_general(a, b, (((1,), (1,)), ((), ())), preferred_element_type=F32)


def _dot_tn(a, b):
    return lax.dot_general(a, b, (((0,), (0,)), ((), ())), preferred_element_type=F32)


def _sigmoid(x):
    return 1.0 / (1.0 + jnp.exp(-x))


def _silu(x):
    return x * _sigmoid(x)


def _mm_kernel(*refs, n_lhs, n_col, has_bias, mode):
    pos = 0
    x_refs = refs[pos:pos + n_lhs]; pos += n_lhs
    w_refs = refs[pos:pos + n_lhs * n_col]; pos += n_lhs * n_col
    b_refs = ()
    if has_bias:
        b_refs = refs[pos:pos + n_col]; pos += n_col
    gate_ref = None
    if mode == "gate":
        gate_ref = refs[pos]; pos += 1
    out_ref = refs[pos]; pos += 1
    wbf_refs = refs[pos:]

    @pl.when(pl.program_id(1) == 0)
    def _():
        for w_ref, wbf_ref in zip(w_refs, wbf_refs):
            wbf_ref[...] = w_ref[...].astype(BF16)

    accs = []
    for c in range(n_col):
        acc = None
        for l in range(n_lhs):
            part = _dot(x_refs[l][...].astype(BF16), wbf_refs[l * n_col + c][...])
            acc = part if acc is None else acc + part
        if has_bias:
            acc = acc + b_refs[c][...]
        accs.append(acc)
    if mode == "glu":
        res = accs[0] * _sigmoid(accs[1])
    elif mode == "gate":
        res = gate_ref[...] * _sigmoid(accs[0])
    else:
        res = accs[0]
    out_ref[...] = res.astype(out_ref.dtype)


def _matmul(xs, w, layer, bias=None, mode="plain", gate=None, n_out=None, out_dtype=F32,
            tm=MM_TM, tn=MM_TN):
    n_lhs = len(xs)
    m, k = xs[0].shape
    n_total = w.shape[2]
    n_col = 2 if mode == "glu" else 1
    if n_out is None:
        n_out = n_total // n_col
    assert m % tm == 0 and w.shape[1] == n_lhs * k
    nj = pl.cdiv(n_out, tn)
    glu_off = (n_total // 2) // tn if mode == "glu" else 0
    if mode == "glu":
        assert (n_total // 2) % tn == 0

    in_specs = [pl.BlockSpec((tm, k), lambda j, i: (i, 0)) for _ in range(n_lhs)]
    args = list(xs)
    for l in range(n_lhs):
        for c in range(n_col):
            in_specs.append(pl.BlockSpec((None, k, tn),
                                         functools.partial(lambda j, i, l, c: (layer, l, j + c * glu_off), l=l, c=c)))
            args.append(w)
    if bias is not None:
        b3 = bias.reshape(bias.shape[0], 1, bias.shape[1])
        for c in range(n_col):
            in_specs.append(pl.BlockSpec((None, 1, tn),
                                         functools.partial(lambda j, i, c: (layer, 0, j + c * glu_off), c=c)))
            args.append(b3)
    if mode == "gate":
        in_specs.append(pl.BlockSpec((tm, tn), lambda j, i: (i, j)))
        args.append(gate)
    kern = functools.partial(_mm_kernel, n_lhs=n_lhs, n_col=n_col, has_bias=bias is not None, mode=mode)
    return pl.pallas_call(
        kern,
        grid=(nj, m // tm),
        in_specs=in_specs,
        out_specs=pl.BlockSpec((tm, tn), lambda j, i: (i, j)),
        out_shape=jax.ShapeDtypeStruct((m, n_out), out_dtype),
        scratch_shapes=[pltpu.VMEM((k, tn), BF16) for _ in range(n_lhs * n_col)],
        compiler_params=_cparams(("arbitrary", "arbitrary")),
        name=f"mm_{mode}",
    )(*args)


def _ln_rows(y, g, b):
    mu = jnp.mean(y, axis=-1, keepdims=True)
    yc = y - mu
    var = jnp.mean(yc * yc, axis=-1, keepdims=True)
    return yc * lax.rsqrt(var + LN_EPS) * g + b


def _first_argmax(v, lane, big):
    m = jnp.max(v, axis=-1, keepdims=True)
    idx = jnp.min(jnp.where(v == m, lane, big), axis=-1, keepdims=True)
    return m, idx


def _ln_kernel(*refs, alpha, route, n_groups, per_group):
    if route:
        x_ref, mix_ref, g_ref, b_ref, wr_ref, br_ref, out_ref, outbf_ref, eid_ref, gate_ref = refs
    else:
        x_ref, mix_ref, g_ref, b_ref, out_ref, outbf_ref = refs
    y = alpha * x_ref[...] + mix_ref[...]
    out = _ln_rows(y, g_ref[...], b_ref[...])
    out_ref[...] = out
    outbf_ref[...] = out.astype(BF16)
    if route:
        lg = _dot_hi(out, wr_ref[...]) + br_ref[...]
        lane = lax.broadcasted_iota(jnp.int32, lg.shape, 1)
        neg = jnp.float32(-jnp.inf)
        big = jnp.int32(LANES)
        gl = jnp.where(lane < n_groups, lg, neg)
        gmax, grp = _first_argmax(gl, lane, big)
        pg_top = 1.0 / jnp.sum(jnp.exp(gl - gmax), axis=-1, keepdims=True)
        lo = n_groups + grp * per_group
        el = jnp.where((lane >= lo) & (lane < lo + per_group), lg, neg)
        m1, i1 = _first_argmax(el, lane, big)
        el2 = jnp.where(lane == i1, neg, el)
        m2, i2 = _first_argmax(el2, lane, big)
        e2 = jnp.exp(m2 - m1)
        g1 = pg_top / (1.0 + e2)
        g2 = pg_top * e2 / (1.0 + e2)
        eid_ref[...] = jnp.where(lane == 0, i1 - n_groups, jnp.where(lane == 1, i2 - n_groups, 0))
        gate_ref[...] = jnp.where(lane == 0, g1, jnp.where(lane == 1, g2, 0.0))


def _ln_residual(x, mix, ln_g, ln_b, layer, which, alpha, router=None, tm=LN_TM):
    m, d = x.shape
    assert m % tm == 0
    row = pl.BlockSpec((tm, d), lambda i: (i, 0))
    par = pl.BlockSpec((None, None, 1, d), lambda i: (layer, which, 0, 0))
    g4 = ln_g.reshape(ln_g.shape[0], ln_g.shape[1], 1, d)
    b4 = ln_b.reshape(ln_b.shape[0], ln_b.shape[1], 1, d)
    in_specs = [row, row, par, par]
    args = [x, mix, g4, b4]
    out_specs = [row, row]
    out_shape = [jax.ShapeDtypeStruct((m, d), F32), jax.ShapeDtypeStruct((m, d), BF16)]
    route = router is not None
    n_groups = per_group = 0
    if route:
        wr, br, n_groups, per_group = router
        in_specs += [pl.BlockSpec((d, LANES), lambda i: (0, 0)), pl.BlockSpec((1, LANES), lambda i: (0, 0))]
        args += [wr, br]
        small = pl.BlockSpec((tm, LANES), lambda i: (i, 0))
        out_specs += [small, small]
        out_shape += [jax.ShapeDtypeStruct((m, LANES), jnp.int32), jax.ShapeDtypeStruct((m, LANES), F32)]
    kern = functools.partial(_ln_kernel, alpha=alpha, route=route, n_groups=n_groups, per_group=per_group)
    return pl.pallas_call(
        kern, grid=(m // tm,), in_specs=in_specs, out_specs=out_specs, out_shape=out_shape,
        compiler_params=_cparams(("parallel",)), name="ln_router" if route else "ln",
    )(*args)


def _moe_up_kernel(be_ref, nu_ref, xs_ref, wg_ref, wu_ref, h_ref, wgbf_ref, wubf_ref):
    i = pl.program_id(1)
    prev = be_ref[jnp.maximum(i - 1, 0)]
    fresh = (i == 0) | (be_ref[i] != prev)

    @pl.when(fresh)
    def _():
        wgbf_ref[...] = wg_ref[...].astype(BF16)
        wubf_ref[...] = wu_ref[...].astype(BF16)

    @pl.when(i < nu_ref[0])
    def _():
        x = xs_ref[...].astype(BF16)
        h = _silu(_dot(x, wgbf_ref[...])) * _dot(x, wubf_ref[...])
        h_ref[...] = h.astype(BF16)

    @pl.when(i >= nu_ref[0])
    def _():
        h_ref[...] = jnp.zeros(h_ref.shape, BF16)


def _moe_down_kernel(be_ref, nu_ref, h_ref, wd_ref, gate_ref, out_ref, wdbf_ref):
    i = pl.program_id(1)
    prev = be_ref[jnp.maximum(i - 1, 0)]
    fresh = (i == 0) | (be_ref[i] != prev)

    @pl.when(fresh)
    def _():
        wdbf_ref[...] = wd_ref[...].astype(BF16)

    @pl.when(i < nu_ref[0])
    def _():
        out_ref[...] = _dot(h_ref[...], wdbf_ref[...]) * gate_ref[...]

    @pl.when(i >= nu_ref[0])
    def _():
        out_ref[...] = jnp.zeros(out_ref.shape, F32)


def _moe_experts(xs, blk_expert, n_used, gate_buf, w_gate, w_up, w_down, layer,
                 bm=MOE_BM, tf=MOE_TF, tn=MOE_TN):
    cap, d = xs.shape
    f = w_gate.shape[3]
    n_blocks = cap // bm
    tf = min(tf, f)
    tn = min(tn, d)
    hidden = pl.pallas_call(
        _moe_up_kernel,
        grid_spec=pltpu.PrefetchScalarGridSpec(
            num_scalar_prefetch=2,
            grid=(f // tf, n_blocks),
            in_specs=[
                pl.BlockSpec((bm, d), lambda j, i, be, nu: (jnp.minimum(i, nu[0] - 1), 0)),
                pl.BlockSpec((None, None, d, tf), lambda j, i, be, nu: (layer, be[i], 0, j)),
                pl.BlockSpec((None, None, d, tf), lambda j, i, be, nu: (layer, be[i], 0, j)),
            ],
            out_specs=pl.BlockSpec((bm, tf), lambda j, i, be, nu: (i, j)),
            scratch_shapes=[pltpu.VMEM((d, tf), BF16), pltpu.VMEM((d, tf), BF16)],
        ),
        out_shape=jax.ShapeDtypeStruct((cap, f), BF16),
        compiler_params=_cparams(("arbitrary", "arbitrary")),
        name="moe_up",
    )(blk_expert, n_used, xs, w_gate, w_up)
    return pl.pallas_call(
        _moe_down_kernel,
        grid_spec=pltpu.PrefetchScalarGridSpec(
            num_scalar_prefetch=2,
            grid=(d // tn, n_blocks),
            in_specs=[
                pl.BlockSpec((bm, f), lambda j, i, be, nu: (jnp.minimum(i, nu[0] - 1), 0)),
                pl.BlockSpec((None, None, f, tn), lambda j, i, be, nu: (layer, be[i], 0, j)),
                pl.BlockSpec((bm, 1), lambda j, i, be, nu: (i, 0)),
            ],
            out_specs=pl.BlockSpec((bm, tn), lambda j, i, be, nu: (i, j)),
            scratch_shapes=[pltpu.VMEM((f, tn), BF16)],
        ),
        out_shape=jax.ShapeDtypeStruct((cap, d), F32),
        compiler_params=_cparams(("arbitrary", "arbitrary")),
        name="moe_down",
    )(blk_expert, n_used, hidden, w_down, gate_buf)


def _moe_dispatch(eid, gates, n_experts, bm):
    t = eid.shape[0]
    n_assign = t * TOP_K
    flat_e = eid.reshape(-1)
    onehot = (flat_e[:, None] == jnp.arange(n_experts, dtype=jnp.int32)[None, :]).astype(jnp.int32)
    csum = jnp.cumsum(onehot, axis=0)
    rank = jnp.sum(csum * onehot, axis=1) - 1
    counts = csum[-1]
    padded = (counts + bm - 1) // bm * bm
    pad_end = jnp.cumsum(padded)
    pad_start = pad_end - padded
    slot = (pad_start[flat_e] + rank).astype(jnp.int32)
    n_blocks = -(-n_assign // bm) + n_experts
    cap = n_blocks * bm
    tok_buf = jnp.zeros((cap,), jnp.int32).at[slot].set(jnp.arange(n_assign, dtype=jnp.int32) // TOP_K)
    gate_buf = jnp.zeros((cap,), F32).at[slot].set(gates.reshape(-1))
    blk_expert = jnp.minimum(
        jnp.searchsorted(pad_end, jnp.arange(n_blocks, dtype=jnp.int32) * bm, side="right"),
        n_experts - 1).astype(jnp.int32)
    n_used = (pad_end[-1] // bm).astype(jnp.int32).reshape(1)
    return slot.reshape(t, TOP_K), tok_buf, gate_buf.reshape(cap, 1), blk_expert, n_used


def _hier_moe(x, eid, gates, w_gate, w_up, w_down, layer):
    n_experts = w_gate.shape[1]
    slot, tok_buf, gate_buf, blk_expert, n_used = _moe_dispatch(eid, gates, n_experts, MOE_BM)
    xs = x.at[tok_buf].get(mode="promise_in_bounds")
    yb = _moe_experts(xs, blk_expert, n_used, gate_buf, w_gate, w_up, w_down, layer)
    return (yb.at[slot[:, 0]].get(mode="promise_in_bounds")
            + yb.at[slot[:, 1]].get(mode="promise_in_bounds"))


def _s5_prepare(lam_re, lam_im, log_dt, b_re, b_im, c_re, c_im, d_skip, lc):
    g, p, c = b_re.shape
    dt = jnp.exp(log_dt.astype(F32))[:, None]
    lr, li = lam_re.astype(F32), lam_im.astype(F32)
    mag = jnp.exp(lr * dt)
    ab_re, ab_im = mag * jnp.cos(li * dt), mag * jnp.sin(li * dt)
    den = lr * lr + li * li
    q_re = ((ab_re - 1.0) * lr + ab_im * li) / den
    q_im = (ab_im * lr - (ab_re - 1.0) * li) / den
    bb_re = q_re[..., None] * b_re - q_im[..., None] * b_im
    bb_im = q_re[..., None] * b_im + q_im[..., None] * b_re
    ks = jnp.arange(lc + 1, dtype=F32)[:, None, None]
    pmag = jnp.exp(lr * dt * ks)
    pw_re, pw_im = pmag * jnp.cos(li * dt * ks), pmag * jnp.sin(li * dt * ks)
    t_re = pw_re[:lc, :, :, None] * bb_re[None] - pw_im[:lc, :, :, None] * bb_im[None]
    t_im = pw_re[:lc, :, :, None] * bb_im[None] + pw_im[:lc, :, :, None] * bb_re[None]
    kt = (jnp.einsum("gop,kgpi->gkoi", c_re, t_re, precision=HIGHEST)
          - jnp.einsum("gop,kgpi->gkoi", c_im, t_im, precision=HIGHEST))
    s_idx = jnp.arange(lc)[:, None]
    t_idx = jnp.arange(lc)[None, :]
    tau = t_idx - s_idx
    toe = kt[:, jnp.clip(tau, 0, lc - 1)]
    toe = jnp.where((tau >= 0)[None, :, :, None, None], toe, 0.0)
    m_intra = jnp.transpose(toe, (0, 1, 4, 2, 3)).reshape(g, lc * c, lc * c)
    rev = pw_re[:lc][::-1], pw_im[:lc][::-1]
    wst_re = rev[0][:, :, :, None] * bb_re[None] - rev[1][:, :, :, None] * bb_im[None]
    wst_im = rev[0][:, :, :, None] * bb_im[None] + rev[1][:, :, :, None] * bb_re[None]
    wst_re = jnp.transpose(wst_re, (1, 0, 3, 2)).reshape(g, lc * c, p)
    wst_im = jnp.transpose(wst_im, (1, 0, 3, 2)).reshape(g, lc * c, p)
    half = (jnp.arange(g) % 2)[:, None] == jnp.arange(2)[None, :]

    def place_cols(wm):
        return jnp.where(half[:, None, :, None], wm[:, :, None, :], 0.0).reshape(g, lc * c, 2 * p)

    wy_re = c_re[:, None] * pw_re[1:].transpose(1, 0, 2)[:, :, None, :] \
        - c_im[:, None] * pw_im[1:].transpose(1, 0, 2)[:, :, None, :]
    wy_im = -(c_re[:, None] * pw_im[1:].transpose(1, 0, 2)[:, :, None, :]
              + c_im[:, None] * pw_re[1:].transpose(1, 0, 2)[:, :, None, :])
    wy_re = jnp.transpose(wy_re, (0, 3, 1, 2)).reshape(g, p, lc * c)
    wy_im = jnp.transpose(wy_im, (0, 3, 1, 2)).reshape(g, p, lc * c)

    def place_rows(wm):
        return jnp.where(half[:, :, None, None], wm[:, None, :, :], 0.0).reshape(g, 2 * p, lc * c)

    al_re = pw_re[lc].reshape(g // 2, 1, 2 * p)
    al_im = pw_im[lc].reshape(g // 2, 1, 2 * p)
    d_ch = jnp.tile(d_skip.astype(F32).reshape(g, 1, c), (1, lc, 1)).reshape(g, 1, lc * c)
    return dict(m=m_intra.astype(BF16), wst_re=place_cols(wst_re).astype(BF16),
                wst_im=place_cols(wst_im).astype(BF16), wy_re=place_rows(wy_re).astype(BF16),
                wy_im=place_rows(wy_im).astype(BF16), al_re=al_re, al_im=al_im, d=d_ch)


def _s5_kernel(uc_ref, m_ref, wsr_ref, wsi_ref, wyr_ref, wyi_ref, alr_ref, ali_ref, d_ref, s0r_ref, s0i_ref,
               h_ref, sfr_ref, sfi_ref, locr_ref, loci_ref, str_ref, sti_ref, *, gb, nb, nk):
    for pair in range(gb // 2):
        g0, g1 = 2 * pair, 2 * pair + 1
        u0 = uc_ref[g0]
        u1 = uc_ref[g1]
        ub0 = u0.astype(BF16)
        ub1 = u1.astype(BF16)
        loc_re = _dot(ub0, wsr_ref[g0]) + _dot(ub1, wsr_ref[g1])
        loc_im = _dot(ub0, wsi_ref[g0]) + _dot(ub1, wsi_ref[g1])
        a_re = alr_ref[pair]
        a_im = ali_ref[pair]
        s_re = s0r_ref[pair]
        s_im = s0i_ref[pair]
        if nk == 1:
            st_re, st_im = s_re, s_im
            f_re = a_re * s_re - a_im * s_im + loc_re
            f_im = a_re * s_im + a_im * s_re + loc_im
        else:
            locr_ref[...] = loc_re
            loci_ref[...] = loc_im

            def step(k, carry):
                c_re, c_im = carry
                rows = pl.ds(k, nb, stride=nk)
                str_ref[rows, :] = c_re
                sti_ref[rows, :] = c_im
                l_re = locr_ref[rows, :]
                l_im = loci_ref[rows, :]
                return (a_re * c_re - a_im * c_im + l_re, a_re * c_im + a_im * c_re + l_im)

            f_re, f_im = lax.fori_loop(0, nk, step, (s_re, s_im))
            st_re = str_ref[...]
            st_im = sti_ref[...]
        sfr_ref[pair] = f_re
        sfi_ref[pair] = f_im
        sb_re = st_re.astype(BF16)
        sb_im = st_im.astype(BF16)
        for gi, u, ub in ((g0, u0, ub0), (g1, u1, ub1)):
            y = (_dot(ub, m_ref[gi]) + _dot(sb_re, wyr_ref[gi]) + _dot(sb_im, wyi_ref[gi])
                 + d_ref[gi] * u)
            h_ref[gi] = jax.nn.gelu(y)


def _s5_mixer(uc, prep, s0_re, s0_im, nb, nk, gb=S5_GB):
    g, r, w = uc.shape
    p2 = prep["al_re"].shape[2]
    grp = lambda *shape: pl.BlockSpec((gb,) + shape, lambda i: (i, 0, 0))
    pr = lambda *shape: pl.BlockSpec((gb // 2,) + shape, lambda i: (i, 0, 0))
    kern = functools.partial(_s5_kernel, gb=gb, nb=nb, nk=nk)
    return pl.pallas_call(
        kern,
        grid=(g // gb,),
        in_specs=[grp(r, w), grp(w, w), grp(w, p2), grp(w, p2), grp(p2, w), grp(p2, w),
                  pr(1, p2), pr(1, p2), grp(1, w), pr(nb, p2), pr(nb, p2)],
        out_specs=[grp(r, w), pr(nb, p2), pr(nb, p2)],
        out_shape=[jax.ShapeDtypeStruct((g, r, w), F32),
                   jax.ShapeDtypeStruct((g // 2, nb, p2), F32),
                   jax.ShapeDtypeStruct((g // 2, nb, p2), F32)],
        scratch_shapes=[pltpu.VMEM((r, p2), F32) for _ in range(4)],
        compiler_params=_cparams(("parallel",)),
        name="s5_mixer",
    )(uc, prep["m"], prep["wst_re"], prep["wst_im"], prep["wy_re"], prep["wy_im"],
      prep["al_re"], prep["al_im"], prep["d"], s0_re, s0_im)


def _to_chunks(u2d, nb, nk, lc, g):
    c = u2d.shape[1] // g
    u5 = u2d.reshape(nb, nk, lc, g, c)
    return jnp.transpose(u5, (3, 0, 1, 2, 4)).reshape(g, nb * nk, lc * c)


def _from_chunks(hc, nb, nk, lc, g):
    c = hc.shape[2] // lc
    h5 = hc.reshape(g, nb, nk, lc, c)
    return jnp.transpose(h5, (1, 2, 3, 0, 4)).reshape(nb * nk * lc, g * c)


def _state_to_pairs(s):
    b, g, p = s.shape
    return jnp.transpose(s.reshape(b, g // 2, 2 * p), (1, 0, 2))


def _pairs_to_state(f):
    g2, b, p2 = f.shape
    return jnp.transpose(f, (1, 0, 2)).reshape(b, g2 * 2, p2 // 2)


def _softplus(x):
    return jnp.maximum(x, 0.0) + jnp.log1p(jnp.exp(-jnp.abs(x)))


def _ssd_kernel(*refs, prompt, q, n_heads, hd, has_prev):
    it = iter(refs)
    z_ref, xs_ref, bm_ref, cm_ref, dt_ref = (next(it) for _ in range(5))
    if prompt:
        hx_ref, hb_ref, hc_ref = (next(it) for _ in range(3))
    cwx_ref, cwb_ref, cwc_ref, cbx_ref, cbb_ref, cbc_ref = (next(it) for _ in range(6))
    dtb_ref, a_ref, dsk_ref, nw_ref, sel_ref, e_ref, et_ref = (next(it) for _ in range(7))
    if not prompt:
        sin_ref = next(it)
    if has_prev:
        next(it)
    y_ref, sout_ref = next(it), next(it)
    padx_ref, padb_ref, padc_ref, yacc_ref = (next(it) for _ in range(4))

    r = z_ref.shape[0]
    nseq = r // q
    hpg = xs_ref.shape[1] // hd
    row = lax.broadcasted_iota(jnp.int32, (r, 1), 0)
    if prompt:
        first = pl.program_id(2) == 0
        keep = jnp.where(first, 0.0, 1.0)
        live = None
    else:
        live = (row % q) >= (q // 2)

    def conv(x_ref, halo_ref, pad_ref, w_ref, b_ref):
        if prompt:
            pad_ref[0:SUBLANES, :] = halo_ref[...] * keep
        else:
            pad_ref[0:SUBLANES, :] = jnp.zeros((SUBLANES, pad_ref.shape[1]), F32)
        pad_ref[SUBLANES:SUBLANES + r, :] = x_ref[...]
        kk = w_ref.shape[0]
        acc = b_ref[...]
        for k in range(kk):
            acc = acc + w_ref[k:k + 1, :] * pad_ref[pl.ds(SUBLANES - (kk - 1) + k, r), :]
        return _silu(acc)

    xs = conv(xs_ref, hx_ref if prompt else None, padx_ref, cwx_ref, cbx_ref)
    bc = conv(bm_ref, hb_ref if prompt else None, padb_ref, cwb_ref, cbb_ref)
    cc = conv(cm_ref, hc_ref if prompt else None, padc_ref, cwc_ref, cbc_ref)

    lane = lax.broadcasted_iota(jnp.int32, (r, LANES), 1)
    dtv = jnp.where(lane < n_heads, _softplus(dt_ref[...] + dtb_ref[...]), 0.0)
    if not prompt:
        dtv = jnp.where(live, dtv, 0.0)
    sel = sel_ref[...]
    dt8 = _dot_hi(dtv, sel)
    adt8 = _dot_hi(dtv * a_ref[...], sel)

    ri = lax.broadcasted_iota(jnp.int32, (r, r), 0)
    ci = lax.broadcasted_iota(jnp.int32, (r, r), 1)
    same = (ri // q) == (ci // q)
    tri = same & (ci <= ri)
    acum = _dot_hi(tri.astype(F32), adt8)
    atot = _dot_hi(same.astype(F32), adt8)
    acum_t = acum.T
    expand = _dot_hi(jnp.concatenate([jnp.exp(acum), dt8, jnp.exp(atot - acum)], axis=0), e_ref[...])
    eac_x, dt_x, dte_x = expand[0:r], expand[r:2 * r], expand[2 * r:3 * r]
    dec = _dot_hi(et_ref[...], jnp.exp(atot).T)

    cb = _dot_nt(cc.astype(BF16), bc.astype(BF16))
    xdt = xs * dt_x
    neg = jnp.float32(-jnp.inf)
    lane_hd = lax.broadcasted_iota(jnp.int32, (r, 2 * hd), 1)
    for pair in range(hpg // 2):
        xp = xdt[:, pair * 2 * hd:(pair + 1) * 2 * hd]
        acc = None
        for half in range(2):
            hl = 2 * pair + half
            seg = acum[:, hl:hl + 1] - acum_t[hl:hl + 1, :]
            scores = cb * jnp.exp(jnp.where(tri, seg, neg))
            mask = (lane_hd < hd) if half == 0 else (lane_hd >= hd)
            part = _dot(scores.astype(BF16), jnp.where(mask, xp, 0.0).astype(BF16))
            acc = part if acc is None else acc + part
        yacc_ref[:, pair * 2 * hd:(pair + 1) * 2 * hd] = acc

    wgt = (xdt * dte_x).astype(BF16)
    bcb = bc.astype(BF16)
    ccb = cc.astype(BF16)
    if prompt:
        @pl.when(first)
        def _():
            sout_ref[...] = jnp.zeros(sout_ref.shape, F32)

        s_old = sout_ref[...]
        yacc_ref[...] += _dot_nt(ccb, s_old.astype(BF16)) * eac_x
        sout_ref[...] = dec[:, 0:1] * s_old + _dot_tn(wgt, bcb)
    else:
        for j in range(nseq):
            rows = slice(j * q, (j + 1) * q)
            s_old = sin_ref[j]
            yacc_ref[rows, :] += _dot_nt(ccb[rows], s_old.astype(BF16)) * eac_x[rows]
            wj = jnp.where((row // q) == j, wgt, jnp.zeros_like(wgt))
            sout_ref[j] = dec[:, j * q:j * q + 1] * s_old + _dot_tn(wj, bcb)

    y = yacc_ref[...] + dsk_ref[...] * xs
    y = y * _silu(z_ref[...])
    y = y * lax.rsqrt(jnp.mean(y * y, axis=-1, keepdims=True) + LN_EPS)
    y_ref[...] = (y * nw_ref[...]).astype(y_ref.dtype)


def _ssd_constants(n_heads, hd, n_groups):
    hpg = n_heads // n_groups
    lane = jnp.arange(LANES)
    sel = (lane[None, :, None] == (jnp.arange(n_groups)[:, None, None] * hpg + lane[None, None, :])) \
        & (lane[None, None, :] < hpg)
    e = (lane[:, None] == (jnp.arange(hpg * hd) // hd)[None, :])
    return sel.astype(F32), e.astype(F32), e.T.astype(F32)


def _pad_lanes(v):
    return jnp.pad(v, ((0, 0), (0, LANES - v.shape[1])))[:, None, :]


def _ssd_params(conv_w, conv_b, dt_bias, a_log, d_skip, norm_w):
    n_heads = dt_bias.shape[1]
    hd = norm_w.shape[1] // n_heads
    return dict(
        ssd_conv_w=conv_w, ssd_conv_b3=conv_b[:, None, :],
        ssd_dt_bias3=_pad_lanes(dt_bias.astype(F32)),
        ssd_a3=_pad_lanes(-jnp.exp(a_log.astype(F32))),
        ssd_d3=jnp.repeat(d_skip.astype(F32), hd, axis=1)[:, None, :],
        ssd_norm3=norm_w[:, None, :],
        ssd_consts=_ssd_constants(n_heads, hd, SSD_GROUPS))


def _ssd_mixer(u, layer, prm, cols, *, prompt, nb, t, state_in=None, out_rows=None, state_prev=None):
    aliases = {}
    o_z, o_x, o_dt, w_b, n_st = cols
    n_groups = SSD_GROUPS
    hd = SSD_HEAD_DIM
    n_heads = w_b // hd
    gw = w_b // n_groups
    r = SSD_CHUNK
    sel, e, et = prm["ssd_consts"]
    o_b = o_x + w_b
    o_c = o_b + n_groups * n_st
    assert o_z % gw == 0 and o_x % gw == 0 and o_b % n_st == 0 and o_dt % LANES == 0 and n_st == LANES
    cw, cb = prm["ssd_conv_w"], prm["ssd_conv_b3"]

    if prompt:
        nc = t // r
        grid = (nb, n_groups, nc)
        rowblk = lambda b, g, c: b * nc + c
        halo = lambda b, g, c: jnp.maximum((b * t + c * r) // SUBLANES - 1, 0)
        im = lambda colf: (lambda b, g, c: (rowblk(b, g, c), colf(g)))
        hm = lambda colf: (lambda b, g, c: (halo(b, g, c), colf(g)))
        pm = lambda f: (lambda b, g, c: f(g))
        q = r
    else:
        q = SAMPLE_SEQ_ROWS
        grid = (nb * q // r, n_groups)
        im = lambda colf: (lambda i, g: (i, colf(g)))
        pm = lambda f: (lambda i, g: f(g))
    col_z = lambda g: o_z // gw + g
    col_x = lambda g: o_x // gw + g
    col_b = lambda g: o_b // n_st + g
    col_c = lambda g: o_c // n_st + g
    col_dt = lambda g: o_dt // LANES

    in_specs = [pl.BlockSpec((r, gw), im(col_z)), pl.BlockSpec((r, gw), im(col_x)),
                pl.BlockSpec((r, n_st), im(col_b)), pl.BlockSpec((r, n_st), im(col_c)),
                pl.BlockSpec((r, LANES), im(col_dt))]
    args = [u, u, u, u, u]
    if prompt:
        in_specs += [pl.BlockSpec((SUBLANES, gw), hm(col_x)), pl.BlockSpec((SUBLANES, n_st), hm(col_b)),
                     pl.BlockSpec((SUBLANES, n_st), hm(col_c))]
        args += [u, u, u]
    kk = cw.shape[1]
    in_specs += [
        pl.BlockSpec((None, kk, gw), pm(lambda g: (layer, 0, g))),
        pl.BlockSpec((None, kk, n_st), pm(lambda g: (layer, 0, w_b // n_st + g))),
        pl.BlockSpec((None, kk, n_st), pm(lambda g: (layer, 0, w_b // n_st + n_groups + g))),
        pl.BlockSpec((None, 1, gw), pm(lambda g: (layer, 0, g))),
        pl.BlockSpec((None, 1, n_st), pm(lambda g: (layer, 0, w_b // n_st + g))),
        pl.BlockSpec((None, 1, n_st), pm(lambda g: (layer, 0, w_b // n_st + n_groups + g))),
        pl.BlockSpec((None, 1, LANES), pm(lambda g: (layer, 0, 0))),
        pl.BlockSpec((None, 1, LANES), pm(lambda g: (layer, 0, 0))),
        pl.BlockSpec((None, 1, gw), pm(lambda g: (layer, 0, g))),
        pl.BlockSpec((None, 1, gw), pm(lambda g: (layer, 0, g))),
        pl.BlockSpec((None, LANES, LANES), pm(lambda g: (g, 0, 0))),
        pl.BlockSpec((LANES, gw), pm(lambda g: (0, 0))),
        pl.BlockSpec((gw, LANES), pm(lambda g: (0, 0))),
    ]
    args += [cw, cw, cw, cb, cb, cb, prm["ssd_dt_bias3"], prm["ssd_a3"], prm["ssd_d3"], prm["ssd_norm3"],
             sel, e, et]
    if prompt:
        y_rows = out_rows
        out_specs = [pl.BlockSpec((r, gw), lambda b, g, c: (b * nc + c, g)),
                     pl.BlockSpec((None, gw, n_st), lambda b, g, c: (b, g, 0))]
        s_shape = (nb, w_b, n_st)
    else:
        nseq = r // q
        s_off = layer * (nb // nseq)
        in_specs.append(pl.BlockSpec((nseq, gw, n_st), lambda i, g: (i + s_off, g, 0)))
        args.append(state_in)
        y_rows = nb * q
        out_specs = [pl.BlockSpec((r, gw), lambda i, g: (i, g)),
                     pl.BlockSpec((nseq, gw, n_st), lambda i, g: (i + s_off, g, 0))]
        s_shape = state_in.shape
        if state_prev is not None:
            in_specs.append(pl.BlockSpec(memory_space=pl.ANY))
            args.append(state_prev)
            aliases = {len(args) - 1: 1}
    kern = functools.partial(_ssd_kernel, prompt=prompt, q=q, n_heads=n_heads, hd=hd,
                             has_prev=state_prev is not None)
    return pl.pallas_call(
        kern, grid=grid, in_specs=in_specs, out_specs=out_specs, input_output_aliases=aliases,
        out_shape=[jax.ShapeDtypeStruct((y_rows, w_b), BF16), jax.ShapeDtypeStruct(s_shape, F32)],
        scratch_shapes=[pltpu.VMEM((r + SUBLANES, gw), F32), pltpu.VMEM((r + SUBLANES, n_st), F32),
                        pltpu.VMEM((r + SUBLANES, n_st), F32), pltpu.VMEM((r, gw), F32)],
        compiler_params=_cparams(("parallel", "parallel", "arbitrary") if prompt else ("parallel", "parallel")),
        name="ssd_prompt" if prompt else "ssd_sample",
    )(*args)


def _conf_prompt_kernel(g_ref, halo_ref, w_ref, b_ref, lg_ref, lb_ref, out_ref, pad_ref, conv_ref, *, rt):
    tt, d = g_ref.shape
    halo = halo_ref.shape[0]
    kk = w_ref.shape[1]
    nlt = d // LANES
    keep = jnp.where(pl.program_id(1) == 0, 0.0, 1.0)
    for lt in range(nlt):
        cols = slice(lt * LANES, (lt + 1) * LANES)
        pad_ref[lt, 0:halo, :] = halo_ref[:, cols] * keep
        pad_ref[lt, halo:halo + tt, :] = g_ref[:, cols]
    base = halo - (kk - 1)

    def lane_tile(lt, carry):
        wt = w_ref[lt]
        bt = b_ref[lt]
        for r0 in range(0, tt, rt):
            acc = jnp.broadcast_to(bt, (rt, LANES))
            for k in range(kk):
                acc = acc + wt[k:k + 1, :] * pad_ref[lt, base + r0 + k:base + r0 + k + rt, :]
            conv_ref[lt, r0:r0 + rt, :] = acc
        return carry

    lax.fori_loop(0, nlt, lane_tile, 0)

    s1 = conv_ref[0]
    for lt in range(1, nlt):
        s1 = s1 + conv_ref[lt]
    mu = jnp.sum(s1, axis=-1, keepdims=True) * (1.0 / d)
    s2 = None
    for lt in range(nlt):
        dv = conv_ref[lt] - mu
        s2 = dv * dv if s2 is None else s2 + dv * dv
    rstd = lax.rsqrt(jnp.sum(s2, axis=-1, keepdims=True) * (1.0 / d) + LN_EPS)
    for lt in range(nlt):
        cols = slice(lt * LANES, (lt + 1) * LANES)
        v = (conv_ref[lt] - mu) * rstd * lg_ref[:, cols] + lb_ref[:, cols]
        out_ref[:, cols] = _silu(v).astype(out_ref.dtype)


def _conf_prompt(g, layer, dw_w, dw_b3, ln_g3, ln_b3, nb, t, out_rows, tt=CONF_TT, rt=CONF_RT, halo=CONF_HALO):
    d = g.shape[1]
    n_layers, kk, _ = dw_w.shape
    assert t % tt == 0 and tt % rt == 0 and halo >= kk - 1 and tt % halo == 0
    nt = t // tt
    nlt = d // LANES
    w_tiles = jnp.transpose(dw_w.reshape(n_layers, kk, nlt, LANES), (0, 2, 1, 3))
    b_tiles = dw_b3.reshape(n_layers, nlt, 1, LANES)
    par = pl.BlockSpec((None, 1, d), lambda b, i: (layer, 0, 0))
    return pl.pallas_call(
        functools.partial(_conf_prompt_kernel, rt=rt),
        grid=(nb, nt),
        in_specs=[pl.BlockSpec((tt, d), lambda b, i: (b * nt + i, 0)),
                  pl.BlockSpec((halo, d), lambda b, i: (jnp.maximum((b * t + i * tt) // halo - 1, 0), 0)),
                  pl.BlockSpec((None, nlt, kk, LANES), lambda b, i: (layer, 0, 0, 0)),
                  pl.BlockSpec((None, nlt, 1, LANES), lambda b, i: (layer, 0, 0, 0)), par, par],
        out_specs=pl.BlockSpec((tt, d), lambda b, i: (b * nt + i, 0)),
        out_shape=jax.ShapeDtypeStruct((out_rows, d), BF16),
        scratch_shapes=[pltpu.VMEM((d // LANES, halo + tt, LANES), F32), pltpu.VMEM((d // LANES, tt, LANES), F32)],
        compiler_params=_cparams(("parallel", "parallel")),
        name="conf_prompt",
    )(g, g, w_tiles, b_tiles, ln_g3, ln_b3)


def _conf_sample_kernel(ext_ref, w_ref, b_ref, lg_ref, lb_ref, out_ref):
    kk = w_ref.shape[0]
    t_new = out_ref.shape[0]
    for t in range(t_new):
        acc = b_ref[...] + w_ref[0:1, :] * ext_ref[t]
        for k in range(1, kk):
            acc = acc + w_ref[k:k + 1, :] * ext_ref[t + k]
        out_ref[t] = _silu(_ln_rows(acc, lg_ref[...], lb_ref[...])).astype(out_ref.dtype)


def _conf_sample(ext_t, layer, dw_w, dw_b3, ln_g3, ln_b3, nbt=16):
    rows, nb, d = ext_t.shape
    kk = dw_w.shape[1]
    t_new = rows - (kk - 1)
    par = pl.BlockSpec((None, 1, d), lambda i: (layer, 0, 0))
    return pl.pallas_call(
        _conf_sample_kernel,
        grid=(nb // nbt,),
        in_specs=[pl.BlockSpec((rows, nbt, d), lambda i: (0, i, 0)),
                  pl.BlockSpec((None, kk, d), lambda i: (layer, 0, 0)), par, par, par],
        out_specs=pl.BlockSpec((t_new, nbt, d), lambda i: (0, i, 0)),
        out_shape=jax.ShapeDtypeStruct((t_new, nb, d), BF16),
        compiler_params=_cparams(("parallel",)),
        name="conf_sample",
    )(ext_t, dw_w, dw_b3, ln_g3, ln_b3)


def kernel(x_prompt, x_sample, state_s5_re, state_s5_im, state_ssm, state_ssd_conv, state_conformer_conv, w_in_even, s5_lam_re, s5_lam_im, s5_log_dt, s5_b_re, s5_b_im, s5_c_re, s5_c_im, s5_d, s5_w_glu, s5_b_glu, ssd_conv_w, ssd_conv_b, ssd_dt_bias, ssd_a_log, ssd_d, ssd_norm_w, w_out_even, conf_w_pw1, conf_b_pw1, conf_dw_w, conf_dw_b, conf_ln_g, conf_ln_b, conf_w_pw2, conf_b_pw2, ln_g, ln_b, moe_w_group, moe_b_group, moe_w_expert, moe_b_expert, moe_w_gate, moe_w_up, moe_w_down):
    bsz, seq, d = x_prompt.shape
    nb_s, t_s, _ = x_sample.shape
    depth = ln_g.shape[0]
    alpha = (2.0 * depth) ** 0.25
    mp, ms = bsz * seq, nb_s * t_s
    x = jnp.concatenate([x_prompt.reshape(mp, d), x_sample.reshape(ms, d)], axis=0)
    xbf = x.astype(BF16)

    g_a = s5_lam_re.shape[1]
    w_a = s5_d.shape[1]
    w_b = ssd_norm_w.shape[1]
    n_heads = ssd_dt_bias.shape[1]
    conv_dim = ssd_conv_w.shape[2]
    n_st = (conv_dim - w_b) // (2 * SSD_GROUPS)
    o_z, o_x = w_a, w_a + w_b
    o_dt = o_x + conv_dim
    in_even = o_dt + n_heads
    cols = (o_z, o_x, o_dt, w_b, n_st)
    ssd_prm = _ssd_params(ssd_conv_w, ssd_conv_b, ssd_dt_bias, ssd_a_log, ssd_d, ssd_norm_w)
    ssm_in = state_ssm.reshape(state_ssm.shape[0] * nb_s, w_b, n_st)

    n_eg, per_g = moe_w_expert.shape[1], moe_w_expert.shape[3]
    n_route = n_eg + n_eg * per_g
    w_route = jnp.concatenate(
        [moe_w_group, jnp.transpose(moe_w_expert, (0, 2, 1, 3)).reshape(depth, d, n_eg * per_g),
         jnp.zeros((depth, d, LANES - n_route), F32)], axis=-1)
    b_route = jnp.concatenate(
        [moe_b_group, moe_b_expert.reshape(depth, n_eg * per_g), jnp.zeros((depth, LANES - n_route), F32)],
        axis=-1)[:, None, :]

    conf_dw_b3 = conf_dw_b[:, None, :]
    conf_ln_g3 = conf_ln_g[:, None, :]
    conf_ln_b3 = conf_ln_b[:, None, :]
    s5_zero = jnp.zeros((g_a // 2, bsz, 2 * s5_lam_re.shape[2]), F32)

    out = dict(re_p=[], im_p=[], ssm_p=[], sh_p=[], ch_p=[], re_s=[], im_s=[], sh_s=[], ch_s=[])
    ssm_s = None
    for layer in range(depth):
        i = layer // 2
        if layer % 2 == 0:
            u = _matmul([xbf], w_in_even, i, n_out=in_even)
            prm = (s5_lam_re[i], s5_lam_im[i], s5_log_dt[i], s5_b_re[i], s5_b_im[i], s5_c_re[i], s5_c_im[i],
                   s5_d[i])
            nk = seq // S5_CHUNK
            hc_p, fr_p, fi_p = _s5_mixer(_to_chunks(u[:mp, :w_a], bsz, nk, S5_CHUNK, g_a),
                                         _s5_prepare(*prm, S5_CHUNK), s5_zero, s5_zero, bsz, nk)
            hc_s, fr_s, fi_s = _s5_mixer(_to_chunks(u[mp:, :w_a], nb_s, 1, t_s, g_a),
                                         _s5_prepare(*prm, t_s), _state_to_pairs(state_s5_re[i]),
                                         _state_to_pairs(state_s5_im[i]), nb_s, 1)
            h = jnp.concatenate([_from_chunks(hc_p, bsz, nk, S5_CHUNK, g_a),
                                 _from_chunks(hc_s, nb_s, 1, t_s, g_a)], axis=0)
            ya = _matmul([h], s5_w_glu, i, bias=s5_b_glu, mode="gate", gate=h, out_dtype=BF16)
            out["re_p"].append(_pairs_to_state(fr_p)); out["im_p"].append(_pairs_to_state(fi_p))
            out["re_s"].append(_pairs_to_state(fr_s)); out["im_s"].append(_pairs_to_state(fi_s))
            yb, ssm_p = _ssd_mixer(u, i, ssd_prm, cols, prompt=True, nb=bsz, t=seq, out_rows=mp + ms)
            us3 = u[mp:].reshape(nb_s, t_s, in_even)
            dead = SAMPLE_SEQ_ROWS - t_s
            hist = state_ssd_conv[i]
            ext = jnp.concatenate([jnp.zeros((nb_s, dead, in_even), F32), us3], axis=1)
            ext = ext.at[:, dead - hist.shape[1]:dead, o_x:o_x + conv_dim].set(hist)
            yb_s, ssm_s = _ssd_mixer(ext.reshape(nb_s * SAMPLE_SEQ_ROWS, in_even), i, ssd_prm, cols, prompt=False,
                                     nb=nb_s, t=t_s, state_in=ssm_in, state_prev=ssm_s)
            yb_s = yb_s.reshape(nb_s, SAMPLE_SEQ_ROWS, w_b)[:, dead:].reshape(ms, w_b)
            yb = lax.dynamic_update_slice(yb, yb_s, (mp, 0))
            mix = _matmul([ya, yb], w_out_even, i)
            kh = hist.shape[1]
            out["ssm_p"].append(ssm_p.reshape(bsz, n_heads, SSD_HEAD_DIM, n_st))
            out["sh_p"].append(jnp.stack([u[(b + 1) * seq - kh:(b + 1) * seq, o_x:o_x + conv_dim]
                                          for b in range(bsz)]))
            out["sh_s"].append(jnp.concatenate([hist, us3[:, :, o_x:o_x + conv_dim]], axis=1)[:, t_s:])
        else:
            gg = _matmul([xbf], conf_w_pw1, i, bias=conf_b_pw1, mode="glu")
            c = _conf_prompt(gg, i, conf_dw_w, conf_dw_b3, conf_ln_g3, conf_ln_b3, bsz, seq, mp + ms)
            hist = state_conformer_conv[i]
            g_s = gg[mp:].reshape(nb_s, t_s, d)
            ext = jnp.concatenate([hist, g_s], axis=1)
            c_s = _conf_sample(jnp.transpose(ext, (1, 0, 2)), i, conf_dw_w, conf_dw_b3, conf_ln_g3, conf_ln_b3)
            c = lax.dynamic_update_slice(c, jnp.transpose(c_s, (1, 0, 2)).reshape(ms, d), (mp, 0))
            mix = _matmul([c], conf_w_pw2, i, bias=conf_b_pw2)
            kh = hist.shape[1]
            out["ch_p"].append(jnp.stack([gg[(b + 1) * seq - kh:(b + 1) * seq] for b in range(bsz)]))
            out["ch_s"].append(ext[:, t_s:])
        x, xbf, eid, gates = _ln_residual(x, mix, ln_g, ln_b, layer, 0, alpha,
                                          router=(w_route[layer], b_route[layer], n_eg, per_g))
        ff = _hier_moe(x, eid[:, :TOP_K], gates[:, :TOP_K], moe_w_gate, moe_w_up, moe_w_down, layer)
        x, xbf = _ln_residual(x, ff, ln_g, ln_b, layer, 1, alpha)

    st = lambda k: jnp.stack(out[k])
    return (x[:mp].reshape(bsz, seq, d), x[mp:].reshape(nb_s, t_s, d),
            st("re_p"), st("im_p"), st("ssm_p"), st("sh_p"), st("ch_p"),
            st("re_s"), st("im_s"), ssm_s.reshape(state_ssm.shape), st("sh_s"), st("ch_s"))
```

```python
import functools
import math

import jax
import jax.numpy as jnp
from jax import lax
from jax.experimental import pallas as pl
from jax.experimental.pallas import tpu as pltpu

F32 = jnp.float32
BF16 = jnp.bfloat16
HIGHEST = lax.Precision.HIGHEST

LN_EPS = 1e-5
V7X_VMEM_LIMIT_BYTES = 56 * 1024 * 1024
LANES = 128
SUBLANES = 8

S5_GROUP = 16
SSD_HEAD_DIM = 64
SSD_GROUPS = 4
SSD_CHUNK = 128
TOP_K = 2

MM_TM = 512
MM_TN = 512
LN_TM = 256
MOE_BM = 256
MOE_TF = 512
MOE_TN = 4096
S5_CHUNK = 16
S5_GB = 8
CONF_TT = 256
CONF_RT = 32
CONF_HALO = 32
SAMPLE_SEQ_ROWS = 8


def _cparams(sem):
    return pltpu.CompilerParams(dimension_semantics=sem, vmem_limit_bytes=V7X_VMEM_LIMIT_BYTES)


def _dot(a, b):
    return jnp.dot(a, b, preferred_element_type=F32)


def _dot_hi(a, b):
    return jnp.dot(a, b, preferred_element_type=F32, precision=HIGHEST)


def _dot_nt(a, b):
    return lax.dot_general(a, b, (((1,), (1,)), ((), ())), preferred_element_type=F32)


def _dot_tn(a, b):
    return lax.dot_general(a, b, (((0,), (0,)), ((), ())), preferred_element_type=F32)


def _sigmoid(x):
    return 1.0 / (1.0 + jnp.exp(-x))


def _silu(x):
    return x * _sigmoid(x)


def _mm_kernel(*refs, n_lhs, n_col, has_bias, mode):
    pos = 0
    x_refs = refs[pos:pos + n_lhs]; pos += n_lhs
    w_refs = refs[pos:pos + n_lhs * n_col]; pos += n_lhs * n_col
    b_refs = ()
    if has_bias:
        b_refs = refs[pos:pos + n_col]; pos += n_col
    gate_ref = None
    if mode == "gate":
        gate_ref = refs[pos]; pos += 1
    out_ref = refs[pos]; pos += 1
    wbf_refs = refs[pos:]

    @pl.when(pl.program_id(1) == 0)
    def _():
        for w_ref, wbf_ref in zip(w_refs, wbf_refs):
            wbf_ref[...] = w_ref[...].astype(BF16)

    accs = []
    for c in range(n_col):
        acc = None
        for l in range(n_lhs):
            part = _dot(x_refs[l][...].astype(BF16), wbf_refs[l * n_col + c][...])
            acc = part if acc is None else acc + part
        if has_bias:
            acc = acc + b_refs[c][...]
        accs.append(acc)
    if mode == "glu":
        res = accs[0] * _sigmoid(accs[1])
    elif mode == "gate":
        res = gate_ref[...] * _sigmoid(accs[0])
    else:
        res = accs[0]
    out_ref[...] = res.astype(out_ref.dtype)


def _matmul(xs, w, layer, bias=None, mode="plain", gate=None, n_out=None, out_dtype=F32,
            tm=MM_TM, tn=MM_TN):
    n_lhs = len(xs)
    m, k = xs[0].shape
    n_total = w.shape[2]
    n_col = 2 if mode == "glu" else 1
    if n_out is None:
        n_out = n_total // n_col
    assert m % tm == 0 and w.shape[1] == n_lhs * k
    nj = pl.cdiv(n_out, tn)
    glu_off = (n_total // 2) // tn if mode == "glu" else 0
    if mode == "glu":
        assert (n_total // 2) % tn == 0

    in_specs = [pl.BlockSpec((tm, k), lambda j, i: (i, 0)) for _ in range(n_lhs)]
    args = list(xs)
    for l in range(n_lhs):
        for c in range(n_col):
            in_specs.append(pl.BlockSpec((None, k, tn),
                                         functools.partial(lambda j, i, l, c: (layer, l, j + c * glu_off), l=l, c=c)))
            args.append(w)
    if bias is not None:
        b3 = bias.reshape(bias.shape[0], 1, bias.shape[1])
        for c in range(n_col):
            in_specs.append(pl.BlockSpec((None, 1, tn),
                                         functools.partial(lambda j, i, c: (layer, 0, j + c * glu_off), c=c)))
            args.append(b3)
    if mode == "gate":
        in_specs.append(pl.BlockSpec((tm, tn), lambda j, i: (i, j)))
        args.append(gate)
    kern = functools.partial(_mm_kernel, n_lhs=n_lhs, n_col=n_col, has_bias=bias is not None, mode=mode)
    return pl.pallas_call(
        kern,
        grid=(nj, m // tm),
        in_specs=in_specs,
        out_specs=pl.BlockSpec((tm, tn), lambda j, i: (i, j)),
        out_shape=jax.ShapeDtypeStruct((m, n_out), out_dtype),
        scratch_shapes=[pltpu.VMEM((k, tn), BF16) for _ in range(n_lhs * n_col)],
        compiler_params=_cparams(("arbitrary", "arbitrary")),
        name=f"mm_{mode}",
    )(*args)


def _ln_rows(y, g, b):
    mu = jnp.mean(y, axis=-1, keepdims=True)
    yc = y - mu
    var = jnp.mean(yc * yc, axis=-1, keepdims=True)
    return yc * lax.rsqrt(var + LN_EPS) * g + b


def _first_argmax(v, lane, big):
    m = jnp.max(v, axis=-1, keepdims=True)
    idx = jnp.min(jnp.where(v == m, lane, big), axis=-1, keepdims=True)
    return m, idx


def _ln_kernel(*refs, alpha, route, n_groups, per_group):
    if route:
        x_ref, mix_ref, g_ref, b_ref, wr_ref, br_ref, out_ref, outbf_ref, eid_ref, gate_ref = refs
    else:
        x_ref, mix_ref, g_ref, b_ref, out_ref, outbf_ref = refs
    y = alpha * x_ref[...] + mix_ref[...]
    out = _ln_rows(y, g_ref[...], b_ref[...])
    out_ref[...] = out
    outbf_ref[...] = out.astype(BF16)
    if route:
        lg = _dot_hi(out, wr_ref[...]) + br_ref[...]
        lane = lax.broadcasted_iota(jnp.int32, lg.shape, 1)
        neg = jnp.float32(-jnp.inf)
        big = jnp.int32(LANES)
        gl = jnp.where(lane < n_groups, lg, neg)
        gmax, grp = _first_argmax(gl, lane, big)
        pg_top = 1.0 / jnp.sum(jnp.exp(gl - gmax), axis=-1, keepdims=True)
        lo = n_groups + grp * per_group
        el = jnp.where((lane >= lo) & (lane < lo + per_group), lg, neg)
        m1, i1 = _first_argmax(el, lane, big)
        el2 = jnp.where(lane == i1, neg, el)
        m2, i2 = _first_argmax(el2, lane, big)
        e2 = jnp.exp(m2 - m1)
        g1 = pg_top / (1.0 + e2)
        g2 = pg_top * e2 / (1.0 + e2)
        eid_ref[...] = jnp.where(lane == 0, i1 - n_groups, jnp.where(lane == 1, i2 - n_groups, 0))
        gate_ref[...] = jnp.where(lane == 0, g1, jnp.where(lane == 1, g2, 0.0))


def _ln_residual(x, mix, ln_g, ln_b, layer, which, alpha, router=None, tm=LN_TM):
    m, d = x.shape
    assert m % tm == 0
    row = pl.BlockSpec((tm, d), lambda i: (i, 0))
    par = pl.BlockSpec((None, None, 1, d), lambda i: (layer, which, 0, 0))
    g4 = ln_g.reshape(ln_g.shape[0], ln_g.shape[1], 1, d)
    b4 = ln_b.reshape(ln_b.shape[0], ln_b.shape[1], 1, d)
    in_specs = [row, row, par, par]
    args = [x, mix, g4, b4]
    out_specs = [row, row]
    out_shape = [jax.ShapeDtypeStruct((m, d), F32), jax.ShapeDtypeStruct((m, d), BF16)]
    route = router is not None
    n_groups = per_group = 0
    if route:
        wr, br, n_groups, per_group = router
        in_specs += [pl.BlockSpec((d, LANES), lambda i: (0, 0)), pl.BlockSpec((1, LANES), lambda i: (0, 0))]
        args += [wr, br]
        small = pl.BlockSpec((tm, LANES), lambda i: (i, 0))
        out_specs += [small, small]
        out_shape += [jax.ShapeDtypeStruct((m, LANES), jnp.int32), jax.ShapeDtypeStruct((m, LANES), F32)]
    kern = functools.partial(_ln_kernel, alpha=alpha, route=route, n_groups=n_groups, per_group=per_group)
    return pl.pallas_call(
        kern, grid=(m // tm,), in_specs=in_specs, out_specs=out_specs, out_shape=out_shape,
        compiler_params=_cparams(("parallel",)), name="ln_router" if route else "ln",
    )(*args)


def _pair_row_copy(y_hbm, buf_ref, sem_ref, slot, k, src_row, dst_row, n_rows):
    return pltpu.make_async_copy(y_hbm.at[pl.ds(src_row, n_rows), :],
                                 buf_ref.at[slot, k, pl.ds(dst_row, n_rows), :], sem_ref.at[slot])


def _ln_combine_kernel(slot_ref, x_ref, y_hbm, g_ref, b_ref, out_ref, outbf_ref, buf_ref, sem_ref, *, alpha):
    i = pl.program_id(0)
    tm = x_ref.shape[0]
    cur = i % 2

    def issue(tile, dst):
        base = tile * (tm * TOP_K)

        def body(r, carry):
            for k in range(TOP_K):
                _pair_row_copy(y_hbm, buf_ref, sem_ref, dst, k, slot_ref[base + TOP_K * r + k], r, 1).start()
            return carry

        lax.fori_loop(0, tm, body, 0, unroll=4)

    @pl.when(i == 0)
    def _():
        issue(0, 0)

    @pl.when(i + 1 < pl.num_programs(0))
    def _():
        issue(i + 1, 1 - cur)

    for k in range(TOP_K):
        _pair_row_copy(y_hbm, buf_ref, sem_ref, cur, k, 0, 0, tm).wait()
    y = alpha * x_ref[...] + (buf_ref[cur, 0] + buf_ref[cur, 1])
    out = _ln_rows(y, g_ref[...], b_ref[...])
    out_ref[...] = out
    outbf_ref[...] = out.astype(BF16)


def _ln_combine(x, yb, slot_flat, ln_g, ln_b, layer, which, alpha, tm=LN_TM):
    m, d = x.shape
    assert m % tm == 0
    row = pl.BlockSpec((tm, d), lambda i, s: (i, 0))
    par = pl.BlockSpec((None, None, 1, d), lambda i, s: (layer, which, 0, 0))
    g4 = ln_g.reshape(ln_g.shape[0], ln_g.shape[1], 1, d)
    b4 = ln_b.reshape(ln_b.shape[0], ln_b.shape[1], 1, d)
    return pl.pallas_call(
        functools.partial(_ln_combine_kernel, alpha=alpha),
        grid_spec=pltpu.PrefetchScalarGridSpec(
            num_scalar_prefetch=1, grid=(m // tm,),
            in_specs=[row, pl.BlockSpec(memory_space=pl.ANY), par, par],
            out_specs=[row, row],
            scratch_shapes=[pltpu.VMEM((2, TOP_K, tm, d), F32), pltpu.SemaphoreType.DMA((2,))]),
        out_shape=[jax.ShapeDtypeStruct((m, d), F32), jax.ShapeDtypeStruct((m, d), BF16)],
        compiler_params=pltpu.CompilerParams(dimension_semantics=("arbitrary",),
                                             vmem_limit_bytes=V7X_VMEM_LIMIT_BYTES,
                                             disable_bounds_checks=True),
        name="ln_combine",
    )(slot_flat, x, yb, g4, b4)


def _moe_row_copy(x_hbm, xbuf_ref, sem_ref, slot, src_row, dst_row, n_rows):
    return pltpu.make_async_copy(x_hbm.at[pl.ds(src_row, n_rows), :],
                                 xbuf_ref.at[slot, pl.ds(dst_row, n_rows), :], sem_ref.at[slot])


def _moe_up_kernel(be_ref, nu_ref, tok_ref, x_hbm, wg_ref, wu_ref, h_ref, wgbf_ref, wubf_ref, xbuf_ref, sem_ref):
    i = pl.program_id(0)
    n_used = nu_ref[0]
    bm = xbuf_ref.shape[1]
    slot = i % 2

    def issue(blk, dst_slot):
        base = blk * bm

        def body(r, carry):
            _moe_row_copy(x_hbm, xbuf_ref, sem_ref, dst_slot, tok_ref[base + r], r, 1).start()
            return carry

        lax.fori_loop(0, bm, body, 0, unroll=8)

    @pl.when(i == 0)
    def _():
        issue(0, 0)

    @pl.when(i + 1 < n_used)
    def _():
        issue(i + 1, 1 - slot)

    prev = be_ref[jnp.maximum(i - 1, 0)]

    @pl.when((i == 0) | (be_ref[i] != prev))
    def _():
        wgbf_ref[...] = wg_ref[...].astype(BF16)
        wubf_ref[...] = wu_ref[...].astype(BF16)

    @pl.when(i < n_used)
    def _():
        _moe_row_copy(x_hbm, xbuf_ref, sem_ref, slot, 0, 0, bm).wait()
        x = xbuf_ref[slot].astype(BF16)
        h = _silu(_dot(x, wgbf_ref[...])) * _dot(x, wubf_ref[...])
        h_ref[...] = h.astype(BF16)

    @pl.when(i >= n_used)
    def _():
        h_ref[...] = jnp.zeros(h_ref.shape, BF16)


def _moe_down_kernel(be_ref, nu_ref, h_ref, wd_ref, gate_ref, out_ref, wdbf_ref):
    i = pl.program_id(1)
    prev = be_ref[jnp.maximum(i - 1, 0)]
    fresh = (i == 0) | (be_ref[i] != prev)

    @pl.when(fresh)
    def _():
        wdbf_ref[...] = wd_ref[...].astype(BF16)

    @pl.when(i < nu_ref[0])
    def _():
        out_ref[...] = _dot(h_ref[...], wdbf_ref[...]) * gate_ref[...]

    @pl.when(i >= nu_ref[0])
    def _():
        out_ref[...] = jnp.zeros(out_ref.shape, F32)


def _moe_experts(x, tok_buf, blk_expert, n_used, gate_buf, w_gate, w_up, w_down, layer,
                 bm=MOE_BM, tn=MOE_TN):
    cap = tok_buf.shape[0]
    d = x.shape[1]
    f = w_gate.shape[3]
    n_blocks = cap // bm
    tn = min(tn, d)
    hidden = pl.pallas_call(
        _moe_up_kernel,
        grid_spec=pltpu.PrefetchScalarGridSpec(
            num_scalar_prefetch=3,
            grid=(n_blocks,),
            in_specs=[
                pl.BlockSpec(memory_space=pl.ANY),
                pl.BlockSpec((None, None, d, f), lambda i, be, nu, tok: (layer, be[i], 0, 0)),
                pl.BlockSpec((None, None, d, f), lambda i, be, nu, tok: (layer, be[i], 0, 0)),
            ],
            out_specs=pl.BlockSpec((bm, f), lambda i, be, nu, tok: (i, 0)),
            scratch_shapes=[pltpu.VMEM((d, f), BF16), pltpu.VMEM((d, f), BF16),
                            pltpu.VMEM((2, bm, d), F32), pltpu.SemaphoreType.DMA((2,))],
        ),
        out_shape=jax.ShapeDtypeStruct((cap, f), BF16),
        compiler_params=pltpu.CompilerParams(dimension_semantics=("arbitrary",),
                                             vmem_limit_bytes=V7X_VMEM_LIMIT_BYTES,
                                             disable_bounds_checks=True),
        name="moe_up",
    )(blk_expert, n_used, tok_buf, x, w_gate, w_up)
    return pl.pallas_call(
        _moe_down_kernel,
        grid_spec=pltpu.PrefetchScalarGridSpec(
            num_scalar_prefetch=2,
            grid=(d // tn, n_blocks),
            in_specs=[
                pl.BlockSpec((bm, f), lambda j, i, be, nu: (jnp.minimum(i, nu[0] - 1), 0)),
                pl.BlockSpec((None, None, f, tn), lambda j, i, be, nu: (layer, be[i], 0, j)),
                pl.BlockSpec((bm, 1), lambda j, i, be, nu: (i, 0)),
            ],
            out_specs=pl.BlockSpec((bm, tn), lambda j, i, be, nu: (i, j)),
            scratch_shapes=[pltpu.VMEM((f, tn), BF16)],
        ),
        out_shape=jax.ShapeDtypeStruct((cap, d), F32),
        compiler_params=_cparams(("arbitrary", "arbitrary")),
        name="moe_down",
    )(blk_expert, n_used, hidden, w_down, gate_buf)


def _moe_dispatch(eid, gates, n_experts, bm):
    t = eid.shape[0]
    n_assign = t * TOP_K
    flat_e = eid.reshape(-1)
    onehot = (flat_e[:, None] == jnp.arange(n_experts, dtype=jnp.int32)[None, :]).astype(jnp.int32)
    csum = jnp.cumsum(onehot, axis=0)
    rank = jnp.sum(csum * onehot, axis=1) - 1
    counts = csum[-1]
    padded = (counts + bm - 1) // bm * bm
    pad_end = jnp.cumsum(padded)
    pad_start = pad_end - padded
    slot = (pad_start[flat_e] + rank).astype(jnp.int32)
    n_blocks = -(-n_assign // bm) + n_experts
    cap = n_blocks * bm
    tok_buf = jnp.zeros((cap,), jnp.int32).at[slot].set(jnp.arange(n_assign, dtype=jnp.int32) // TOP_K)
    gate_buf = jnp.zeros((cap,), F32).at[slot].set(gates.reshape(-1))
    blk_expert = jnp.minimum(
        jnp.searchsorted(pad_end, jnp.arange(n_blocks, dtype=jnp.int32) * bm, side="right"),
        n_experts - 1).astype(jnp.int32)
    n_used = (pad_end[-1] // bm).astype(jnp.int32).reshape(1)
    return slot.reshape(t, TOP_K), tok_buf, gate_buf.reshape(cap, 1), blk_expert, n_used


def _hier_moe(x, eid, gates, w_gate, w_up, w_down, layer):
    n_experts = w_gate.shape[1]
    slot, tok_buf, gate_buf, blk_expert, n_used = _moe_dispatch(eid, gates, n_experts, MOE_BM)
    yb = _moe_experts(x, tok_buf, blk_expert, n_used, gate_buf, w_gate, w_up, w_down, layer)
    return yb, slot.reshape(-1)


def _s5_prepare(lam_re, lam_im, log_dt, b_re, b_im, c_re, c_im, d_skip, lc):
    g, p, c = b_re.shape
    dt = jnp.exp(log_dt.astype(F32))[:, None]
    lr, li = lam_re.astype(F32), lam_im.astype(F32)
    mag = jnp.exp(lr * dt)
    ab_re, ab_im = mag * jnp.cos(li * dt), mag * jnp.sin(li * dt)
    den = lr * lr + li * li
    q_re = ((ab_re - 1.0) * lr + ab_im * li) / den
    q_im = (ab_im * lr - (ab_re - 1.0) * li) / den
    bb_re = q_re[..., None] * b_re - q_im[..., None] * b_im
    bb_im = q_re[..., None] * b_im + q_im[..., None] * b_re
    ks = jnp.arange(lc + 1, dtype=F32)[:, None, None]
    pmag = jnp.exp(lr * dt * ks)
    pw_re, pw_im = pmag * jnp.cos(li * dt * ks), pmag * jnp.sin(li * dt * ks)
    t_re = pw_re[:lc, :, :, None] * bb_re[None] - pw_im[:lc, :, :, None] * bb_im[None]
    t_im = pw_re[:lc, :, :, None] * bb_im[None] + pw_im[:lc, :, :, None] * bb_re[None]
    kt = (jnp.einsum("gop,kgpi->gkoi", c_re, t_re, precision=HIGHEST)
          - jnp.einsum("gop,kgpi->gkoi", c_im, t_im, precision=HIGHEST))
    s_idx = jnp.arange(lc)[:, None]
    t_idx = jnp.arange(lc)[None, :]
    tau = t_idx - s_idx
    toe = kt[:, jnp.clip(tau, 0, lc - 1)]
    toe = jnp.where((tau >= 0)[None, :, :, None, None], toe, 0.0)
    m_intra = jnp.transpose(toe, (0, 1, 4, 2, 3)).reshape(g, lc * c, lc * c)
    rev = pw_re[:lc][::-1], pw_im[:lc][::-1]
    wst_re = rev[0][:, :, :, None] * bb_re[None] - rev[1][:, :, :, None] * bb_im[None]
    wst_im = rev[0][:, :, :, None] * bb_im[None] + rev[1][:, :, :, None] * bb_re[None]
    wst_re = jnp.transpose(wst_re, (1, 0, 3, 2)).reshape(g, lc * c, p)
    wst_im = jnp.transpose(wst_im, (1, 0, 3, 2)).reshape(g, lc * c, p)
    half = (jnp.arange(g) % 2)[:, None] == jnp.arange(2)[None, :]

    def place_cols(wm):
        return jnp.where(half[:, None, :, None], wm[:, :, None, :], 0.0).reshape(g, lc * c, 2 * p)

    wy_re = c_re[:, None] * pw_re[1:].transpose(1, 0, 2)[:, :, None, :] \
        - c_im[:, None] * pw_im[1:].transpose(1, 0, 2)[:, :, None, :]
    wy_im = -(c_re[:, None] * pw_im[1:].transpose(1, 0, 2)[:, :, None, :]
              + c_im[:, None] * pw_re[1:].transpose(1, 0, 2)[:, :, None, :])
    wy_re = jnp.transpose(wy_re, (0, 3, 1, 2)).reshape(g, p, lc * c)
    wy_im = jnp.transpose(wy_im, (0, 3, 1, 2)).reshape(g, p, lc * c)

    def place_rows(wm):
        return jnp.where(half[:, :, None, None], wm[:, None, :, :], 0.0).reshape(g, 2 * p, lc * c)

    al_re = pw_re[lc].reshape(g // 2, 1, 2 * p)
    al_im = pw_im[lc].reshape(g // 2, 1, 2 * p)
    d_ch = jnp.tile(d_skip.astype(F32).reshape(g, 1, c), (1, lc, 1)).reshape(g, 1, lc * c)
    return dict(m=m_intra.astype(BF16), wst_re=place_cols(wst_re).astype(BF16),
                wst_im=place_cols(wst_im).astype(BF16), wy_re=place_rows(wy_re).astype(BF16),
                wy_im=place_rows(wy_im).astype(BF16), al_re=al_re, al_im=al_im, d=d_ch)


def _s5_kernel(uc_ref, m_ref, wsr_ref, wsi_ref, wyr_ref, wyi_ref, alr_ref, ali_ref, d_ref, s0r_ref, s0i_ref,
               h_ref, sfr_ref, sfi_ref, locr_ref, loci_ref, str_ref, sti_ref, *, gb, nb, nk):
    for pair in range(gb // 2):
        g0, g1 = 2 * pair, 2 * pair + 1
        u0 = uc_ref[g0]
        u1 = uc_ref[g1]
        ub0 = u0.astype(BF16)
        ub1 = u1.astype(BF16)
        loc_re = _dot(ub0, wsr_ref[g0]) + _dot(ub1, wsr_ref[g1])
        loc_im = _dot(ub0, wsi_ref[g0]) + _dot(ub1, wsi_ref[g1])
        a_re = alr_ref[pair]
        a_im = ali_ref[pair]
        s_re = s0r_ref[pair]
        s_im = s0i_ref[pair]
        if nk == 1:
            st_re, st_im = s_re, s_im
            f_re = a_re * s_re - a_im * s_im + loc_re
            f_im = a_re * s_im + a_im * s_re + loc_im
        else:
            locr_ref[...] = loc_re
            loci_ref[...] = loc_im

            def step(k, carry):
                c_re, c_im = carry
                rows = pl.ds(k, nb, stride=nk)
                str_ref[rows, :] = c_re
                sti_ref[rows, :] = c_im
                l_re = locr_ref[rows, :]
                l_im = loci_ref[rows, :]
                return (a_re * c_re - a_im * c_im + l_re, a_re * c_im + a_im * c_re + l_im)

            f_re, f_im = lax.fori_loop(0, nk, step, (s_re, s_im))
            st_re = str_ref[...]
            st_im = sti_ref[...]
        sfr_ref[pair] = f_re
        sfi_ref[pair] = f_im
        sb_re = st_re.astype(BF16)
        sb_im = st_im.astype(BF16)
        for gi, u, ub in ((g0, u0, ub0), (g1, u1, ub1)):
            y = (_dot(ub, m_ref[gi]) + _dot(sb_re, wyr_ref[gi]) + _dot(sb_im, wyi_ref[gi])
                 + d_ref[gi] * u)
            h_ref[gi] = jax.nn.gelu(y)


def _s5_mixer(uc, prep, s0_re, s0_im, nb, nk, gb=S5_GB):
    g, r, w = uc.shape
    p2 = prep["al_re"].shape[2]
    grp = lambda *shape: pl.BlockSpec((gb,) + shape, lambda i: (i, 0, 0))
    pr = lambda *shape: pl.BlockSpec((gb // 2,) + shape, lambda i: (i, 0, 0))
    kern = functools.partial(_s5_kernel, gb=gb, nb=nb, nk=nk)
    return pl.pallas_call(
        kern,
        grid=(g // gb,),
        in_specs=[grp(r, w), grp(w, w), grp(w, p2), grp(w, p2), grp(p2, w), grp(p2, w),
                  pr(1, p2), pr(1, p2), grp(1, w), pr(nb, p2), pr(nb, p2)],
        out_specs=[grp(r, w), pr(nb, p2), pr(nb, p2)],
        out_shape=[jax.ShapeDtypeStruct((g, r, w), F32),
                   jax.ShapeDtypeStruct((g // 2, nb, p2), F32),
                   jax.ShapeDtypeStruct((g // 2, nb, p2), F32)],
        scratch_shapes=[pltpu.VMEM((r, p2), F32) for _ in range(4)],
        compiler_params=_cparams(("parallel",)),
        name="s5_mixer",
    )(uc, prep["m"], prep["wst_re"], prep["wst_im"], prep["wy_re"], prep["wy_im"],
      prep["al_re"], prep["al_im"], prep["d"], s0_re, s0_im)


def _to_chunks(u2d, nb, nk, lc, g):
    c = u2d.shape[1] // g
    u5 = u2d.reshape(nb, nk, lc, g, c)
    return jnp.transpose(u5, (3, 0, 1, 2, 4)).reshape(g, nb * nk, lc * c)


def _from_chunks(hc, nb, nk, lc, g):
    c = hc.shape[2] // lc
    h5 = hc.reshape(g, nb, nk, lc, c)
    return jnp.transpose(h5, (1, 2, 3, 0, 4)).reshape(nb * nk * lc, g * c)


def _state_to_pairs(s):
    b, g, p = s.shape
    return jnp.transpose(s.reshape(b, g // 2, 2 * p), (1, 0, 2))


def _pairs_to_state(f):
    g2, b, p2 = f.shape
    return jnp.transpose(f, (1, 0, 2)).reshape(b, g2 * 2, p2 // 2)


def _softplus(x):
    return jnp.maximum(x, 0.0) + jnp.log1p(jnp.exp(-jnp.abs(x)))


def _ssd_kernel(*refs, prompt, q, n_heads, hd, has_prev):
    it = iter(refs)
    z_ref, xs_ref, bm_ref, cm_ref, dt_ref = (next(it) for _ in range(5))
    if prompt:
        hx_ref, hb_ref, hc_ref = (next(it) for _ in range(3))
    cwx_ref, cwb_ref, cwc_ref, cbx_ref, cbb_ref, cbc_ref = (next(it) for _ in range(6))
    dtb_ref, a_ref, dsk_ref, nw_ref, sel_ref, e_ref, et_ref = (next(it) for _ in range(7))
    if not prompt:
        sin_ref = next(it)
    if has_prev:
        next(it)
    y_ref, sout_ref = next(it), next(it)
    padx_ref, padb_ref, padc_ref, yacc_ref = (next(it) for _ in range(4))

    r = z_ref.shape[0]
    nseq = r // q
    hpg = xs_ref.shape[1] // hd
    row = lax.broadcasted_iota(jnp.int32, (r, 1), 0)
    if prompt:
        first = pl.program_id(2) == 0
        keep = jnp.where(first, 0.0, 1.0)
        live = None
    else:
        live = (row % q) >= (q // 2)

    def conv(x_ref, halo_ref, pad_ref, w_ref, b_ref):
        if prompt:
            pad_ref[0:SUBLANES, :] = halo_ref[...] * keep
        else:
            pad_ref[0:SUBLANES, :] = jnp.zeros((SUBLANES, pad_ref.shape[1]), F32)
        pad_ref[SUBLANES:SUBLANES + r, :] = x_ref[...]
        kk = w_ref.shape[0]
        acc = b_ref[...]
        for k in range(kk):
            acc = acc + w_ref[k:k + 1, :] * pad_ref[pl.ds(SUBLANES - (kk - 1) + k, r), :]
        return _silu(acc)

    xs = conv(xs_ref, hx_ref if prompt else None, padx_ref, cwx_ref, cbx_ref)
    bc = conv(bm_ref, hb_ref if prompt else None, padb_ref, cwb_ref, cbb_ref)
    cc = conv(cm_ref, hc_ref if prompt else None, padc_ref, cwc_ref, cbc_ref)

    lane = lax.broadcasted_iota(jnp.int32, (r, LANES), 1)
    dtv = jnp.where(lane < n_heads, _softplus(dt_ref[...] + dtb_ref[...]), 0.0)
    if not prompt:
        dtv = jnp.where(live, dtv, 0.0)
    sel = sel_ref[...]
    dt8 = _dot_hi(dtv, sel)
    adt8 = _dot_hi(dtv * a_ref[...], sel)

    ri = lax.broadcasted_iota(jnp.int32, (r, r), 0)
    ci = lax.broadcasted_iota(jnp.int32, (r, r), 1)
    same = (ri // q) == (ci // q)
    tri = same & (ci <= ri)
    acum = _dot_hi(tri.astype(F32), adt8)
    atot = _dot_hi(same.astype(F32), adt8)
    acum_t = acum.T
    expand = _dot_hi(jnp.concatenate([jnp.exp(acum), dt8, jnp.exp(atot - acum)], axis=0), e_ref[...])
    eac_x, dt_x, dte_x = expand[0:r], expand[r:2 * r], expand[2 * r:3 * r]
    dec = _dot_hi(et_ref[...], jnp.exp(atot).T)

    cb = _dot_nt(cc.astype(BF16), bc.astype(BF16))
    xdt = xs * dt_x
    neg = jnp.float32(-jnp.inf)
    lane_hd = lax.broadcasted_iota(jnp.int32, (r, 2 * hd), 1)
    for pair in range(hpg // 2):
        xp = xdt[:, pair * 2 * hd:(pair + 1) * 2 * hd]
        acc = None
        for half in range(2):
            hl = 2 * pair + half
            seg = acum[:, hl:hl + 1] - acum_t[hl:hl + 1, :]
            scores = cb * jnp.exp(jnp.where(tri, seg, neg))
            mask = (lane_hd < hd) if half == 0 else (lane_hd >= hd)
            part = _dot(scores.astype(BF16), jnp.where(mask, xp, 0.0).astype(BF16))
            acc = part if acc is None else acc + part
        yacc_ref[:, pair * 2 * hd:(pair + 1) * 2 * hd] = acc

    wgt = (xdt * dte_x).astype(BF16)
    bcb = bc.astype(BF16)
    ccb = cc.astype(BF16)
    if prompt:
        @pl.when(first)
        def _():
            sout_ref[...] = jnp.zeros(sout_ref.shape, F32)

        s_old = sout_ref[...]
        yacc_ref[...] += _dot_nt(ccb, s_old.astype(BF16)) * eac_x
        sout_ref[...] = dec[:, 0:1] * s_old + _dot_tn(wgt, bcb)
    else:
        for j in range(nseq):
            rows = slice(j * q, (j + 1) * q)
            s_old = sin_ref[j]
            yacc_ref[rows, :] += _dot_nt(ccb[rows], s_old.astype(BF16)) * eac_x[rows]
            wj = jnp.where((row // q) == j, wgt, jnp.zeros_like(wgt))
            sout_ref[j] = dec[:, j * q:j * q + 1] * s_old + _dot_tn(wj, bcb)

    y = yacc_ref[...] + dsk_ref[...] * xs
    y = y * _silu(z_ref[...])
    y = y * lax.rsqrt(jnp.mean(y * y, axis=-1, keepdims=True) + LN_EPS)
    y_ref[...] = (y * nw_ref[...]).astype(y_ref.dtype)


def _ssd_constants(n_heads, hd, n_groups):
    hpg = n_heads // n_groups
    lane = jnp.arange(LANES)
    sel = (lane[None, :, None] == (jnp.arange(n_groups)[:, None, None] * hpg + lane[None, None, :])) \
        & (lane[None, None, :] < hpg)
    e = (lane[:, None] == (jnp.arange(hpg * hd) // hd)[None, :])
    return sel.astype(F32), e.astype(F32), e.T.astype(F32)


def _pad_lanes(v):
    return jnp.pad(v, ((0, 0), (0, LANES - v.shape[1])))[:, None, :]


def _ssd_params(conv_w, conv_b, dt_bias, a_log, d_skip, norm_w):
    n_heads = dt_bias.shape[1]
    hd = norm_w.shape[1] // n_heads
    return dict(
        ssd_conv_w=conv_w, ssd_conv_b3=conv_b[:, None, :],
        ssd_dt_bias3=_pad_lanes(dt_bias.astype(F32)),
        ssd_a3=_pad_lanes(-jnp.exp(a_log.astype(F32))),
        ssd_d3=jnp.repeat(d_skip.astype(F32), hd, axis=1)[:, None, :],
        ssd_norm3=norm_w[:, None, :],
        ssd_consts=_ssd_constants(n_heads, hd, SSD_GROUPS))


def _ssd_mixer(u, layer, prm, cols, *, prompt, nb, t, state_in=None, out_rows=None, state_prev=None):
    aliases = {}
    o_z, o_x, o_dt, w_b, n_st = cols
    n_groups = SSD_GROUPS
    hd = SSD_HEAD_DIM
    n_heads = w_b // hd
    gw = w_b // n_groups
    r = SSD_CHUNK
    sel, e, et = prm["ssd_consts"]
    o_b = o_x + w_b
    o_c = o_b + n_groups * n_st
    assert o_z % gw == 0 and o_x % gw == 0 and o_b % n_st == 0 and o_dt % LANES == 0 and n_st == LANES
    cw, cb = prm["ssd_conv_w"], prm["ssd_conv_b3"]

    if prompt:
        nc = t // r
        grid = (nb, n_groups, nc)
        rowblk = lambda b, g, c: b * nc + c
        halo = lambda b, g, c: jnp.maximum((b * t + c * r) // SUBLANES - 1, 0)
        im = lambda colf: (lambda b, g, c: (rowblk(b, g, c), colf(g)))
        hm = lambda colf: (lambda b, g, c: (halo(b, g, c), colf(g)))
        pm = lambda f: (lambda b, g, c: f(g))
        q = r
    else:
        q = SAMPLE_SEQ_ROWS
        grid = (nb * q // r, n_groups)
        im = lambda colf: (lambda i, g: (i, colf(g)))
        pm = lambda f: (lambda i, g: f(g))
    col_z = lambda g: o_z // gw + g
    col_x = lambda g: o_x // gw + g
    col_b = lambda g: o_b // n_st + g
    col_c = lambda g: o_c // n_st + g
    col_dt = lambda g: o_dt // LANES

    in_specs = [pl.BlockSpec((r, gw), im(col_z)), pl.BlockSpec((r, gw), im(col_x)),
                pl.BlockSpec((r, n_st), im(col_b)), pl.BlockSpec((r, n_st), im(col_c)),
                pl.BlockSpec((r, LANES), im(col_dt))]
    args = [u, u, u, u, u]
    if prompt:
        in_specs += [pl.BlockSpec((SUBLANES, gw), hm(col_x)), pl.BlockSpec((SUBLANES, n_st), hm(col_b)),
                     pl.BlockSpec((SUBLANES, n_st), hm(col_c))]
        args += [u, u, u]
    kk = cw.shape[1]
    in_specs += [
        pl.BlockSpec((None, kk, gw), pm(lambda g: (layer, 0, g))),
        pl.BlockSpec((None, kk, n_st), pm(lambda g: (layer, 0, w_b // n_st + g))),
        pl.BlockSpec((None, kk, n_st), pm(lambda g: (layer, 0, w_b // n_st + n_groups + g))),
        pl.BlockSpec((None, 1, gw), pm(lambda g: (layer, 0, g))),
        pl.BlockSpec((None, 1, n_st), pm(lambda g: (layer, 0, w_b // n_st + g))),
        pl.BlockSpec((None, 1, n_st), pm(lambda g: (layer, 0, w_b // n_st + n_groups + g))),
        pl.BlockSpec((None, 1, LANES), pm(lambda g: (layer, 0, 0))),
        pl.BlockSpec((None, 1, LANES), pm(lambda g: (layer, 0, 0))),
        pl.BlockSpec((None, 1, gw), pm(lambda g: (layer, 0, g))),
        pl.BlockSpec((None, 1, gw), pm(lambda g: (layer, 0, g))),
        pl.BlockSpec((None, LANES, LANES), pm(lambda g: (g, 0, 0))),
        pl.BlockSpec((LANES, gw), pm(lambda g: (0, 0))),
        pl.BlockSpec((gw, LANES), pm(lambda g: (0, 0))),
    ]
    args += [cw, cw, cw, cb, cb, cb, prm["ssd_dt_bias3"], prm["ssd_a3"], prm["ssd_d3"], prm["ssd_norm3"],
             sel, e, et]
    if prompt:
        y_rows = out_rows
        out_specs = [pl.BlockSpec((r, gw), lambda b, g, c: (b * nc + c, g)),
                     pl.BlockSpec((None, gw, n_st), lambda b, g, c: (b, g, 0))]
        s_shape = (nb, w_b, n_st)
    else:
        nseq = r // q
        s_off = layer * (nb // nseq)
        in_specs.append(pl.BlockSpec((nseq, gw, n_st), lambda i, g: (i + s_off, g, 0)))
        args.append(state_in)
        y_rows = nb * q
        out_specs = [pl.BlockSpec((r, gw), lambda i, g: (i, g)),
                     pl.BlockSpec((nseq, gw, n_st), lambda i, g: (i + s_off, g, 0))]
        s_shape = state_in.shape
        if state_prev is not None:
            in_specs.append(pl.BlockSpec(memory_space=pl.ANY))
            args.append(state_prev)
            aliases = {len(args) - 1: 1}
    kern = functools.partial(_ssd_kernel, prompt=prompt, q=q, n_heads=n_heads, hd=hd,
                             has_prev=state_prev is not None)
    return pl.pallas_call(
        kern, grid=grid, in_specs=in_specs, out_specs=out_specs, input_output_aliases=aliases,
        out_shape=[jax.ShapeDtypeStruct((y_rows, w_b), BF16), jax.ShapeDtypeStruct(s_shape, F32)],
        scratch_shapes=[pltpu.VMEM((r + SUBLANES, gw), F32), pltpu.VMEM((r + SUBLANES, n_st), F32),
                        pltpu.VMEM((r + SUBLANES, n_st), F32), pltpu.VMEM((r, gw), F32)],
        compiler_params=_cparams(("parallel", "parallel", "arbitrary") if prompt else ("parallel", "parallel")),
        name="ssd_prompt" if prompt else "ssd_sample",
    )(*args)


def _conf_prompt_kernel(g_ref, halo_ref, w_ref, b_ref, lg_ref, lb_ref, out_ref, pad_ref, conv_ref, *, rt):
    tt, d = g_ref.shape
    halo = halo_ref.shape[0]
    kk = w_ref.shape[1]
    nlt = d // LANES
    keep = jnp.where(pl.program_id(1) == 0, 0.0, 1.0)
    for lt in range(nlt):
        cols = slice(lt * LANES, (lt + 1) * LANES)
        pad_ref[lt, 0:halo, :] = halo_ref[:, cols] * keep
        pad_ref[lt, halo:halo + tt, :] = g_ref[:, cols]
    base = halo - (kk - 1)

    def lane_tile(lt, carry):
        wt = w_ref[lt]
        bt = b_ref[lt]
        for r0 in range(0, tt, rt):
            acc = jnp.broadcast_to(bt, (rt, LANES))
            for k in range(kk):
                acc = acc + wt[k:k + 1, :] * pad_ref[lt, base + r0 + k:base + r0 + k + rt, :]
            conv_ref[lt, r0:r0 + rt, :] = acc
        return carry

    lax.fori_loop(0, nlt, lane_tile, 0)

    s1 = conv_ref[0]
    for lt in range(1, nlt):
        s1 = s1 + conv_ref[lt]
    mu = jnp.sum(s1, axis=-1, keepdims=True) * (1.0 / d)
    s2 = None
    for lt in range(nlt):
        dv = conv_ref[lt] - mu
        s2 = dv * dv if s2 is None else s2 + dv * dv
    rstd = lax.rsqrt(jnp.sum(s2, axis=-1, keepdims=True) * (1.0 / d) + LN_EPS)
    for lt in range(nlt):
        cols = slice(lt * LANES, (lt + 1) * LANES)
        v = (conv_ref[lt] - mu) * rstd * lg_ref[:, cols] + lb_ref[:, cols]
        out_ref[:, cols] = _silu(v).astype(out_ref.dtype)


def _conf_prompt(g, layer, dw_w, dw_b3, ln_g3, ln_b3, nb, t, out_rows, tt=CONF_TT, rt=CONF_RT, halo=CONF_HALO):
    d = g.shape[1]
    n_layers, kk, _ = dw_w.shape
    assert t % tt == 0 and tt % rt == 0 and halo >= kk - 1 and tt % halo == 0
    nt = t // tt
    nlt = d // LANES
    w_tiles = jnp.transpose(dw_w.reshape(n_layers, kk, nlt, LANES), (0, 2, 1, 3))
    b_tiles = dw_b3.reshape(n_layers, nlt, 1, LANES)
    par = pl.BlockSpec((None, 1, d), lambda b, i: (layer, 0, 0))
    return pl.pallas_call(
        functools.partial(_conf_prompt_kernel, rt=rt),
        grid=(nb, nt),
        in_specs=[pl.BlockSpec((tt, d), lambda b, i: (b * nt + i, 0)),
                  pl.BlockSpec((halo, d), lambda b, i: (jnp.maximum((b * t + i * tt) // halo - 1, 0), 0)),
                  pl.BlockSpec((None, nlt, kk, LANES), lambda b, i: (layer, 0, 0, 0)),
                  pl.BlockSpec((None, nlt, 1, LANES), lambda b, i: (layer, 0, 0, 0)), par, par],
        out_specs=pl.BlockSpec((tt, d), lambda b, i: (b * nt + i, 0)),
        out_shape=jax.ShapeDtypeStruct((out_rows, d), BF16),
        scratch_shapes=[pltpu.VMEM((d // LANES, halo + tt, LANES), F32), pltpu.VMEM((d // LANES, tt, LANES), F32)],
        compiler_params=_cparams(("parallel", "parallel")),
        name="conf_prompt",
    )(g, g, w_tiles, b_tiles, ln_g3, ln_b3)


def _conf_sample_kernel(ext_ref, w_ref, b_ref, lg_ref, lb_ref, out_ref):
    kk = w_ref.shape[0]
    t_new = out_ref.shape[0]
    for t in range(t_new):
        acc = b_ref[...] + w_ref[0:1, :] * ext_ref[t]
        for k in range(1, kk):
            acc = acc + w_ref[k:k + 1, :] * ext_ref[t + k]
        out_ref[t] = _silu(_ln_rows(acc, lg_ref[...], lb_ref[...])).astype(out_ref.dtype)


def _conf_sample(ext_t, layer, dw_w, dw_b3, ln_g3, ln_b3, nbt=16):
    rows, nb, d = ext_t.shape
    kk = dw_w.shape[1]
    t_new = rows - (kk - 1)
    par = pl.BlockSpec((None, 1, d), lambda i: (layer, 0, 0))
    return pl.pallas_call(
        _conf_sample_kernel,
        grid=(nb // nbt,),
        in_specs=[pl.BlockSpec((rows, nbt, d), lambda i: (0, i, 0)),
                  pl.BlockSpec((None, kk, d), lambda i: (layer, 0, 0)), par, par, par],
        out_specs=pl.BlockSpec((t_new, nbt, d), lambda i: (0, i, 0)),
        out_shape=jax.ShapeDtypeStruct((t_new, nb, d), BF16),
        compiler_params=_cparams(("parallel",)),
        name="conf_sample",
    )(ext_t, dw_w, dw_b3, ln_g3, ln_b3)


def kernel(x_prompt, x_sample, state_s5_re, state_s5_im, state_ssm, state_ssd_conv, state_conformer_conv, w_in_even, s5_lam_re, s5_lam_im, s5_log_dt, s5_b_re, s5_b_im, s5_c_re, s5_c_im, s5_d, s5_w_glu, s5_b_glu, ssd_conv_w, ssd_conv_b, ssd_dt_bias, ssd_a_log, ssd_d, ssd_norm_w, w_out_even, conf_w_pw1, conf_b_pw1, conf_dw_w, conf_dw_b, conf_ln_g, conf_ln_b, conf_w_pw2, conf_b_pw2, ln_g, ln_b, moe_w_group, moe_b_group, moe_w_expert, moe_b_expert, moe_w_gate, moe_w_up, moe_w_down):
    bsz, seq, d = x_prompt.shape
    nb_s, t_s, _ = x_sample.shape
    depth = ln_g.shape[0]
    alpha = (2.0 * depth) ** 0.25
    mp, ms = bsz * seq, nb_s * t_s
    x = jnp.concatenate([x_prompt.reshape(mp, d), x_sample.reshape(ms, d)], axis=0)
    xbf = x.astype(BF16)

    g_a = s5_lam_re.shape[1]
    w_a = s5_d.shape[1]
    w_b = ssd_norm_w.shape[1]
    n_heads = ssd_dt_bias.shape[1]
    conv_dim = ssd_conv_w.shape[2]
    n_st = (conv_dim - w_b) // (2 * SSD_GROUPS)
    o_z, o_x = w_a, w_a + w_b
    o_dt = o_x + conv_dim
    in_even = o_dt + n_heads
    cols = (o_z, o_x, o_dt, w_b, n_st)
    ssd_prm = _ssd_params(ssd_conv_w, ssd_conv_b, ssd_dt_bias, ssd_a_log, ssd_d, ssd_norm_w)
    ssm_in = state_ssm.reshape(state_ssm.shape[0] * nb_s, w_b, n_st)

    n_eg, per_g = moe_w_expert.shape[1], moe_w_expert.shape[3]
    n_route = n_eg + n_eg * per_g
    w_route = jnp.concatenate(
        [moe_w_group, jnp.transpose(moe_w_expert, (0, 2, 1, 3)).reshape(depth, d, n_eg * per_g),
         jnp.zeros((depth, d, LANES - n_route), F32)], axis=-1)
    b_route = jnp.concatenate(
        [moe_b_group, moe_b_expert.reshape(depth, n_eg * per_g), jnp.zeros((depth, LANES - n_route), F32)],
        axis=-1)[:, None, :]

    conf_dw_b3 = conf_dw_b[:, None, :]
    conf_ln_g3 = conf_ln_g[:, None, :]
    conf_ln_b3 = conf_ln_b[:, None, :]
    s5_zero = jnp.zeros((g_a // 2, bsz, 2 * s5_lam_re.shape[2]), F32)

    out = dict(re_p=[], im_p=[], ssm_p=[], sh_p=[], ch_p=[], re_s=[], im_s=[], sh_s=[], ch_s=[])
    ssm_s = None
    for layer in range(depth):
        i = layer // 2
        if layer % 2 == 0:
            u = _matmul([xbf], w_in_even, i, n_out=in_even)
            prm = (s5_lam_re[i], s5_lam_im[i], s5_log_dt[i], s5_b_re[i], s5_b_im[i], s5_c_re[i], s5_c_im[i],
                   s5_d[i])
            nk = seq // S5_CHUNK
            hc_p, fr_p, fi_p = _s5_mixer(_to_chunks(u[:mp, :w_a], bsz, nk, S5_CHUNK, g_a),
                                         _s5_prepare(*prm, S5_CHUNK), s5_zero, s5_zero, bsz, nk)
            hc_s, fr_s, fi_s = _s5_mixer(_to_chunks(u[mp:, :w_a], nb_s, 1, t_s, g_a),
                                         _s5_prepare(*prm, t_s), _state_to_pairs(state_s5_re[i]),
                                         _state_to_pairs(state_s5_im[i]), nb_s, 1)
            h = jnp.concatenate([_from_chunks(hc_p, bsz, nk, S5_CHUNK, g_a),
                                 _from_chunks(hc_s, nb_s, 1, t_s, g_a)], axis=0)
            ya = _matmul([h], s5_w_glu, i, bias=s5_b_glu, mode="gate", gate=h, out_dtype=BF16)
            out["re_p"].append(_pairs_to_state(fr_p)); out["im_p"].append(_pairs_to_state(fi_p))
            out["re_s"].append(_pairs_to_state(fr_s)); out["im_s"].append(_pairs_to_state(fi_s))
            yb, ssm_p = _ssd_mixer(u, i, ssd_prm, cols, prompt=True, nb=bsz, t=seq, out_rows=mp + ms)
            us3 = u[mp:].reshape(nb_s, t_s, in_even)
            dead = SAMPLE_SEQ_ROWS - t_s
            hist = state_ssd_conv[i]
            ext = jnp.concatenate([jnp.zeros((nb_s, dead, in_even), F32), us3], axis=1)
            ext = ext.at[:, dead - hist.shape[1]:dead, o_x:o_x + conv_dim].set(hist)
            yb_s, ssm_s = _ssd_mixer(ext.reshape(nb_s * SAMPLE_SEQ_ROWS, in_even), i, ssd_prm, cols, prompt=False,
                                     nb=nb_s, t=t_s, state_in=ssm_in, state_prev=ssm_s)
            yb_s = yb_s.reshape(nb_s, SAMPLE_SEQ_ROWS, w_b)[:, dead:].reshape(ms, w_b)
            yb = lax.dynamic_update_slice(yb, yb_s, (mp, 0))
            mix = _matmul([ya, yb], w_out_even, i)
            kh = hist.shape[1]
            out["ssm_p"].append(ssm_p.reshape(bsz, n_heads, SSD_HEAD_DIM, n_st))
            out["sh_p"].append(jnp.stack([u[(b + 1) * seq - kh:(b + 1) * seq, o_x:o_x + conv_dim]
                                          for b in range(bsz)]))
            out["sh_s"].append(jnp.concatenate([hist, us3[:, :, o_x:o_x + conv_dim]], axis=1)[:, t_s:])
        else:
            gg = _matmul([xbf], conf_w_pw1, i, bias=conf_b_pw1, mode="glu")
            c = _conf_prompt(gg, i, conf_dw_w, conf_dw_b3, conf_ln_g3, conf_ln_b3, bsz, seq, mp + ms)
            hist = state_conformer_conv[i]
            g_s = gg[mp:].reshape(nb_s, t_s, d)
            ext = jnp.concatenate([hist, g_s], axis=1)
            c_s = _conf_sample(jnp.transpose(ext, (1, 0, 2)), i, conf_dw_w, conf_dw_b3, conf_ln_g3, conf_ln_b3)
            c = lax.dynamic_update_slice(c, jnp.transpose(c_s, (1, 0, 2)).reshape(ms, d), (mp, 0))
            mix = _matmul([c], conf_w_pw2, i, bias=conf_b_pw2)
            kh = hist.shape[1]
            out["ch_p"].append(jnp.stack([gg[(b + 1) * seq - kh:(b + 1) * seq] for b in range(bsz)]))
            out["ch_s"].append(ext[:, t_s:])
        x, xbf, eid, gates = _ln_residual(x, mix, ln_g, ln_b, layer, 0, alpha,
                                          router=(w_route[layer], b_route[layer], n_eg, per_g))
        yb_moe, slots = _hier_moe(x, eid[:, :TOP_K], gates[:, :TOP_K], moe_w_gate, moe_w_up, moe_w_down, layer)
        x, xbf = _ln_combine(x, yb_moe, slots, ln_g, ln_b, layer, 1, alpha)

    st = lambda k: jnp.stack(out[k])
    return (x[:mp].reshape(bsz, seq, d), x[mp:].reshape(nb_s, t_s, d),
            st("re_p"), st("im_p"), st("ssm_p"), st("sh_p"), st("ch_p"),
            st("re_s"), st("im_s"), ssm_s.reshape(state_ssm.shape), st("sh_s"), st("ch_s"))
```

```python
import functools
import math

import jax
import jax.numpy as jnp
from jax import lax
from jax.experimental import pallas as pl
from jax.experimental.pallas import tpu as pltpu

F32 = jnp.float32
BF16 = jnp.bfloat16
HIGHEST = lax.Precision.HIGHEST

LN_EPS = 1e-5
V7X_VMEM_LIMIT_BYTES = 56 * 1024 * 1024
LANES = 128
SUBLANES = 8

S5_GROUP = 16
SSD_HEAD_DIM = 64
SSD_GROUPS = 4
SSD_CHUNK = 128
TOP_K = 2

MM_TM = 512
MM_TN = 512
LN_TM = 256
MOE_BM = 256
MOE_TF = 512
MOE_TN = 4096
S5_CHUNK = 16
S5_GB = 8
S5_RELAYOUT_UNROLL = 4
CONF_TT = 256
CONF_RT = 32
CONF_HALO = 32
SAMPLE_SEQ_ROWS = 8


def _cparams(sem):
    return pltpu.CompilerParams(dimension_semantics=sem, vmem_limit_bytes=V7X_VMEM_LIMIT_BYTES)


def _dot(a, b):
    return jnp.dot(a, b, preferred_element_type=F32)


def _split3(a):
    hi = a.astype(BF16)
    r1 = a - hi.astype(F32)
    mid = r1.astype(BF16)
    lo = (r1 - mid.astype(F32)).astype(BF16)
    return hi, mid, lo


def _dot_select_lhs(a, onehot):
    hi, mid, lo = _split3(a)
    return (_dot(hi, onehot) + _dot(mid, onehot)) + _dot(lo, onehot)


def _dot_select_rhs(onehot, b):
    hi, mid, lo = _split3(b)
    return (_dot(onehot, hi) + _dot(onehot, mid)) + _dot(onehot, lo)


def _dot_nt(a, b):
    return lax.dot_general(a, b, (((1,), (1,)), ((), ())), preferred_element_type=F32)


def _dot_tn(a, b):
    return lax.dot_general(a, b, (((0,), (0,)), ((), ())), preferred_element_type=F32)


def _sigmoid(x):
    return 1.0 / (1.0 + jnp.exp(-x))


def _silu(x):
    return x * _sigmoid(x)


def _mm_kernel(*refs, n_lhs, n_col, has_bias, mode):
    pos = 0
    x_refs = refs[pos:pos + n_lhs]; pos += n_lhs
    w_refs = refs[pos:pos + n_lhs * n_col]; pos += n_lhs * n_col
    b_refs = ()
    if has_bias:
        b_refs = refs[pos:pos + n_col]; pos += n_col
    gate_ref = None
    if mode == "gate":
        gate_ref = refs[pos]; pos += 1
    out_ref = refs[pos]; pos += 1
    wbf_refs = refs[pos:]

    @pl.when(pl.program_id(1) == 0)
    def _():
        for w_ref, wbf_ref in zip(w_refs, wbf_refs):
            wbf_ref[...] = w_ref[...].astype(BF16)

    accs = []
    for c in range(n_col):
        acc = None
        for l in range(n_lhs):
            part = _dot(x_refs[l][...].astype(BF16), wbf_refs[l * n_col + c][...])
            acc = part if acc is None else acc + part
        if has_bias:
            acc = acc + b_refs[c][...]
        accs.append(acc)
    if mode == "glu":
        res = accs[0] * _sigmoid(accs[1])
    elif mode == "gate":
        res = gate_ref[...] * _sigmoid(accs[0])
    else:
        res = accs[0]
    out_ref[...] = res.astype(out_ref.dtype)


def _matmul(xs, w, layer, bias=None, mode="plain", gate=None, n_out=None, out_dtype=F32,
            tm=MM_TM, tn=MM_TN):
    n_lhs = len(xs)
    m, k = xs[0].shape
    n_total = w.shape[2]
    n_col = 2 if mode == "glu" else 1
    if n_out is None:
        n_out = n_total // n_col
    assert m % tm == 0 and w.shape[1] == n_lhs * k
    nj = pl.cdiv(n_out, tn)
    glu_off = (n_total // 2) // tn if mode == "glu" else 0
    if mode == "glu":
        assert (n_total // 2) % tn == 0

    in_specs = [pl.BlockSpec((tm, k), lambda j, i: (i, 0)) for _ in range(n_lhs)]
    args = list(xs)
    for l in range(n_lhs):
        for c in range(n_col):
            in_specs.append(pl.BlockSpec((None, k, tn),
                                         functools.partial(lambda j, i, l, c: (layer, l, j + c * glu_off), l=l, c=c)))
            args.append(w)
    if bias is not None:
        b3 = bias.reshape(bias.shape[0], 1, bias.shape[1])
        for c in range(n_col):
            in_specs.append(pl.BlockSpec((None, 1, tn),
                                         functools.partial(lambda j, i, c: (layer, 0, j + c * glu_off), c=c)))
            args.append(b3)
    if mode == "gate":
        in_specs.append(pl.BlockSpec((tm, tn), lambda j, i: (i, j)))
        args.append(gate)
    kern = functools.partial(_mm_kernel, n_lhs=n_lhs, n_col=n_col, has_bias=bias is not None, mode=mode)
    return pl.pallas_call(
        kern,
        grid=(nj, m // tm),
        in_specs=in_specs,
        out_specs=pl.BlockSpec((tm, tn), lambda j, i: (i, j)),
        out_shape=jax.ShapeDtypeStruct((m, n_out), out_dtype),
        scratch_shapes=[pltpu.VMEM((k, tn), BF16) for _ in range(n_lhs * n_col)],
        compiler_params=_cparams(("arbitrary", "arbitrary")),
        name=f"mm_{mode}",
    )(*args)


def _put_rows_kernel(small_ref, big_ref, out_ref):
    del big_ref
    out_ref[...] = small_ref[...]


def _put_rows(big, small, row0):
    n, w = small.shape
    assert row0 % n == 0 and big.shape[1] == w and big.dtype == small.dtype
    return pl.pallas_call(
        _put_rows_kernel, grid=(1,),
        in_specs=[pl.BlockSpec((n, w), lambda i: (0, 0)), pl.BlockSpec(memory_space=pl.ANY)],
        out_specs=pl.BlockSpec((n, w), lambda i: (row0 // n, 0)),
        out_shape=jax.ShapeDtypeStruct(big.shape, big.dtype),
        input_output_aliases={1: 0}, compiler_params=_cparams(("arbitrary",)), name="put_rows",
    )(small, big)


def _ln_rows(y, g, b):
    mu = jnp.mean(y, axis=-1, keepdims=True)
    yc = y - mu
    var = jnp.mean(yc * yc, axis=-1, keepdims=True)
    return yc * lax.rsqrt(var + LN_EPS) * g + b


def _first_argmax(v, lane, big):
    m = jnp.max(v, axis=-1, keepdims=True)
    idx = jnp.min(jnp.where(v == m, lane, big), axis=-1, keepdims=True)
    return m, idx


def _ln_kernel(*refs, alpha, route, n_groups, per_group):
    if route:
        x_ref, mix_ref, g_ref, b_ref, wrh_ref, wrl_ref, br_ref, out_ref, outbf_ref, eid_ref, gate_ref = refs
    else:
        x_ref, mix_ref, g_ref, b_ref, out_ref, outbf_ref = refs
    y = alpha * x_ref[...] + mix_ref[...]
    out = _ln_rows(y, g_ref[...], b_ref[...])
    out_ref[...] = out
    outbf_ref[...] = out.astype(BF16)
    if route:
        out_hi = out.astype(BF16)
        out_lo = (out - out_hi.astype(F32)).astype(BF16)
        wr_hi = wrh_ref[...]
        lg = _dot(out_hi, wr_hi) + (_dot(out_lo, wr_hi) + _dot(out_hi, wrl_ref[...])) + br_ref[...]
        lane = lax.broadcasted_iota(jnp.int32, lg.shape, 1)
        neg = jnp.float32(-jnp.inf)
        big = jnp.int32(LANES)
        gl = jnp.where(lane < n_groups, lg, neg)
        gmax, grp = _first_argmax(gl, lane, big)
        pg_top = 1.0 / jnp.sum(jnp.exp(gl - gmax), axis=-1, keepdims=True)
        lo = n_groups + grp * per_group
        el = jnp.where((lane >= lo) & (lane < lo + per_group), lg, neg)
        m1, i1 = _first_argmax(el, lane, big)
        el2 = jnp.where(lane == i1, neg, el)
        m2, i2 = _first_argmax(el2, lane, big)
        e2 = jnp.exp(m2 - m1)
        g1 = pg_top / (1.0 + e2)
        g2 = pg_top * e2 / (1.0 + e2)
        eid_ref[...] = jnp.where(lane == 0, i1 - n_groups, jnp.where(lane == 1, i2 - n_groups, 0))
        gate_ref[...] = jnp.where(lane == 0, g1, jnp.where(lane == 1, g2, 0.0))


def _ln_residual(x, mix, ln_g, ln_b, layer, which, alpha, router=None, tm=LN_TM):
    m, d = x.shape
    assert m % tm == 0
    row = pl.BlockSpec((tm, d), lambda i: (i, 0))
    par = pl.BlockSpec((None, None, 1, d), lambda i: (layer, which, 0, 0))
    g4 = ln_g.reshape(ln_g.shape[0], ln_g.shape[1], 1, d)
    b4 = ln_b.reshape(ln_b.shape[0], ln_b.shape[1], 1, d)
    in_specs = [row, row, par, par]
    args = [x, mix, g4, b4]
    out_specs = [row, row]
    out_shape = [jax.ShapeDtypeStruct((m, d), F32), jax.ShapeDtypeStruct((m, d), BF16)]
    route = router is not None
    n_groups = per_group = 0
    if route:
        wr, br, n_groups, per_group = router
        wr_hi = wr.astype(BF16)
        wr_lo = (wr - wr_hi.astype(F32)).astype(BF16)
        in_specs += [pl.BlockSpec((d, LANES), lambda i: (0, 0)), pl.BlockSpec((d, LANES), lambda i: (0, 0)),
                     pl.BlockSpec((1, LANES), lambda i: (0, 0))]
        args += [wr_hi, wr_lo, br]
        small = pl.BlockSpec((tm, LANES), lambda i: (i, 0))
        out_specs += [small, small]
        out_shape += [jax.ShapeDtypeStruct((m, LANES), jnp.int32), jax.ShapeDtypeStruct((m, LANES), F32)]
    kern = functools.partial(_ln_kernel, alpha=alpha, route=route, n_groups=n_groups, per_group=per_group)
    return pl.pallas_call(
        kern, grid=(m // tm,), in_specs=in_specs, out_specs=out_specs, out_shape=out_shape,
        compiler_params=_cparams(("parallel",)), name="ln_router" if route else "ln",
    )(*args)


def _pair_row_copy(y_hbm, buf_ref, sem_ref, slot, k, src_row, dst_row, n_rows):
    return pltpu.make_async_copy(y_hbm.at[pl.ds(src_row, n_rows), :],
                                 buf_ref.at[slot, k, pl.ds(dst_row, n_rows), :], sem_ref.at[slot])


def _ln_combine_kernel(slot_ref, x_ref, gate_ref, y_hbm, g_ref, b_ref, out_ref, outbf_ref, buf_ref, sem_ref, *,
                       alpha):
    i = pl.program_id(0)
    tm = x_ref.shape[0]
    cur = i % 2

    def issue(tile, dst):
        base = tile * (tm * TOP_K)

        def body(r, carry):
            for k in range(TOP_K):
                _pair_row_copy(y_hbm, buf_ref, sem_ref, dst, k, slot_ref[base + TOP_K * r + k], r, 1).start()
            return carry

        lax.fori_loop(0, tm, body, 0, unroll=4)

    @pl.when(i == 0)
    def _():
        issue(0, 0)

    @pl.when(i + 1 < pl.num_programs(0))
    def _():
        issue(i + 1, 1 - cur)

    for k in range(TOP_K):
        _pair_row_copy(y_hbm, buf_ref, sem_ref, cur, k, 0, 0, tm).wait()
    gates = gate_ref[...]
    y = alpha * x_ref[...] + (buf_ref[cur, 0] * gates[:, 0:1] + buf_ref[cur, 1] * gates[:, 1:2])
    out = _ln_rows(y, g_ref[...], b_ref[...])
    out_ref[...] = out
    outbf_ref[...] = out.astype(BF16)


def _ln_combine(x, yb, slot_flat, gates, ln_g, ln_b, layer, which, alpha, tm=LN_TM):
    m, d = x.shape
    assert m % tm == 0 and TOP_K == 2
    row = pl.BlockSpec((tm, d), lambda i, s: (i, 0))
    gate_spec = pl.BlockSpec((tm, LANES), lambda i, s: (i, 0))
    par = pl.BlockSpec((None, None, 1, d), lambda i, s: (layer, which, 0, 0))
    g4 = ln_g.reshape(ln_g.shape[0], ln_g.shape[1], 1, d)
    b4 = ln_b.reshape(ln_b.shape[0], ln_b.shape[1], 1, d)
    return pl.pallas_call(
        functools.partial(_ln_combine_kernel, alpha=alpha),
        grid_spec=pltpu.PrefetchScalarGridSpec(
            num_scalar_prefetch=1, grid=(m // tm,),
            in_specs=[row, gate_spec, pl.BlockSpec(memory_space=pl.ANY), par, par],
            out_specs=[row, row],
            scratch_shapes=[pltpu.VMEM((2, TOP_K, tm, d), F32), pltpu.SemaphoreType.DMA((2,))]),
        out_shape=[jax.ShapeDtypeStruct((m, d), F32), jax.ShapeDtypeStruct((m, d), BF16)],
        compiler_params=pltpu.CompilerParams(dimension_semantics=("arbitrary",),
                                             vmem_limit_bytes=V7X_VMEM_LIMIT_BYTES,
                                             disable_bounds_checks=True),
        name="ln_combine",
    )(slot_flat, x, gates, yb, g4, b4)


def _moe_row_copy(x_hbm, xbuf_ref, sem_ref, slot, src_row, dst_row, n_rows):
    return pltpu.make_async_copy(x_hbm.at[pl.ds(src_row, n_rows), :],
                                 xbuf_ref.at[slot, pl.ds(dst_row, n_rows), :], sem_ref.at[slot])


def _moe_up_kernel(be_ref, nu_ref, tok_ref, x_hbm, wg_ref, wu_ref, h_ref, wgbf_ref, wubf_ref, xbuf_ref, sem_ref):
    i = pl.program_id(0)
    n_used = nu_ref[0]
    bm = xbuf_ref.shape[1]
    slot = i % 2

    def issue(blk, dst_slot):
        base = blk * bm

        def body(r, carry):
            _moe_row_copy(x_hbm, xbuf_ref, sem_ref, dst_slot, tok_ref[base + r], r, 1).start()
            return carry

        lax.fori_loop(0, bm, body, 0, unroll=8)

    @pl.when(i == 0)
    def _():
        issue(0, 0)

    prev = be_ref[jnp.maximum(i - 1, 0)]

    @pl.when((i == 0) | (be_ref[i] != prev))
    def _():
        wgbf_ref[...] = wg_ref[...].astype(BF16)
        wubf_ref[...] = wu_ref[...].astype(BF16)

    def wait_rows():
        _moe_row_copy(x_hbm, xbuf_ref, sem_ref, slot, 0, 0, bm).wait()

    @pl.when(i == n_used)
    def _():
        wait_rows()

    @pl.when(i < n_used)
    def _():
        wait_rows()
        nxt = (i + 1) * bm
        for r in range(bm):
            _moe_row_copy(x_hbm, xbuf_ref, sem_ref, 1 - slot, tok_ref[nxt + r], r, 1).start()
        x = xbuf_ref[slot].astype(BF16)
        h = _silu(_dot(x, wgbf_ref[...])) * _dot(x, wubf_ref[...])
        h_ref[...] = h.astype(BF16)

    @pl.when(i >= n_used)
    def _():
        h_ref[...] = jnp.zeros(h_ref.shape, BF16)


def _moe_down_kernel(be_ref, nu_ref, h_ref, wd_ref, out_ref, wdbf_ref):
    i = pl.program_id(1)
    prev = be_ref[jnp.maximum(i - 1, 0)]
    fresh = (i == 0) | (be_ref[i] != prev)

    @pl.when(fresh)
    def _():
        wdbf_ref[...] = wd_ref[...].astype(BF16)

    @pl.when(i < nu_ref[0])
    def _():
        out_ref[...] = _dot(h_ref[...], wdbf_ref[...])

    @pl.when(i >= nu_ref[0])
    def _():
        out_ref[...] = jnp.zeros(out_ref.shape, F32)


def _moe_experts(x, tok_buf, blk_expert, n_used, w_gate, w_up, w_down, layer,
                 bm=MOE_BM, tn=MOE_TN):
    cap = tok_buf.shape[0]
    d = x.shape[1]
    f = w_gate.shape[3]
    n_blocks = cap // bm
    tn = min(tn, d)
    hidden = pl.pallas_call(
        _moe_up_kernel,
        grid_spec=pltpu.PrefetchScalarGridSpec(
            num_scalar_prefetch=3,
            grid=(n_blocks,),
            in_specs=[
                pl.BlockSpec(memory_space=pl.ANY),
                pl.BlockSpec((None, None, d, f), lambda i, be, nu, tok: (layer, be[i], 0, 0)),
                pl.BlockSpec((None, None, d, f), lambda i, be, nu, tok: (layer, be[i], 0, 0)),
            ],
            out_specs=pl.BlockSpec((bm, f), lambda i, be, nu, tok: (i, 0)),
            scratch_shapes=[pltpu.VMEM((d, f), BF16), pltpu.VMEM((d, f), BF16),
                            pltpu.VMEM((2, bm, d), F32), pltpu.SemaphoreType.DMA((2,))],
        ),
        out_shape=jax.ShapeDtypeStruct((cap, f), BF16),
        compiler_params=pltpu.CompilerParams(dimension_semantics=("arbitrary",),
                                             vmem_limit_bytes=V7X_VMEM_LIMIT_BYTES,
                                             disable_bounds_checks=True),
        name="moe_up",
    )(blk_expert, n_used, tok_buf, x, w_gate, w_up)
    return pl.pallas_call(
        _moe_down_kernel,
        grid_spec=pltpu.PrefetchScalarGridSpec(
            num_scalar_prefetch=2,
            grid=(d // tn, n_blocks),
            in_specs=[
                pl.BlockSpec((bm, f), lambda j, i, be, nu: (jnp.minimum(i, nu[0] - 1), 0)),
                pl.BlockSpec((None, None, f, tn), lambda j, i, be, nu: (layer, be[i], 0, j)),
            ],
            out_specs=pl.BlockSpec((bm, tn), lambda j, i, be, nu: (i, j)),
            scratch_shapes=[pltpu.VMEM((f, tn), BF16)],
        ),
        out_shape=jax.ShapeDtypeStruct((cap, d), F32),
        compiler_params=_cparams(("arbitrary", "arbitrary")),
        name="moe_down",
    )(blk_expert, n_used, hidden, w_down)


def _moe_dispatch(eid, n_experts, bm):
    t = eid.shape[0]
    n_assign = t * TOP_K
    flat_e = eid.reshape(-1)
    onehot = (flat_e[:, None] == jnp.arange(n_experts, dtype=jnp.int32)[None, :]).astype(jnp.int32)
    csum = jnp.cumsum(onehot, axis=0)
    rank = jnp.sum(csum * onehot, axis=1) - 1
    counts = csum[-1]
    padded = (counts + bm - 1) // bm * bm
    pad_end = jnp.cumsum(padded)
    pad_start = pad_end - padded
    slot = (pad_start[flat_e] + rank).astype(jnp.int32)
    n_blocks = -(-n_assign // bm) + n_experts
    cap = n_blocks * bm
    tok_buf = jnp.zeros((cap,), jnp.int32).at[slot].set(jnp.arange(n_assign, dtype=jnp.int32) // TOP_K)
    blk_expert = jnp.minimum(
        jnp.searchsorted(pad_end, jnp.arange(n_blocks, dtype=jnp.int32) * bm, side="right"),
        n_experts - 1).astype(jnp.int32)
    n_used = (pad_end[-1] // bm).astype(jnp.int32).reshape(1)
    return slot, tok_buf, blk_expert, n_used


def _hier_moe(x, eid, w_gate, w_up, w_down, layer):
    n_experts = w_gate.shape[1]
    slot, tok_buf, blk_expert, n_used = _moe_dispatch(eid, n_experts, MOE_BM)
    return _moe_experts(x, tok_buf, blk_expert, n_used, w_gate, w_up, w_down, layer), slot


def _s5_prepare(lam_re, lam_im, log_dt, b_re, b_im, c_re, c_im, d_skip, lc):
    g, p, c = b_re.shape
    dt = jnp.exp(log_dt.astype(F32))[:, None]
    lr, li = lam_re.astype(F32), lam_im.astype(F32)
    mag = jnp.exp(lr * dt)
    ab_re, ab_im = mag * jnp.cos(li * dt), mag * jnp.sin(li * dt)
    den = lr * lr + li * li
    q_re = ((ab_re - 1.0) * lr + ab_im * li) / den
    q_im = (ab_im * lr - (ab_re - 1.0) * li) / den
    bb_re = q_re[..., None] * b_re - q_im[..., None] * b_im
    bb_im = q_re[..., None] * b_im + q_im[..., None] * b_re
    ks = jnp.arange(lc + 1, dtype=F32)[:, None, None]
    pmag = jnp.exp(lr * dt * ks)
    pw_re, pw_im = pmag * jnp.cos(li * dt * ks), pmag * jnp.sin(li * dt * ks)
    t_re = pw_re[:lc, :, :, None] * bb_re[None] - pw_im[:lc, :, :, None] * bb_im[None]
    t_im = pw_re[:lc, :, :, None] * bb_im[None] + pw_im[:lc, :, :, None] * bb_re[None]
    kt = (jnp.einsum("gop,kgpi->gkoi", c_re, t_re, precision=HIGHEST)
          - jnp.einsum("gop,kgpi->gkoi", c_im, t_im, precision=HIGHEST))
    s_idx = jnp.arange(lc)[:, None]
    t_idx = jnp.arange(lc)[None, :]
    tau = t_idx - s_idx
    toe = kt[:, jnp.clip(tau, 0, lc - 1)]
    toe = jnp.where((tau >= 0)[None, :, :, None, None], toe, 0.0)
    m_intra = jnp.transpose(toe, (0, 1, 4, 2, 3)).reshape(g, lc * c, lc * c)
    rev = pw_re[:lc][::-1], pw_im[:lc][::-1]
    wst_re = rev[0][:, :, :, None] * bb_re[None] - rev[1][:, :, :, None] * bb_im[None]
    wst_im = rev[0][:, :, :, None] * bb_im[None] + rev[1][:, :, :, None] * bb_re[None]
    wst_re = jnp.transpose(wst_re, (1, 0, 3, 2)).reshape(g, lc * c, p)
    wst_im = jnp.transpose(wst_im, (1, 0, 3, 2)).reshape(g, lc * c, p)
    half = (jnp.arange(g) % 2)[:, None] == jnp.arange(2)[None, :]

    def place_cols(wm):
        return jnp.where(half[:, None, :, None], wm[:, :, None, :], 0.0).reshape(g, lc * c, 2 * p)

    wy_re = c_re[:, None] * pw_re[1:].transpose(1, 0, 2)[:, :, None, :] \
        - c_im[:, None] * pw_im[1:].transpose(1, 0, 2)[:, :, None, :]
    wy_im = -(c_re[:, None] * pw_im[1:].transpose(1, 0, 2)[:, :, None, :]
              + c_im[:, None] * pw_re[1:].transpose(1, 0, 2)[:, :, None, :])
    wy_re = jnp.transpose(wy_re, (0, 3, 1, 2)).reshape(g, p, lc * c)
    wy_im = jnp.transpose(wy_im, (0, 3, 1, 2)).reshape(g, p, lc * c)

    def place_rows(wm):
        return jnp.where(half[:, :, None, None], wm[:, None, :, :], 0.0).reshape(g, 2 * p, lc * c)

    al_re = pw_re[lc].reshape(g // 2, 1, 2 * p)
    al_im = pw_im[lc].reshape(g // 2, 1, 2 * p)
    d_ch = jnp.tile(d_skip.astype(F32).reshape(g, 1, c), (1, lc, 1)).reshape(g, 1, lc * c)
    return dict(m=m_intra.astype(BF16), wst_re=place_cols(wst_re).astype(BF16),
                wst_im=place_cols(wst_im).astype(BF16), wy_re=place_rows(wy_re).astype(BF16),
                wy_im=place_rows(wy_im).astype(BF16), al_re=al_re, al_im=al_im, d=d_ch)


def _block_transpose8(vs, blk):
    vs = list(vs)
    lane_blk = lax.broadcasted_iota(jnp.int32, vs[0].shape, 1) // blk
    for s in (4, 2, 1):
        low = (lane_blk & s) == 0
        for i in range(8):
            if i & s:
                continue
            a, b = vs[i], vs[i + s]
            vs[i] = jnp.where(low, a, pltpu.roll(b, s * blk, axis=1))
            vs[i + s] = jnp.where(low, pltpu.roll(a, LANES - s * blk, axis=1), b)
    return vs


def _rows_to_chunks(u_ref, uc_ref, lc):
    gb, rows, w = uc_ref.shape
    blk = w // lc

    def body(rb, carry):
        r0 = pl.multiple_of(rb * SUBLANES, SUBLANES)
        for half in range(lc // 8):
            vs = [u_ref[pl.ds(r0 * lc + half * 8 + l, SUBLANES, stride=lc), :] for l in range(8)]
            ws = _block_transpose8(vs, blk)
            for g in range(gb):
                uc_ref[g, pl.ds(r0, SUBLANES), half * LANES:(half + 1) * LANES] = ws[g]
        return carry

    lax.fori_loop(0, rows // SUBLANES, body, 0, unroll=S5_RELAYOUT_UNROLL)


def _chunks_to_rows(yc_ref, h_ref, lc):
    gb, rows, w = yc_ref.shape
    blk = w // lc

    def body(rb, carry):
        r0 = pl.multiple_of(rb * SUBLANES, SUBLANES)
        for half in range(lc // 8):
            ws = [yc_ref[g, pl.ds(r0, SUBLANES), half * LANES:(half + 1) * LANES] for g in range(gb)]
            vs = _block_transpose8(ws, blk)
            for l in range(8):
                h_ref[pl.ds(r0 * lc + half * 8 + l, SUBLANES, stride=lc), :] = vs[l]
        return carry

    lax.fori_loop(0, rows // SUBLANES, body, 0, unroll=S5_RELAYOUT_UNROLL)


def _s5_kernel(*refs, gb, nb, nk, lc, natural):
    (u_ref, m_ref, wsr_ref, wsi_ref, wyr_ref, wyi_ref, alr_ref, ali_ref, d_ref, s0r_ref, s0i_ref,
     h_ref, sfr_ref, sfi_ref, locr_ref, loci_ref, str_ref, sti_ref) = refs[:18]
    if natural:
        uc_ref, yc_ref = refs[18:]
        _rows_to_chunks(u_ref, uc_ref, lc)
    else:
        uc_ref, yc_ref = u_ref, h_ref
    for pair in range(gb // 2):
        g0, g1 = 2 * pair, 2 * pair + 1
        u0 = uc_ref[g0]
        u1 = uc_ref[g1]
        ub0 = u0.astype(BF16)
        ub1 = u1.astype(BF16)
        loc_re = _dot(ub0, wsr_ref[g0]) + _dot(ub1, wsr_ref[g1])
        loc_im = _dot(ub0, wsi_ref[g0]) + _dot(ub1, wsi_ref[g1])
        a_re = alr_ref[pair]
        a_im = ali_ref[pair]
        s_re = s0r_ref[pair]
        s_im = s0i_ref[pair]
        if nk == 1:
            st_re, st_im = s_re, s_im
            f_re = a_re * s_re - a_im * s_im + loc_re
            f_im = a_re * s_im + a_im * s_re + loc_im
        else:
            locr_ref[...] = loc_re
            loci_ref[...] = loc_im

            def step(k, carry):
                c_re, c_im = carry
                rows = pl.ds(k, nb, stride=nk)
                str_ref[rows, :] = c_re
                sti_ref[rows, :] = c_im
                l_re = locr_ref[rows, :]
                l_im = loci_ref[rows, :]
                return (a_re * c_re - a_im * c_im + l_re, a_re * c_im + a_im * c_re + l_im)

            f_re, f_im = lax.fori_loop(0, nk, step, (s_re, s_im))
            st_re = str_ref[...]
            st_im = sti_ref[...]
        sfr_ref[pair] = f_re
        sfi_ref[pair] = f_im
        sb_re = st_re.astype(BF16)
        sb_im = st_im.astype(BF16)
        for gi, u, ub in ((g0, u0, ub0), (g1, u1, ub1)):
            y = (_dot(ub, m_ref[gi]) + _dot(sb_re, wyr_ref[gi]) + _dot(sb_im, wyi_ref[gi])
                 + d_ref[gi] * u)
            yc_ref[gi] = jax.nn.gelu(y)
    if natural:
        _chunks_to_rows(yc_ref, h_ref, lc)


def _s5_mixer(u, prep, s0_re, s0_im, nb, nk, gb=S5_GB, natural_rows=None):
    g, w, _ = prep["m"].shape
    p2 = prep["al_re"].shape[2]
    r = nb * nk
    natural = natural_rows is not None
    grp = lambda *shape: pl.BlockSpec((gb,) + shape, lambda i: (i, 0, 0))
    pr = lambda *shape: pl.BlockSpec((gb // 2,) + shape, lambda i: (i, 0, 0))
    scratch = [pltpu.VMEM((r, p2), F32) for _ in range(4)]
    if natural:
        lc = w // S5_GROUP
        assert gb == 8 and gb * S5_GROUP == LANES and lc % 8 == 0 and r % SUBLANES == 0
        io_spec = pl.BlockSpec((r * lc, LANES), lambda i: (0, i))
        out_h = jax.ShapeDtypeStruct((natural_rows, g * S5_GROUP), F32)
        scratch += [pltpu.VMEM((gb, r, w), F32), pltpu.VMEM((gb, r, w), F32)]
    else:
        lc = 0
        io_spec = grp(r, w)
        out_h = jax.ShapeDtypeStruct((g, r, w), F32)
    kern = functools.partial(_s5_kernel, gb=gb, nb=nb, nk=nk, lc=lc, natural=natural)
    return pl.pallas_call(
        kern,
        grid=(g // gb,),
        in_specs=[io_spec, grp(w, w), grp(w, p2), grp(w, p2), grp(p2, w), grp(p2, w),
                  pr(1, p2), pr(1, p2), grp(1, w), pr(nb, p2), pr(nb, p2)],
        out_specs=[io_spec, pr(nb, p2), pr(nb, p2)],
        out_shape=[out_h,
                   jax.ShapeDtypeStruct((g // 2, nb, p2), F32),
                   jax.ShapeDtypeStruct((g // 2, nb, p2), F32)],
        scratch_shapes=scratch,
        compiler_params=_cparams(("parallel",)),
        name="s5_prompt" if natural else "s5_sample",
    )(u, prep["m"], prep["wst_re"], prep["wst_im"], prep["wy_re"], prep["wy_im"],
      prep["al_re"], prep["al_im"], prep["d"], s0_re, s0_im)


def _to_chunks(u2d, nb, nk, lc, g):
    c = u2d.shape[1] // g
    u5 = u2d.reshape(nb, nk, lc, g, c)
    return jnp.transpose(u5, (3, 0, 1, 2, 4)).reshape(g, nb * nk, lc * c)


def _from_chunks(hc, nb, nk, lc, g):
    c = hc.shape[2] // lc
    h5 = hc.reshape(g, nb, nk, lc, c)
    return jnp.transpose(h5, (1, 2, 3, 0, 4)).reshape(nb * nk * lc, g * c)


def _state_to_pairs(s):
    b, g, p = s.shape
    return jnp.transpose(s.reshape(b, g // 2, 2 * p), (1, 0, 2))


def _pairs_to_state(f):
    g2, b, p2 = f.shape
    return jnp.transpose(f, (1, 0, 2)).reshape(b, g2 * 2, p2 // 2)


def _softplus(x):
    return jnp.maximum(x, 0.0) + jnp.log1p(jnp.exp(-jnp.abs(x)))


def _ssd_kernel(*refs, prompt, q, n_heads, hd, has_prev):
    it = iter(refs)
    z_ref, xs_ref, bm_ref, cm_ref, dt_ref = (next(it) for _ in range(5))
    if prompt:
        hx_ref, hb_ref, hc_ref = (next(it) for _ in range(3))
    cwx_ref, cwb_ref, cwc_ref, cbx_ref, cbb_ref, cbc_ref = (next(it) for _ in range(6))
    dtb_ref, a_ref, dsk_ref, nw_ref, sel_ref, e_ref, et_ref = (next(it) for _ in range(7))
    if not prompt:
        sin_ref = next(it)
    if has_prev:
        next(it)
    y_ref, sout_ref = next(it), next(it)
    padx_ref, padb_ref, padc_ref, yacc_ref = (next(it) for _ in range(4))

    r = z_ref.shape[0]
    nseq = r // q
    hpg = xs_ref.shape[1] // hd
    row = lax.broadcasted_iota(jnp.int32, (r, 1), 0)
    if prompt:
        first = pl.program_id(2) == 0
        keep = jnp.where(first, 0.0, 1.0)
        live = None
    else:
        live = (row % q) >= (q // 2)

    def conv(x_ref, halo_ref, pad_ref, w_ref, b_ref):
        if prompt:
            pad_ref[0:SUBLANES, :] = halo_ref[...] * keep
        else:
            pad_ref[0:SUBLANES, :] = jnp.zeros((SUBLANES, pad_ref.shape[1]), F32)
        pad_ref[SUBLANES:SUBLANES + r, :] = x_ref[...]
        kk = w_ref.shape[0]
        acc = b_ref[...]
        for k in range(kk):
            acc = acc + w_ref[k:k + 1, :] * pad_ref[pl.ds(SUBLANES - (kk - 1) + k, r), :]
        return _silu(acc)

    xs = conv(xs_ref, hx_ref if prompt else None, padx_ref, cwx_ref, cbx_ref)
    bc = conv(bm_ref, hb_ref if prompt else None, padb_ref, cwb_ref, cbb_ref)
    cc = conv(cm_ref, hc_ref if prompt else None, padc_ref, cwc_ref, cbc_ref)

    lane = lax.broadcasted_iota(jnp.int32, (r, LANES), 1)
    dtv = jnp.where(lane < n_heads, _softplus(dt_ref[...] + dtb_ref[...]), 0.0)
    if not prompt:
        dtv = jnp.where(live, dtv, 0.0)
    sel = sel_ref[...]
    both = _dot_select_lhs(jnp.concatenate([dtv, dtv * a_ref[...]], axis=0), sel)
    dt8, adt8 = both[0:r], both[r:2 * r]

    ri = lax.broadcasted_iota(jnp.int32, (r, r), 0)
    ci = lax.broadcasted_iota(jnp.int32, (r, r), 1)
    same = (ri // q) == (ci // q)
    tri = same & (ci <= ri)
    masks = jnp.concatenate([jnp.where(tri, 1.0, 0.0), jnp.where(same, 1.0, 0.0)], axis=0).astype(BF16)
    sums = _dot_select_rhs(masks, adt8)
    acum, atot = sums[0:r], sums[r:2 * r]
    acum_t = acum.T
    expand = _dot_select_lhs(jnp.concatenate([jnp.exp(acum), dt8, jnp.exp(atot - acum)], axis=0), e_ref[...])
    eac_x, dt_x, dte_x = expand[0:r], expand[r:2 * r], expand[2 * r:3 * r]
    dec = _dot_select_rhs(et_ref[...], jnp.exp(atot).T)

    cb = _dot_nt(cc.astype(BF16), bc.astype(BF16))
    xdt = xs * dt_x
    neg = jnp.float32(-jnp.inf)
    lane_hd = lax.broadcasted_iota(jnp.int32, (r, 2 * hd), 1)
    for pair in range(hpg // 2):
        xp = xdt[:, pair * 2 * hd:(pair + 1) * 2 * hd]
        acc = None
        for half in range(2):
            hl = 2 * pair + half
            seg = acum[:, hl:hl + 1] - acum_t[hl:hl + 1, :]
            scores = cb * jnp.exp(jnp.where(tri, seg, neg))
            mask = (lane_hd < hd) if half == 0 else (lane_hd >= hd)
            part = _dot(scores.astype(BF16), jnp.where(mask, xp, 0.0).astype(BF16))
            acc = part if acc is None else acc + part
        yacc_ref[:, pair * 2 * hd:(pair + 1) * 2 * hd] = acc

    wgt = (xdt * dte_x).astype(BF16)
    bcb = bc.astype(BF16)
    ccb = cc.astype(BF16)
    if prompt:
        @pl.when(first)
        def _():
            sout_ref[...] = jnp.zeros(sout_ref.shape, F32)

        s_old = sout_ref[...]
        yacc_ref[...] += _dot_nt(ccb, s_old.astype(BF16)) * eac_x
        sout_ref[...] = dec[:, 0:1] * s_old + _dot_tn(wgt, bcb)
    else:
        for j in range(nseq):
            rows = slice(j * q, (j + 1) * q)
            s_old = sin_ref[j]
            yacc_ref[rows, :] += _dot_nt(ccb[rows], s_old.astype(BF16)) * eac_x[rows]
            wj = jnp.where((row // q) == j, wgt, jnp.zeros_like(wgt))
            sout_ref[j] = dec[:, j * q:j * q + 1] * s_old + _dot_tn(wj, bcb)

    y = yacc_ref[...] + dsk_ref[...] * xs
    y = y * _silu(z_ref[...])
    y = y * lax.rsqrt(jnp.mean(y * y, axis=-1, keepdims=True) + LN_EPS)
    y_ref[...] = (y * nw_ref[...]).astype(y_ref.dtype)


def _ssd_constants(n_heads, hd, n_groups):
    hpg = n_heads // n_groups
    lane = jnp.arange(LANES)
    sel = (lane[None, :, None] == (jnp.arange(n_groups)[:, None, None] * hpg + lane[None, None, :])) \
        & (lane[None, None, :] < hpg)
    e = (lane[:, None] == (jnp.arange(hpg * hd) // hd)[None, :])
    return sel.astype(BF16), e.astype(BF16), e.T.astype(BF16)


def _pad_lanes(v):
    return jnp.pad(v, ((0, 0), (0, LANES - v.shape[1])))[:, None, :]


def _ssd_params(conv_w, conv_b, dt_bias, a_log, d_skip, norm_w):
    n_heads = dt_bias.shape[1]
    hd = norm_w.shape[1] // n_heads
    return dict(
        ssd_conv_w=conv_w, ssd_conv_b3=conv_b[:, None, :],
        ssd_dt_bias3=_pad_lanes(dt_bias.astype(F32)),
        ssd_a3=_pad_lanes(-jnp.exp(a_log.astype(F32))),
        ssd_d3=jnp.repeat(d_skip.astype(F32), hd, axis=1)[:, None, :],
        ssd_norm3=norm_w[:, None, :],
        ssd_consts=_ssd_constants(n_heads, hd, SSD_GROUPS))


def _ssd_mixer(u, layer, prm, cols, *, prompt, nb, t, state_in=None, out_rows=None, state_prev=None):
    aliases = {}
    o_z, o_x, o_dt, w_b, n_st = cols
    n_groups = SSD_GROUPS
    hd = SSD_HEAD_DIM
    n_heads = w_b // hd
    gw = w_b // n_groups
    r = SSD_CHUNK
    sel, e, et = prm["ssd_consts"]
    o_b = o_x + w_b
    o_c = o_b + n_groups * n_st
    assert o_z % gw == 0 and o_x % gw == 0 and o_b % n_st == 0 and o_dt % LANES == 0 and n_st == LANES
    cw, cb = prm["ssd_conv_w"], prm["ssd_conv_b3"]

    if prompt:
        nc = t // r
        grid = (nb, n_groups, nc)
        rowblk = lambda b, g, c: b * nc + c
        halo = lambda b, g, c: jnp.maximum((b * t + c * r) // SUBLANES - 1, 0)
        im = lambda colf: (lambda b, g, c: (rowblk(b, g, c), colf(g)))
        hm = lambda colf: (lambda b, g, c: (halo(b, g, c), colf(g)))
        pm = lambda f: (lambda b, g, c: f(g))
        q = r
    else:
        q = SAMPLE_SEQ_ROWS
        grid = (nb * q // r, n_groups)
        im = lambda colf: (lambda i, g: (i, colf(g)))
        pm = lambda f: (lambda i, g: f(g))
    col_z = lambda g: o_z // gw + g
    col_x = lambda g: o_x // gw + g
    col_b = lambda g: o_b // n_st + g
    col_c = lambda g: o_c // n_st + g
    col_dt = lambda g: o_dt // LANES

    in_specs = [pl.BlockSpec((r, gw), im(col_z)), pl.BlockSpec((r, gw), im(col_x)),
                pl.BlockSpec((r, n_st), im(col_b)), pl.BlockSpec((r, n_st), im(col_c)),
                pl.BlockSpec((r, LANES), im(col_dt))]
    args = [u, u, u, u, u]
    if prompt:
        in_specs += [pl.BlockSpec((SUBLANES, gw), hm(col_x)), pl.BlockSpec((SUBLANES, n_st), hm(col_b)),
                     pl.BlockSpec((SUBLANES, n_st), hm(col_c))]
        args += [u, u, u]
    kk = cw.shape[1]
    in_specs += [
        pl.BlockSpec((None, kk, gw), pm(lambda g: (layer, 0, g))),
        pl.BlockSpec((None, kk, n_st), pm(lambda g: (layer, 0, w_b // n_st + g))),
        pl.BlockSpec((None, kk, n_st), pm(lambda g: (layer, 0, w_b // n_st + n_groups + g))),
        pl.BlockSpec((None, 1, gw), pm(lambda g: (layer, 0, g))),
        pl.BlockSpec((None, 1, n_st), pm(lambda g: (layer, 0, w_b // n_st + g))),
        pl.BlockSpec((None, 1, n_st), pm(lambda g: (layer, 0, w_b // n_st + n_groups + g))),
        pl.BlockSpec((None, 1, LANES), pm(lambda g: (layer, 0, 0))),
        pl.BlockSpec((None, 1, LANES), pm(lambda g: (layer, 0, 0))),
        pl.BlockSpec((None, 1, gw), pm(lambda g: (layer, 0, g))),
        pl.BlockSpec((None, 1, gw), pm(lambda g: (layer, 0, g))),
        pl.BlockSpec((None, LANES, LANES), pm(lambda g: (g, 0, 0))),
        pl.BlockSpec((LANES, gw), pm(lambda g: (0, 0))),
        pl.BlockSpec((gw, LANES), pm(lambda g: (0, 0))),
    ]
    args += [cw, cw, cw, cb, cb, cb, prm["ssd_dt_bias3"], prm["ssd_a3"], prm["ssd_d3"], prm["ssd_norm3"],
             sel, e, et]
    if prompt:
        y_rows = out_rows
        out_specs = [pl.BlockSpec((r, gw), lambda b, g, c: (b * nc + c, g)),
                     pl.BlockSpec((None, gw, n_st), lambda b, g, c: (b, g, 0))]
        s_shape = (nb, w_b, n_st)
    else:
        nseq = r // q
        s_off = layer * (nb // nseq)
        in_specs.append(pl.BlockSpec((nseq, gw, n_st), lambda i, g: (i + s_off, g, 0)))
        args.append(state_in)
        y_rows = nb * q
        out_specs = [pl.BlockSpec((r, gw), lambda i, g: (i, g)),
                     pl.BlockSpec((nseq, gw, n_st), lambda i, g: (i + s_off, g, 0))]
        s_shape = state_in.shape
        if state_prev is not None:
            in_specs.append(pl.BlockSpec(memory_space=pl.ANY))
            args.append(state_prev)
            aliases = {len(args) - 1: 1}
    kern = functools.partial(_ssd_kernel, prompt=prompt, q=q, n_heads=n_heads, hd=hd,
                             has_prev=state_prev is not None)
    return pl.pallas_call(
        kern, grid=grid, in_specs=in_specs, out_specs=out_specs, input_output_aliases=aliases,
        out_shape=[jax.ShapeDtypeStruct((y_rows, w_b), BF16), jax.ShapeDtypeStruct(s_shape, F32)],
        scratch_shapes=[pltpu.VMEM((r + SUBLANES, gw), F32), pltpu.VMEM((r + SUBLANES, n_st), F32),
                        pltpu.VMEM((r + SUBLANES, n_st), F32), pltpu.VMEM((r, gw), F32)],
        compiler_params=_cparams(("parallel", "parallel", "arbitrary") if prompt else ("parallel", "parallel")),
        name="ssd_prompt" if prompt else "ssd_sample",
    )(*args)


def _conf_prompt_kernel(g_ref, halo_ref, w_ref, b_ref, lg_ref, lb_ref, out_ref, pad_ref, conv_ref, *, rt):
    tt, d = g_ref.shape
    halo = halo_ref.shape[0]
    kk = w_ref.shape[1]
    nlt = d // LANES
    keep = jnp.where(pl.program_id(1) == 0, 0.0, 1.0)
    for lt in range(nlt):
        cols = slice(lt * LANES, (lt + 1) * LANES)
        pad_ref[lt, 0:halo, :] = halo_ref[:, cols] * keep
        pad_ref[lt, halo:halo + tt, :] = g_ref[:, cols]
    base = halo - (kk - 1)

    def lane_tile(lt, carry):
        wt = w_ref[lt]
        bt = b_ref[lt]
        for r0 in range(0, tt, rt):
            acc = jnp.broadcast_to(bt, (rt, LANES))
            for k in range(kk):
                acc = acc + wt[k:k + 1, :] * pad_ref[lt, base + r0 + k:base + r0 + k + rt, :]
            conv_ref[lt, r0:r0 + rt, :] = acc
        return carry

    lax.fori_loop(0, nlt, lane_tile, 0)

    s1 = conv_ref[0]
    for lt in range(1, nlt):
        s1 = s1 + conv_ref[lt]
    mu = jnp.sum(s1, axis=-1, keepdims=True) * (1.0 / d)
    s2 = None
    for lt in range(nlt):
        dv = conv_ref[lt] - mu
        s2 = dv * dv if s2 is None else s2 + dv * dv
    rstd = lax.rsqrt(jnp.sum(s2, axis=-1, keepdims=True) * (1.0 / d) + LN_EPS)
    for lt in range(nlt):
        cols = slice(lt * LANES, (lt + 1) * LANES)
        v = (conv_ref[lt] - mu) * rstd * lg_ref[:, cols] + lb_ref[:, cols]
        out_ref[:, cols] = _silu(v).astype(out_ref.dtype)


def _conf_prompt(g, layer, dw_w, dw_b3, ln_g3, ln_b3, nb, t, out_rows, tt=CONF_TT, rt=CONF_RT, halo=CONF_HALO):
    d = g.shape[1]
    n_layers, kk, _ = dw_w.shape
    assert t % tt == 0 and tt % rt == 0 and halo >= kk - 1 and tt % halo == 0
    nt = t // tt
    nlt = d // LANES
    w_tiles = jnp.transpose(dw_w.reshape(n_layers, kk, nlt, LANES), (0, 2, 1, 3))
    b_tiles = dw_b3.reshape(n_layers, nlt, 1, LANES)
    par = pl.BlockSpec((None, 1, d), lambda b, i: (layer, 0, 0))
    return pl.pallas_call(
        functools.partial(_conf_prompt_kernel, rt=rt),
        grid=(nb, nt),
        in_specs=[pl.BlockSpec((tt, d), lambda b, i: (b * nt + i, 0)),
                  pl.BlockSpec((halo, d), lambda b, i: (jnp.maximum((b * t + i * tt) // halo - 1, 0), 0)),
                  pl.BlockSpec((None, nlt, kk, LANES), lambda b, i: (layer, 0, 0, 0)),
                  pl.BlockSpec((None, nlt, 1, LANES), lambda b, i: (layer, 0, 0, 0)), par, par],
        out_specs=pl.BlockSpec((tt, d), lambda b, i: (b * nt + i, 0)),
        out_shape=jax.ShapeDtypeStruct((out_rows, d), BF16),
        scratch_shapes=[pltpu.VMEM((d // LANES, halo + tt, LANES), F32), pltpu.VMEM((d // LANES, tt, LANES), F32)],
        compiler_params=_cparams(("parallel", "parallel")),
        name="conf_prompt",
    )(g, g, w_tiles, b_tiles, ln_g3, ln_b3)


def _conf_sample_kernel(ext_ref, w_ref, b_ref, lg_ref, lb_ref, out_ref):
    kk = w_ref.shape[0]
    t_new = out_ref.shape[0]
    for t in range(t_new):
        acc = b_ref[...] + w_ref[0:1, :] * ext_ref[t]
        for k in range(1, kk):
            acc = acc + w_ref[k:k + 1, :] * ext_ref[t + k]
        out_ref[t] = _silu(_ln_rows(acc, lg_ref[...], lb_ref[...])).astype(out_ref.dtype)


def _conf_sample(ext_t, layer, dw_w, dw_b3, ln_g3, ln_b3, nbt=16):
    rows, nb, d = ext_t.shape
    kk = dw_w.shape[1]
    t_new = rows - (kk - 1)
    par = pl.BlockSpec((None, 1, d), lambda i: (layer, 0, 0))
    return pl.pallas_call(
        _conf_sample_kernel,
        grid=(nb // nbt,),
        in_specs=[pl.BlockSpec((rows, nbt, d), lambda i: (0, i, 0)),
                  pl.BlockSpec((None, kk, d), lambda i: (layer, 0, 0)), par, par, par],
        out_specs=pl.BlockSpec((t_new, nbt, d), lambda i: (0, i, 0)),
        out_shape=jax.ShapeDtypeStruct((t_new, nb, d), BF16),
        compiler_params=_cparams(("parallel",)),
        name="conf_sample",
    )(ext_t, dw_w, dw_b3, ln_g3, ln_b3)


def kernel(x_prompt, x_sample, state_s5_re, state_s5_im, state_ssm, state_ssd_conv, state_conformer_conv, w_in_even, s5_lam_re, s5_lam_im, s5_log_dt, s5_b_re, s5_b_im, s5_c_re, s5_c_im, s5_d, s5_w_glu, s5_b_glu, ssd_conv_w, ssd_conv_b, ssd_dt_bias, ssd_a_log, ssd_d, ssd_norm_w, w_out_even, conf_w_pw1, conf_b_pw1, conf_dw_w, conf_dw_b, conf_ln_g, conf_ln_b, conf_w_pw2, conf_b_pw2, ln_g, ln_b, moe_w_group, moe_b_group, moe_w_expert, moe_b_expert, moe_w_gate, moe_w_up, moe_w_down):
    bsz, seq, d = x_prompt.shape
    nb_s, t_s, _ = x_sample.shape
    depth = ln_g.shape[0]
    alpha = (2.0 * depth) ** 0.25
    mp, ms = bsz * seq, nb_s * t_s
    x = jnp.concatenate([x_prompt.reshape(mp, d), x_sample.reshape(ms, d)], axis=0)
    xbf = x.astype(BF16)

    g_a = s5_lam_re.shape[1]
    w_a = s5_d.shape[1]
    w_b = ssd_norm_w.shape[1]
    n_heads = ssd_dt_bias.shape[1]
    conv_dim = ssd_conv_w.shape[2]
    n_st = (conv_dim - w_b) // (2 * SSD_GROUPS)
    o_z, o_x = w_a, w_a + w_b
    o_dt = o_x + conv_dim
    in_even = o_dt + n_heads
    cols = (o_z, o_x, o_dt, w_b, n_st)
    ssd_prm = _ssd_params(ssd_conv_w, ssd_conv_b, ssd_dt_bias, ssd_a_log, ssd_d, ssd_norm_w)
    ssm_in = state_ssm.reshape(state_ssm.shape[0] * nb_s, w_b, n_st)

    n_eg, per_g = moe_w_expert.shape[1], moe_w_expert.shape[3]
    n_route = n_eg + n_eg * per_g
    w_route = jnp.concatenate(
        [moe_w_group, jnp.transpose(moe_w_expert, (0, 2, 1, 3)).reshape(depth, d, n_eg * per_g),
         jnp.zeros((depth, d, LANES - n_route), F32)], axis=-1)
    b_route = jnp.concatenate(
        [moe_b_group, moe_b_expert.reshape(depth, n_eg * per_g), jnp.zeros((depth, LANES - n_route), F32)],
        axis=-1)[:, None, :]

    conf_dw_b3 = conf_dw_b[:, None, :]
    conf_ln_g3 = conf_ln_g[:, None, :]
    conf_ln_b3 = conf_ln_b[:, None, :]
    s5_zero = jnp.zeros((g_a // 2, bsz, 2 * s5_lam_re.shape[2]), F32)

    out = dict(re_p=[], im_p=[], ssm_p=[], sh_p=[], ch_p=[], re_s=[], im_s=[], sh_s=[], ch_s=[])
    ssm_s = None
    for layer in range(depth):
        i = layer // 2
        if layer % 2 == 0:
            u = _matmul([xbf], w_in_even, i, n_out=in_even)
            prm = (s5_lam_re[i], s5_lam_im[i], s5_log_dt[i], s5_b_re[i], s5_b_im[i], s5_c_re[i], s5_c_im[i],
                   s5_d[i])
            nk = seq // S5_CHUNK
            h, fr_p, fi_p = _s5_mixer(u, _s5_prepare(*prm, S5_CHUNK), s5_zero, s5_zero, bsz, nk,
                                      natural_rows=mp + ms)
            hc_s, fr_s, fi_s = _s5_mixer(_to_chunks(u[mp:, :w_a], nb_s, 1, t_s, g_a),
                                         _s5_prepare(*prm, t_s), _state_to_pairs(state_s5_re[i]),
                                         _state_to_pairs(state_s5_im[i]), nb_s, 1)
            h = _put_rows(h, _from_chunks(hc_s, nb_s, 1, t_s, g_a), mp)
            ya = _matmul([h], s5_w_glu, i, bias=s5_b_glu, mode="gate", gate=h, out_dtype=BF16)
            out["re_p"].append(_pairs_to_state(fr_p)); out["im_p"].append(_pairs_to_state(fi_p))
            out["re_s"].append(_pairs_to_state(fr_s)); out["im_s"].append(_pairs_to_state(fi_s))
            yb, ssm_p = _ssd_mixer(u, i, ssd_prm, cols, prompt=True, nb=bsz, t=seq, out_rows=mp + ms)
            us3 = u[mp:].reshape(nb_s, t_s, in_even)
            dead = SAMPLE_SEQ_ROWS - t_s
            hist = state_ssd_conv[i]
            ext = jnp.concatenate([jnp.zeros((nb_s, dead, in_even), F32), us3], axis=1)
            ext = ext.at[:, dead - hist.shape[1]:dead, o_x:o_x + conv_dim].set(hist)
            yb_s, ssm_s = _ssd_mixer(ext.reshape(nb_s * SAMPLE_SEQ_ROWS, in_even), i, ssd_prm, cols, prompt=False,
                                     nb=nb_s, t=t_s, state_in=ssm_in, state_prev=ssm_s)
            yb_s = yb_s.reshape(nb_s, SAMPLE_SEQ_ROWS, w_b)[:, dead:].reshape(ms, w_b)
            yb = _put_rows(yb, yb_s, mp)
            mix = _matmul([ya, yb], w_out_even, i)
            kh = hist.shape[1]
            out["ssm_p"].append(ssm_p.reshape(bsz, n_heads, SSD_HEAD_DIM, n_st))
            out["sh_p"].append(jnp.stack([u[(b + 1) * seq - kh:(b + 1) * seq, o_x:o_x + conv_dim]
                                          for b in range(bsz)]))
            out["sh_s"].append(jnp.concatenate([hist, us3[:, :, o_x:o_x + conv_dim]], axis=1)[:, t_s:])
        else:
            gg = _matmul([xbf], conf_w_pw1, i, bias=conf_b_pw1, mode="glu")
            c = _conf_prompt(gg, i, conf_dw_w, conf_dw_b3, conf_ln_g3, conf_ln_b3, bsz, seq, mp + ms)
            hist = state_conformer_conv[i]
            g_s = gg[mp:].reshape(nb_s, t_s, d)
            ext = jnp.concatenate([hist, g_s], axis=1)
            c_s = _conf_sample(jnp.transpose(ext, (1, 0, 2)), i, conf_dw_w, conf_dw_b3, conf_ln_g3, conf_ln_b3)
            c = _put_rows(c, jnp.transpose(c_s, (1, 0, 2)).reshape(ms, d), mp)
            mix = _matmul([c], conf_w_pw2, i, bias=conf_b_pw2)
            kh = hist.shape[1]
            out["ch_p"].append(jnp.stack([gg[(b + 1) * seq - kh:(b + 1) * seq] for b in range(bsz)]))
            out["ch_s"].append(ext[:, t_s:])
        x, xbf, eid, gates = _ln_residual(x, mix, ln_g, ln_b, layer, 0, alpha,
                                          router=(w_route[layer], b_route[layer], n_eg, per_g))
        yb_moe, slots = _hier_moe(x, eid[:, :TOP_K], moe_w_gate, moe_w_up, moe_w_down, layer)
        x, xbf = _ln_combine(x, yb_moe, slots, gates, ln_g, ln_b, layer, 1, alpha)

    st = lambda k: jnp.stack(out[k])
    return (x[:mp].reshape(bsz, seq, d), x[mp:].reshape(nb_s, t_s, d),
            st("re_p"), st("im_p"), st("ssm_p"), st("sh_p"), st("ch_p"),
            st("re_s"), st("im_s"), ssm_s.reshape(state_ssm.shape), st("sh_s"), st("ch_s"))
```

```python
import functools
import math

import jax
import jax.numpy as jnp
from jax import lax
from jax.experimental import pallas as pl
from jax.experimental.pallas import tpu as pltpu

F32 = jnp.float32
BF16 = jnp.bfloat16
HIGHEST = lax.Precision.HIGHEST

LN_EPS = 1e-5
V7X_VMEM_LIMIT_BYTES = 56 * 1024 * 1024
LANES = 128
SUBLANES = 8

S5_GROUP = 16
SSD_HEAD_DIM = 64
SSD_GROUPS = 4
SSD_CHUNK = 128
TOP_K = 2

MM_TM = 512
MM_TN = 512
LN_TM = 256
MOE_BM = 256
S5_CHUNK = 16
S5_GB = 8
S5_RELAYOUT_UNROLL = 4
CONF_TT = 256
CONF_RT = 32
CONF_HALO = 32
SAMPLE_SEQ_ROWS = 8


def _cparams(sem):
    return pltpu.CompilerParams(dimension_semantics=sem, vmem_limit_bytes=V7X_VMEM_LIMIT_BYTES)


def _dot(a, b):
    return jnp.dot(a, b, preferred_element_type=F32)


def _split3(a):
    hi = a.astype(BF16)
    r1 = a - hi.astype(F32)
    mid = r1.astype(BF16)
    lo = (r1 - mid.astype(F32)).astype(BF16)
    return hi, mid, lo


def _dot_select_lhs(a, onehot):
    hi, mid, lo = _split3(a)
    return (_dot(hi, onehot) + _dot(mid, onehot)) + _dot(lo, onehot)


def _dot_select_rhs(onehot, b):
    hi, mid, lo = _split3(b)
    return (_dot(onehot, hi) + _dot(onehot, mid)) + _dot(onehot, lo)


def _dot_nt(a, b):
    return lax.dot_general(a, b, (((1,), (1,)), ((), ())), preferred_element_type=F32)


def _dot_tn(a, b):
    return lax.dot_general(a, b, (((0,), (0,)), ((), ())), preferred_element_type=F32)


def _sigmoid(x):
    return 1.0 / (1.0 + jnp.exp(-x))


def _silu(x):
    return x * _sigmoid(x)


def _mm_kernel(*refs, n_lhs, n_col, has_bias, mode):
    pos = 0
    x_refs = refs[pos:pos + n_lhs]; pos += n_lhs
    w_refs = refs[pos:pos + n_lhs * n_col]; pos += n_lhs * n_col
    b_refs = ()
    if has_bias:
        b_refs = refs[pos:pos + n_col]; pos += n_col
    gate_ref = None
    if mode == "gate":
        gate_ref = refs[pos]; pos += 1
    out_ref = refs[pos]; pos += 1
    wbf_refs = refs[pos:]

    @pl.when(pl.program_id(1) == 0)
    def _():
        for w_ref, wbf_ref in zip(w_refs, wbf_refs):
            wbf_ref[...] = w_ref[...].astype(BF16)

    accs = []
    for c in range(n_col):
        acc = None
        for l in range(n_lhs):
            part = _dot(x_refs[l][...].astype(BF16), wbf_refs[l * n_col + c][...])
            acc = part if acc is None else acc + part
        if has_bias:
            acc = acc + b_refs[c][...]
        accs.append(acc)
    if mode == "glu":
        res = accs[0] * _sigmoid(accs[1])
    elif mode == "gate":
        res = gate_ref[...] * _sigmoid(accs[0])
    else:
        res = accs[0]
    out_ref[...] = res.astype(out_ref.dtype)


def _matmul(xs, w, layer, bias=None, mode="plain", gate=None, n_out=None, out_dtype=F32,
            tm=MM_TM, tn=MM_TN):
    n_lhs = len(xs)
    m, k = xs[0].shape
    n_total = w.shape[2]
    n_col = 2 if mode == "glu" else 1
    if n_out is None:
        n_out = n_total // n_col
    assert m % tm == 0 and w.shape[1] == n_lhs * k
    nj = pl.cdiv(n_out, tn)
    glu_off = (n_total // 2) // tn if mode == "glu" else 0
    if mode == "glu":
        assert (n_total // 2) % tn == 0

    in_specs = [pl.BlockSpec((tm, k), lambda j, i: (i, 0)) for _ in range(n_lhs)]
    args = list(xs)
    for l in range(n_lhs):
        for c in range(n_col):
            in_specs.append(pl.BlockSpec((None, k, tn),
                                         functools.partial(lambda j, i, l, c: (layer, l, j + c * glu_off), l=l, c=c)))
            args.append(w)
    if bias is not None:
        b3 = bias.reshape(bias.shape[0], 1, bias.shape[1])
        for c in range(n_col):
            in_specs.append(pl.BlockSpec((None, 1, tn),
                                         functools.partial(lambda j, i, c: (layer, 0, j + c * glu_off), c=c)))
            args.append(b3)
    if mode == "gate":
        in_specs.append(pl.BlockSpec((tm, tn), lambda j, i: (i, j)))
        args.append(gate)
    kern = functools.partial(_mm_kernel, n_lhs=n_lhs, n_col=n_col, has_bias=bias is not None, mode=mode)
    return pl.pallas_call(
        kern,
        grid=(nj, m // tm),
        in_specs=in_specs,
        out_specs=pl.BlockSpec((tm, tn), lambda j, i: (i, j)),
        out_shape=jax.ShapeDtypeStruct((m, n_out), out_dtype),
        scratch_shapes=[pltpu.VMEM((k, tn), BF16) for _ in range(n_lhs * n_col)],
        compiler_params=_cparams(("arbitrary", "arbitrary")),
        name=f"mm_{mode}",
    )(*args)


def _put_rows_kernel(small_ref, big_ref, out_ref):
    del big_ref
    out_ref[...] = small_ref[...]


def _put_rows(big, small, row0):
    n, w = small.shape
    assert row0 % n == 0 and big.shape[1] == w and big.dtype == small.dtype
    return pl.pallas_call(
        _put_rows_kernel, grid=(1,),
        in_specs=[pl.BlockSpec((n, w), lambda i: (0, 0)), pl.BlockSpec(memory_space=pl.ANY)],
        out_specs=pl.BlockSpec((n, w), lambda i: (row0 // n, 0)),
        out_shape=jax.ShapeDtypeStruct(big.shape, big.dtype),
        input_output_aliases={1: 0}, compiler_params=_cparams(("arbitrary",)), name="put_rows",
    )(small, big)


def _ln_rows(y, g, b):
    mu = jnp.mean(y, axis=-1, keepdims=True)
    yc = y - mu
    var = jnp.mean(yc * yc, axis=-1, keepdims=True)
    return yc * lax.rsqrt(var + LN_EPS) * g + b


def _first_argmax(v, lane, big):
    m = jnp.max(v, axis=-1, keepdims=True)
    idx = jnp.min(jnp.where(v == m, lane, big), axis=-1, keepdims=True)
    return m, idx


def _ln_kernel(*refs, alpha, route, n_groups, per_group):
    if route:
        x_ref, mix_ref, g_ref, b_ref, wrh_ref, wrl_ref, br_ref, out_ref, outbf_ref, eid_ref, gate_ref = refs
    else:
        x_ref, mix_ref, g_ref, b_ref, out_ref, outbf_ref = refs
    y = alpha * x_ref[...] + mix_ref[...]
    out = _ln_rows(y, g_ref[...], b_ref[...])
    out_ref[...] = out
    outbf_ref[...] = out.astype(BF16)
    if route:
        out_hi = out.astype(BF16)
        out_lo = (out - out_hi.astype(F32)).astype(BF16)
        wr_hi = wrh_ref[...]
        lg = _dot(out_hi, wr_hi) + (_dot(out_lo, wr_hi) + _dot(out_hi, wrl_ref[...])) + br_ref[...]
        lane = lax.broadcasted_iota(jnp.int32, lg.shape, 1)
        neg = jnp.float32(-jnp.inf)
        big = jnp.int32(LANES)
        gl = jnp.where(lane < n_groups, lg, neg)
        gmax, grp = _first_argmax(gl, lane, big)
        pg_top = 1.0 / jnp.sum(jnp.exp(gl - gmax), axis=-1, keepdims=True)
        lo = n_groups + grp * per_group
        el = jnp.where((lane >= lo) & (lane < lo + per_group), lg, neg)
        m1, i1 = _first_argmax(el, lane, big)
        el2 = jnp.where(lane == i1, neg, el)
        m2, i2 = _first_argmax(el2, lane, big)
        e2 = jnp.exp(m2 - m1)
        g1 = pg_top / (1.0 + e2)
        g2 = pg_top * e2 / (1.0 + e2)
        eid_ref[...] = jnp.where(lane == 0, i1 - n_groups, jnp.where(lane == 1, i2 - n_groups, 0))
        gate_ref[...] = jnp.where(lane == 0, g1, jnp.where(lane == 1, g2, 0.0))


def _ln_residual(x, mix, ln_g, ln_b, layer, which, alpha, router=None, tm=LN_TM):
    m, d = x.shape
    assert m % tm == 0
    row = pl.BlockSpec((tm, d), lambda i: (i, 0))
    par = pl.BlockSpec((None, None, 1, d), lambda i: (layer, which, 0, 0))
    g4 = ln_g.reshape(ln_g.shape[0], ln_g.shape[1], 1, d)
    b4 = ln_b.reshape(ln_b.shape[0], ln_b.shape[1], 1, d)
    in_specs = [row, row, par, par]
    args = [x, mix, g4, b4]
    out_specs = [row, row]
    out_shape = [jax.ShapeDtypeStruct((m, d), F32), jax.ShapeDtypeStruct((m, d), BF16)]
    route = router is not None
    n_groups = per_group = 0
    if route:
        wr, br, n_groups, per_group = router
        wr_hi = wr.astype(BF16)
        wr_lo = (wr - wr_hi.astype(F32)).astype(BF16)
        in_specs += [pl.BlockSpec((d, LANES), lambda i: (0, 0)), pl.BlockSpec((d, LANES), lambda i: (0, 0)),
                     pl.BlockSpec((1, LANES), lambda i: (0, 0))]
        args += [wr_hi, wr_lo, br]
        small = pl.BlockSpec((tm, LANES), lambda i: (i, 0))
        out_specs += [small, small]
        out_shape += [jax.ShapeDtypeStruct((m, LANES), jnp.int32), jax.ShapeDtypeStruct((m, LANES), F32)]
    kern = functools.partial(_ln_kernel, alpha=alpha, route=route, n_groups=n_groups, per_group=per_group)
    return pl.pallas_call(
        kern, grid=(m // tm,), in_specs=in_specs, out_specs=out_specs, out_shape=out_shape,
        compiler_params=_cparams(("parallel",)), name="ln_router" if route else "ln",
    )(*args)


def _pair_row_copy(y_hbm, buf_ref, sem_ref, slot, k, src_row, dst_row, n_rows):
    return pltpu.make_async_copy(y_hbm.at[pl.ds(src_row, n_rows), :],
                                 buf_ref.at[slot, k, pl.ds(dst_row, n_rows), :], sem_ref.at[slot])


def _ln_combine_kernel(slot_ref, x_ref, gate_ref, y_hbm, g_ref, b_ref, out_ref, outbf_ref, buf_ref, sem_ref, *,
                       alpha):
    i = pl.program_id(0)
    tm = x_ref.shape[0]
    cur = i % 2

    def issue(tile, dst):
        base = tile * (tm * TOP_K)

        def body(r, carry):
            for k in range(TOP_K):
                _pair_row_copy(y_hbm, buf_ref, sem_ref, dst, k, slot_ref[base + TOP_K * r + k], r, 1).start()
            return carry

        lax.fori_loop(0, tm, body, 0, unroll=4)

    @pl.when(i == 0)
    def _():
        issue(0, 0)

    @pl.when(i + 1 < pl.num_programs(0))
    def _():
        issue(i + 1, 1 - cur)

    for k in range(TOP_K):
        _pair_row_copy(y_hbm, buf_ref, sem_ref, cur, k, 0, 0, tm).wait()
    gates = gate_ref[...]
    y = alpha * x_ref[...] + (buf_ref[cur, 0] * gates[:, 0:1] + buf_ref[cur, 1] * gates[:, 1:2])
    out = _ln_rows(y, g_ref[...], b_ref[...])
    out_ref[...] = out
    outbf_ref[...] = out.astype(BF16)


def _ln_combine(x, yb, slot_flat, gates, ln_g, ln_b, layer, which, alpha, tm=LN_TM):
    m, d = x.shape
    assert m % tm == 0 and TOP_K == 2
    row = pl.BlockSpec((tm, d), lambda i, s: (i, 0))
    gate_spec = pl.BlockSpec((tm, LANES), lambda i, s: (i, 0))
    par = pl.BlockSpec((None, None, 1, d), lambda i, s: (layer, which, 0, 0))
    g4 = ln_g.reshape(ln_g.shape[0], ln_g.shape[1], 1, d)
    b4 = ln_b.reshape(ln_b.shape[0], ln_b.shape[1], 1, d)
    return pl.pallas_call(
        functools.partial(_ln_combine_kernel, alpha=alpha),
        grid_spec=pltpu.PrefetchScalarGridSpec(
            num_scalar_prefetch=1, grid=(m // tm,),
            in_specs=[row, gate_spec, pl.BlockSpec(memory_space=pl.ANY), par, par],
            out_specs=[row, row],
            scratch_shapes=[pltpu.VMEM((2, TOP_K, tm, d), F32), pltpu.SemaphoreType.DMA((2,))]),
        out_shape=[jax.ShapeDtypeStruct((m, d), F32), jax.ShapeDtypeStruct((m, d), BF16)],
        compiler_params=pltpu.CompilerParams(dimension_semantics=("arbitrary",),
                                             vmem_limit_bytes=V7X_VMEM_LIMIT_BYTES,
                                             disable_bounds_checks=True),
        name="ln_combine",
    )(slot_flat, x, gates, yb, g4, b4)


def _moe_row_copy(x_hbm, xbuf_ref, sem_ref, slot, src_row, dst_row, n_rows):
    return pltpu.make_async_copy(x_hbm.at[pl.ds(src_row, n_rows), :],
                                 xbuf_ref.at[slot, pl.ds(dst_row, n_rows), :], sem_ref.at[slot])


def _expert_weight_stream(i, n_used, be_ref, nxt_ref, w_hbms, wbuf_ref, wsem_ref, layer, on_arrival):
    def copies(expert):
        return [pltpu.make_async_copy(w.at[layer, expert], wbuf_ref.at[k], wsem_ref.at[0])
                for k, w in enumerate(w_hbms)]

    @pl.when(i == 0)
    def _():
        for c in copies(be_ref[0]):
            c.start()

    prev = be_ref[jnp.maximum(i - 1, 0)]

    @pl.when((i == 0) | ((i < n_used) & (be_ref[i] != prev)))
    def _():
        for c in copies(be_ref[i]):
            c.wait()
        on_arrival()
        nxt = nxt_ref[i]

        @pl.when(nxt >= 0)
        def _():
            for c in copies(nxt):
                c.start()


def _moe_up_kernel(be_ref, nxt_ref, nu_ref, tok_ref, x_hbm, wg_hbm, wu_hbm, h_ref,
                   wbuf_ref, wgbf_ref, wubf_ref, xbuf_ref, sem_ref, wsem_ref, *, layer):
    i = pl.program_id(0)
    n_used = nu_ref[0]
    bm = xbuf_ref.shape[1]
    slot = i % 2

    def issue(blk, dst_slot):
        base = blk * bm

        def body(r, carry):
            _moe_row_copy(x_hbm, xbuf_ref, sem_ref, dst_slot, tok_ref[base + r], r, 1).start()
            return carry

        lax.fori_loop(0, bm, body, 0, unroll=8)

    @pl.when(i == 0)
    def _():
        issue(0, 0)

    def cast_weights():
        wgbf_ref[...] = wbuf_ref[0].astype(BF16)
        wubf_ref[...] = wbuf_ref[1].astype(BF16)

    _expert_weight_stream(i, n_used, be_ref, nxt_ref, (wg_hbm, wu_hbm), wbuf_ref, wsem_ref, layer, cast_weights)

    def wait_rows():
        _moe_row_copy(x_hbm, xbuf_ref, sem_ref, slot, 0, 0, bm).wait()

    @pl.when(i == n_used)
    def _():
        wait_rows()

    @pl.when(i < n_used)
    def _():
        wait_rows()
        nxt = (i + 1) * bm
        for r in range(bm):
            _moe_row_copy(x_hbm, xbuf_ref, sem_ref, 1 - slot, tok_ref[nxt + r], r, 1).start()
        x = xbuf_ref[slot].astype(BF16)
        h = _silu(_dot(x, wgbf_ref[...])) * _dot(x, wubf_ref[...])
        h_ref[...] = h.astype(BF16)

    @pl.when(i >= n_used)
    def _():
        h_ref[...] = jnp.zeros(h_ref.shape, BF16)


def _moe_down_kernel(be_ref, nxt_ref, nu_ref, h_ref, wd_hbm, out_ref, wbuf_ref, wdbf_ref, wsem_ref, *, layer):
    i = pl.program_id(0)

    def cast_weights():
        wdbf_ref[...] = wbuf_ref[0].astype(BF16)

    _expert_weight_stream(i, nu_ref[0], be_ref, nxt_ref, (wd_hbm,), wbuf_ref, wsem_ref, layer, cast_weights)

    @pl.when(i < nu_ref[0])
    def _():
        out_ref[...] = _dot(h_ref[...], wdbf_ref[...])

    @pl.when(i >= nu_ref[0])
    def _():
        out_ref[...] = jnp.zeros(out_ref.shape, F32)


def _moe_experts(x, tok_buf, blk_expert, blk_next, n_used, w_gate, w_up, w_down, layer, bm=MOE_BM):
    cap = tok_buf.shape[0]
    d = x.shape[1]
    f = w_gate.shape[3]
    n_blocks = cap // bm
    anyspace = pl.BlockSpec(memory_space=pl.ANY)
    gather_params = pltpu.CompilerParams(dimension_semantics=("arbitrary",),
                                         vmem_limit_bytes=V7X_VMEM_LIMIT_BYTES, disable_bounds_checks=True)
    hidden = pl.pallas_call(
        functools.partial(_moe_up_kernel, layer=layer),
        grid_spec=pltpu.PrefetchScalarGridSpec(
            num_scalar_prefetch=4,
            grid=(n_blocks,),
            in_specs=[anyspace, anyspace, anyspace],
            out_specs=pl.BlockSpec((bm, f), lambda i, be, nx, nu, tok: (i, 0)),
            scratch_shapes=[pltpu.VMEM((2, d, f), F32), pltpu.VMEM((d, f), BF16), pltpu.VMEM((d, f), BF16),
                            pltpu.VMEM((2, bm, d), F32),
                            pltpu.SemaphoreType.DMA((2,)), pltpu.SemaphoreType.DMA((1,))],
        ),
        out_shape=jax.ShapeDtypeStruct((cap, f), BF16),
        compiler_params=gather_params,
        name="moe_up",
    )(blk_expert, blk_next, n_used, tok_buf, x, w_gate, w_up)
    return pl.pallas_call(
        functools.partial(_moe_down_kernel, layer=layer),
        grid_spec=pltpu.PrefetchScalarGridSpec(
            num_scalar_prefetch=3,
            grid=(n_blocks,),
            in_specs=[pl.BlockSpec((bm, f), lambda i, be, nx, nu: (jnp.minimum(i, nu[0] - 1), 0)), anyspace],
            out_specs=pl.BlockSpec((bm, d), lambda i, be, nx, nu: (i, 0)),
            scratch_shapes=[pltpu.VMEM((1, f, d), F32), pltpu.VMEM((f, d), BF16),
                            pltpu.SemaphoreType.DMA((1,))],
        ),
        out_shape=jax.ShapeDtypeStruct((cap, d), F32),
        compiler_params=_cparams(("arbitrary",)),
        name="moe_down",
    )(blk_expert, blk_next, n_used, hidden, w_down)


def _moe_dispatch(eid, n_experts, bm):
    t = eid.shape[0]
    n_assign = t * TOP_K
    flat_e = eid.reshape(-1)
    onehot = (flat_e[:, None] == jnp.arange(n_experts, dtype=jnp.int32)[None, :]).astype(jnp.int32)
    csum = jnp.cumsum(onehot, axis=0)
    rank = jnp.sum(csum * onehot, axis=1) - 1
    counts = csum[-1]
    padded = (counts + bm - 1) // bm * bm
    pad_end = jnp.cumsum(padded)
    pad_start = pad_end - padded
    slot = (pad_start[flat_e] + rank).astype(jnp.int32)
    n_blocks = -(-n_assign // bm) + n_experts
    cap = n_blocks * bm
    tok_buf = jnp.zeros((cap,), jnp.int32).at[slot].set(jnp.arange(n_assign, dtype=jnp.int32) // TOP_K)
    blk_expert = jnp.minimum(
        jnp.searchsorted(pad_end, jnp.arange(n_blocks, dtype=jnp.int32) * bm, side="right"),
        n_experts - 1).astype(jnp.int32)
    n_used = (pad_end[-1] // bm).astype(jnp.int32).reshape(1)
    ids = jnp.arange(n_experts, dtype=jnp.int32)
    at_or_after = lax.cummin(jnp.where(counts > 0, ids, n_experts), axis=0, reverse=True)
    after = jnp.concatenate([at_or_after[1:], jnp.full((1,), n_experts, jnp.int32)])
    blk_next = jnp.where(after < n_experts, after, -1)[blk_expert].astype(jnp.int32)
    return slot, tok_buf, blk_expert, blk_next, n_used


def _hier_moe(x, eid, w_gate, w_up, w_down, layer):
    n_experts = w_gate.shape[1]
    slot, tok_buf, blk_expert, blk_next, n_used = _moe_dispatch(eid, n_experts, MOE_BM)
    return _moe_experts(x, tok_buf, blk_expert, blk_next, n_used, w_gate, w_up, w_down, layer), slot


def _s5_prepare(lam_re, lam_im, log_dt, b_re, b_im, c_re, c_im, d_skip, lc):
    g, p, c = b_re.shape
    dt = jnp.exp(log_dt.astype(F32))[:, None]
    lr, li = lam_re.astype(F32), lam_im.astype(F32)
    mag = jnp.exp(lr * dt)
    ab_re, ab_im = mag * jnp.cos(li * dt), mag * jnp.sin(li * dt)
    den = lr * lr + li * li
    q_re = ((ab_re - 1.0) * lr + ab_im * li) / den
    q_im = (ab_im * lr - (ab_re - 1.0) * li) / den
    bb_re = q_re[..., None] * b_re - q_im[..., None] * b_im
    bb_im = q_re[..., None] * b_im + q_im[..., None] * b_re
    ks = jnp.arange(lc + 1, dtype=F32)[:, None, None]
    pmag = jnp.exp(lr * dt * ks)
    pw_re, pw_im = pmag * jnp.cos(li * dt * ks), pmag * jnp.sin(li * dt * ks)
    t_re = pw_re[:lc, :, :, None] * bb_re[None] - pw_im[:lc, :, :, None] * bb_im[None]
    t_im = pw_re[:lc, :, :, None] * bb_im[None] + pw_im[:lc, :, :, None] * bb_re[None]
    kt = (jnp.einsum("gop,kgpi->gkoi", c_re, t_re, precision=HIGHEST)
          - jnp.einsum("gop,kgpi->gkoi", c_im, t_im, precision=HIGHEST))
    toe = jnp.stack([jnp.pad(kt[:, :lc - s], ((0, 0), (s, 0), (0, 0), (0, 0))) for s in range(lc)],
                    axis=1)
    m_intra = jnp.transpose(toe, (0, 1, 4, 2, 3)).reshape(g, lc * c, lc * c)
    rev = pw_re[:lc][::-1], pw_im[:lc][::-1]
    wst_re = rev[0][:, :, :, None] * bb_re[None] - rev[1][:, :, :, None] * bb_im[None]
    wst_im = rev[0][:, :, :, None] * bb_im[None] + rev[1][:, :, :, None] * bb_re[None]
    wst_re = jnp.transpose(wst_re, (1, 0, 3, 2)).reshape(g, lc * c, p)
    wst_im = jnp.transpose(wst_im, (1, 0, 3, 2)).reshape(g, lc * c, p)
    half = (jnp.arange(g) % 2)[:, None] == jnp.arange(2)[None, :]

    def place_cols(wm):
        return jnp.where(half[:, None, :, None], wm[:, :, None, :], 0.0).reshape(g, lc * c, 2 * p)

    wy_re = c_re[:, None] * pw_re[1:].transpose(1, 0, 2)[:, :, None, :] \
        - c_im[:, None] * pw_im[1:].transpose(1, 0, 2)[:, :, None, :]
    wy_im = -(c_re[:, None] * pw_im[1:].transpose(1, 0, 2)[:, :, None, :]
              + c_im[:, None] * pw_re[1:].transpose(1, 0, 2)[:, :, None, :])
    wy_re = jnp.transpose(wy_re, (0, 3, 1, 2)).reshape(g, p, lc * c)
    wy_im = jnp.transpose(wy_im, (0, 3, 1, 2)).reshape(g, p, lc * c)

    def place_rows(wm):
        return jnp.where(half[:, :, None, None], wm[:, None, :, :], 0.0).reshape(g, 2 * p, lc * c)

    al_re = pw_re[lc].reshape(g // 2, 1, 2 * p)
    al_im = pw_im[lc].reshape(g // 2, 1, 2 * p)
    d_ch = jnp.tile(d_skip.astype(F32).reshape(g, 1, c), (1, lc, 1)).reshape(g, 1, lc * c)
    return dict(m=m_intra.astype(BF16), wst_re=place_cols(wst_re).astype(BF16),
                wst_im=place_cols(wst_im).astype(BF16), wy_re=place_rows(wy_re).astype(BF16),
                wy_im=place_rows(wy_im).astype(BF16), al_re=al_re, al_im=al_im, d=d_ch)


def _block_transpose8(vs, blk):
    vs = list(vs)
    lane_blk = lax.broadcasted_iota(jnp.int32, vs[0].shape, 1) // blk
    for s in (4, 2, 1):
        low = (lane_blk & s) == 0
        for i in range(8):
            if i & s:
                continue
            a, b = vs[i], vs[i + s]
            vs[i] = jnp.where(low, a, pltpu.roll(b, s * blk, axis=1))
            vs[i + s] = jnp.where(low, pltpu.roll(a, LANES - s * blk, axis=1), b)
    return vs


def _rows_to_chunks(u_ref, uc_ref, lc):
    gb, rows, w = uc_ref.shape
    blk = w // lc

    def body(rb, carry):
        r0 = pl.multiple_of(rb * SUBLANES, SUBLANES)
        for half in range(lc // 8):
            vs = [u_ref[pl.ds(r0 * lc + half * 8 + l, SUBLANES, stride=lc), :] for l in range(8)]
            ws = _block_transpose8(vs, blk)
            for g in range(gb):
                uc_ref[g, pl.ds(r0, SUBLANES), half * LANES:(half + 1) * LANES] = ws[g]
        return carry

    lax.fori_loop(0, rows // SUBLANES, body, 0, unroll=S5_RELAYOUT_UNROLL)


def _chunks_to_rows(yc_ref, h_ref, lc):
    gb, rows, w = yc_ref.shape
    blk = w // lc

    def body(rb, carry):
        r0 = pl.multiple_of(rb * SUBLANES, SUBLANES)
        for half in range(lc // 8):
            ws = [yc_ref[g, pl.ds(r0, SUBLANES), half * LANES:(half + 1) * LANES] for g in range(gb)]
            vs = _block_transpose8(ws, blk)
            for l in range(8):
                h_ref[pl.ds(r0 * lc + half * 8 + l, SUBLANES, stride=lc), :] = vs[l]
        return carry

    lax.fori_loop(0, rows // SUBLANES, body, 0, unroll=S5_RELAYOUT_UNROLL)


def _s5_kernel(*refs, gb, nb, nk, lc, natural):
    (u_ref, m_ref, wsr_ref, wsi_ref, wyr_ref, wyi_ref, alr_ref, ali_ref, d_ref, s0r_ref, s0i_ref,
     h_ref, sfr_ref, sfi_ref, locr_ref, loci_ref, str_ref, sti_ref) = refs[:18]
    if natural:
        uc_ref, yc_ref = refs[18:]
        _rows_to_chunks(u_ref, uc_ref, lc)
    else:
        uc_ref, yc_ref = u_ref, h_ref
    for pair in range(gb // 2):
        g0, g1 = 2 * pair, 2 * pair + 1
        u0 = uc_ref[g0]
        u1 = uc_ref[g1]
        ub0 = u0.astype(BF16)
        ub1 = u1.astype(BF16)
        loc_re = _dot(ub0, wsr_ref[g0]) + _dot(ub1, wsr_ref[g1])
        loc_im = _dot(ub0, wsi_ref[g0]) + _dot(ub1, wsi_ref[g1])
        a_re = alr_ref[pair]
        a_im = ali_ref[pair]
        s_re = s0r_ref[pair]
        s_im = s0i_ref[pair]
        if nk == 1:
            st_re, st_im = s_re, s_im
            f_re = a_re * s_re - a_im * s_im + loc_re
            f_im = a_re * s_im + a_im * s_re + loc_im
        else:
            locr_ref[...] = loc_re
            loci_ref[...] = loc_im

            def step(k, carry):
                c_re, c_im = carry
                rows = pl.ds(k, nb, stride=nk)
                str_ref[rows, :] = c_re
                sti_ref[rows, :] = c_im
                l_re = locr_ref[rows, :]
                l_im = loci_ref[rows, :]
                return (a_re * c_re - a_im * c_im + l_re, a_re * c_im + a_im * c_re + l_im)

            f_re, f_im = lax.fori_loop(0, nk, step, (s_re, s_im))
            st_re = str_ref[...]
            st_im = sti_ref[...]
        sfr_ref[pair] = f_re
        sfi_ref[pair] = f_im
        sb_re = st_re.astype(BF16)
        sb_im = st_im.astype(BF16)
        for gi, u, ub in ((g0, u0, ub0), (g1, u1, ub1)):
            y = (_dot(ub, m_ref[gi]) + _dot(sb_re, wyr_ref[gi]) + _dot(sb_im, wyi_ref[gi])
                 + d_ref[gi] * u)
            yc_ref[gi] = jax.nn.gelu(y)
    if natural:
        _chunks_to_rows(yc_ref, h_ref, lc)


def _s5_mixer(u, prep, s0_re, s0_im, nb, nk, gb=S5_GB, natural_rows=None):
    g, w, _ = prep["m"].shape
    p2 = prep["al_re"].shape[2]
    r = nb * nk
    natural = natural_rows is not None
    grp = lambda *shape: pl.BlockSpec((gb,) + shape, lambda i: (i, 0, 0))
    pr = lambda *shape: pl.BlockSpec((gb // 2,) + shape, lambda i: (i, 0, 0))
    scratch = [pltpu.VMEM((r, p2), F32) for _ in range(4)]
    if natural:
        lc = w // S5_GROUP
        assert gb == 8 and gb * S5_GROUP == LANES and lc % 8 == 0 and r % SUBLANES == 0
        io_spec = pl.BlockSpec((r * lc, LANES), lambda i: (0, i))
        out_h = jax.ShapeDtypeStruct((natural_rows, g * S5_GROUP), F32)
        scratch += [pltpu.VMEM((gb, r, w), F32), pltpu.VMEM((gb, r, w), F32)]
    else:
        lc = 0
        io_spec = grp(r, w)
        out_h = jax.ShapeDtypeStruct((g, r, w), F32)
    kern = functools.partial(_s5_kernel, gb=gb, nb=nb, nk=nk, lc=lc, natural=natural)
    return pl.pallas_call(
        kern,
        grid=(g // gb,),
        in_specs=[io_spec, grp(w, w), grp(w, p2), grp(w, p2), grp(p2, w), grp(p2, w),
                  pr(1, p2), pr(1, p2), grp(1, w), pr(nb, p2), pr(nb, p2)],
        out_specs=[io_spec, pr(nb, p2), pr(nb, p2)],
        out_shape=[out_h,
                   jax.ShapeDtypeStruct((g // 2, nb, p2), F32),
                   jax.ShapeDtypeStruct((g // 2, nb, p2), F32)],
        scratch_shapes=scratch,
        compiler_params=_cparams(("parallel",)),
        name="s5_prompt" if natural else "s5_sample",
    )(u, prep["m"], prep["wst_re"], prep["wst_im"], prep["wy_re"], prep["wy_im"],
      prep["al_re"], prep["al_im"], prep["d"], s0_re, s0_im)


def _to_chunks(u2d, nb, nk, lc, g):
    c = u2d.shape[1] // g
    u5 = u2d.reshape(nb, nk, lc, g, c)
    return jnp.transpose(u5, (3, 0, 1, 2, 4)).reshape(g, nb * nk, lc * c)


def _from_chunks(hc, nb, nk, lc, g):
    c = hc.shape[2] // lc
    h5 = hc.reshape(g, nb, nk, lc, c)
    return jnp.transpose(h5, (1, 2, 3, 0, 4)).reshape(nb * nk * lc, g * c)


def _state_to_pairs(s):
    b, g, p = s.shape
    return jnp.transpose(s.reshape(b, g // 2, 2 * p), (1, 0, 2))


def _pairs_to_state(f):
    g2, b, p2 = f.shape
    return jnp.transpose(f, (1, 0, 2)).reshape(b, g2 * 2, p2 // 2)


def _softplus(x):
    return jnp.maximum(x, 0.0) + jnp.log1p(jnp.exp(-jnp.abs(x)))


def _ssd_kernel(*refs, prompt, q, n_heads, hd, has_prev):
    it = iter(refs)
    z_ref, xs_ref, bm_ref, cm_ref, dt_ref = (next(it) for _ in range(5))
    if prompt:
        hx_ref, hb_ref, hc_ref = (next(it) for _ in range(3))
    cwx_ref, cwb_ref, cwc_ref, cbx_ref, cbb_ref, cbc_ref = (next(it) for _ in range(6))
    dtb_ref, a_ref, dsk_ref, nw_ref, sel_ref, e_ref, et_ref = (next(it) for _ in range(7))
    if not prompt:
        sin_ref = next(it)
    if has_prev:
        next(it)
    y_ref, sout_ref = next(it), next(it)
    padx_ref, padb_ref, padc_ref, yacc_ref = (next(it) for _ in range(4))

    r = z_ref.shape[0]
    nseq = r // q
    hpg = xs_ref.shape[1] // hd
    row = lax.broadcasted_iota(jnp.int32, (r, 1), 0)
    if prompt:
        first = pl.program_id(2) == 0
        keep = jnp.where(first, 0.0, 1.0)
        live = None
    else:
        live = (row % q) >= (q // 2)

    def conv(x_ref, halo_ref, pad_ref, w_ref, b_ref):
        if prompt:
            pad_ref[0:SUBLANES, :] = halo_ref[...] * keep
        else:
            pad_ref[0:SUBLANES, :] = jnp.zeros((SUBLANES, pad_ref.shape[1]), F32)
        pad_ref[SUBLANES:SUBLANES + r, :] = x_ref[...]
        kk = w_ref.shape[0]
        acc = b_ref[...]
        for k in range(kk):
            acc = acc + w_ref[k:k + 1, :] * pad_ref[pl.ds(SUBLANES - (kk - 1) + k, r), :]
        return _silu(acc)

    xs = conv(xs_ref, hx_ref if prompt else None, padx_ref, cwx_ref, cbx_ref)
    bc = conv(bm_ref, hb_ref if prompt else None, padb_ref, cwb_ref, cbb_ref)
    cc = conv(cm_ref, hc_ref if prompt else None, padc_ref, cwc_ref, cbc_ref)

    lane = lax.broadcasted_iota(jnp.int32, (r, LANES), 1)
    dtv = jnp.where(lane < n_heads, _softplus(dt_ref[...] + dtb_ref[...]), 0.0)
    if not prompt:
        dtv = jnp.where(live, dtv, 0.0)
    sel = sel_ref[...]
    both = _dot_select_lhs(jnp.concatenate([dtv, dtv * a_ref[...]], axis=0), sel)
    dt8, adt8 = both[0:r], both[r:2 * r]

    ri = lax.broadcasted_iota(jnp.int32, (r, r), 0)
    ci = lax.broadcasted_iota(jnp.int32, (r, r), 1)
    same = (ri // q) == (ci // q)
    tri = same & (ci <= ri)
    masks = jnp.concatenate([jnp.where(tri, 1.0, 0.0), jnp.where(same, 1.0, 0.0)], axis=0).astype(BF16)
    sums = _dot_select_rhs(masks, adt8)
    acum, atot = sums[0:r], sums[r:2 * r]
    acum_t = acum.T
    expand = _dot_select_lhs(jnp.concatenate([jnp.exp(acum), dt8, jnp.exp(atot - acum)], axis=0), e_ref[...])
    eac_x, dt_x, dte_x = expand[0:r], expand[r:2 * r], expand[2 * r:3 * r]
    dec = _dot_select_rhs(et_ref[...], jnp.exp(atot).T)

    cb = _dot_nt(cc.astype(BF16), bc.astype(BF16))
    xdt = xs * dt_x
    neg = jnp.float32(-jnp.inf)
    lane_hd = lax.broadcasted_iota(jnp.int32, (r, 2 * hd), 1)
    for pair in range(hpg // 2):
        xp = xdt[:, pair * 2 * hd:(pair + 1) * 2 * hd]
        acc = None
        for half in range(2):
            hl = 2 * pair + half
            seg = acum[:, hl:hl + 1] - acum_t[hl:hl + 1, :]
            scores = cb * jnp.exp(jnp.where(tri, seg, neg))
            mask = (lane_hd < hd) if half == 0 else (lane_hd >= hd)
            part = _dot(scores.astype(BF16), jnp.where(mask, xp, 0.0).astype(BF16))
            acc = part if acc is None else acc + part
        yacc_ref[:, pair * 2 * hd:(pair + 1) * 2 * hd] = acc

    wgt = (xdt * dte_x).astype(BF16)
    bcb = bc.astype(BF16)
    ccb = cc.astype(BF16)
    if prompt:
        @pl.when(first)
        def _():
            sout_ref[...] = jnp.zeros(sout_ref.shape, F32)

        s_old = sout_ref[...]
        yacc_ref[...] += _dot_nt(ccb, s_old.astype(BF16)) * eac_x
        sout_ref[...] = dec[:, 0:1] * s_old + _dot_tn(wgt, bcb)
    else:
        for j in range(nseq):
            rows = slice(j * q, (j + 1) * q)
            s_old = sin_ref[j]
            yacc_ref[rows, :] += _dot_nt(ccb[rows], s_old.astype(BF16)) * eac_x[rows]
            wj = jnp.where((row // q) == j, wgt, jnp.zeros_like(wgt))
            sout_ref[j] = dec[:, j * q:j * q + 1] * s_old + _dot_tn(wj, bcb)

    y = yacc_ref[...] + dsk_ref[...] * xs
    y = y * _silu(z_ref[...])
    y = y * lax.rsqrt(jnp.mean(y * y, axis=-1, keepdims=True) + LN_EPS)
    y_ref[...] = (y * nw_ref[...]).astype(y_ref.dtype)


def _ssd_constants(n_heads, hd, n_groups):
    hpg = n_heads // n_groups
    lane = jnp.arange(LANES)
    sel = (lane[None, :, None] == (jnp.arange(n_groups)[:, None, None] * hpg + lane[None, None, :])) \
        & (lane[None, None, :] < hpg)
    e = (lane[:, None] == (jnp.arange(hpg * hd) // hd)[None, :])
    return sel.astype(BF16), e.astype(BF16), e.T.astype(BF16)


def _pad_lanes(v):
    return jnp.pad(v, ((0, 0), (0, LANES - v.shape[1])))[:, None, :]


def _ssd_params(conv_w, conv_b, dt_bias, a_log, d_skip, norm_w):
    n_heads = dt_bias.shape[1]
    hd = norm_w.shape[1] // n_heads
    return dict(
        ssd_conv_w=conv_w, ssd_conv_b3=conv_b[:, None, :],
        ssd_dt_bias3=_pad_lanes(dt_bias.astype(F32)),
        ssd_a3=_pad_lanes(-jnp.exp(a_log.astype(F32))),
        ssd_d3=jnp.repeat(d_skip.astype(F32), hd, axis=1)[:, None, :],
        ssd_norm3=norm_w[:, None, :],
        ssd_consts=_ssd_constants(n_heads, hd, SSD_GROUPS))


def _ssd_mixer(u, layer, prm, cols, *, prompt, nb, t, state_in=None, out_rows=None, state_prev=None):
    aliases = {}
    u_z, u_x, u_dt = u if isinstance(u, tuple) else (u, u, u)
    o_z, o_x, o_dt, w_b, n_st = cols
    n_groups = SSD_GROUPS
    hd = SSD_HEAD_DIM
    n_heads = w_b // hd
    gw = w_b // n_groups
    r = SSD_CHUNK
    sel, e, et = prm["ssd_consts"]
    o_b = o_x + w_b
    o_c = o_b + n_groups * n_st
    assert o_z % gw == 0 and o_x % gw == 0 and o_b % n_st == 0 and o_dt % LANES == 0 and n_st == LANES
    cw, cb = prm["ssd_conv_w"], prm["ssd_conv_b3"]

    if prompt:
        nc = t // r
        grid = (nb, n_groups, nc)
        rowblk = lambda b, g, c: b * nc + c
        halo = lambda b, g, c: jnp.maximum((b * t + c * r) // SUBLANES - 1, 0)
        im = lambda colf: (lambda b, g, c: (rowblk(b, g, c), colf(g)))
        hm = lambda colf: (lambda b, g, c: (halo(b, g, c), colf(g)))
        pm = lambda f: (lambda b, g, c: f(g))
        q = r
    else:
        q = SAMPLE_SEQ_ROWS
        grid = (nb * q // r, n_groups)
        im = lambda colf: (lambda i, g: (i, colf(g)))
        pm = lambda f: (lambda i, g: f(g))
    col_z = lambda g: o_z // gw + g
    col_x = lambda g: o_x // gw + g
    col_b = lambda g: o_b // n_st + g
    col_c = lambda g: o_c // n_st + g
    col_dt = lambda g: o_dt // LANES

    in_specs = [pl.BlockSpec((r, gw), im(col_z)), pl.BlockSpec((r, gw), im(col_x)),
                pl.BlockSpec((r, n_st), im(col_b)), pl.BlockSpec((r, n_st), im(col_c)),
                pl.BlockSpec((r, LANES), im(col_dt))]
    args = [u_z, u_x, u_x, u_x, u_dt]
    if prompt:
        in_specs += [pl.BlockSpec((SUBLANES, gw), hm(col_x)), pl.BlockSpec((SUBLANES, n_st), hm(col_b)),
                     pl.BlockSpec((SUBLANES, n_st), hm(col_c))]
        args += [u_x, u_x, u_x]
    kk = cw.shape[1]
    in_specs += [
        pl.BlockSpec((None, kk, gw), pm(lambda g: (layer, 0, g))),
        pl.BlockSpec((None, kk, n_st), pm(lambda g: (layer, 0, w_b // n_st + g))),
        pl.BlockSpec((None, kk, n_st), pm(lambda g: (layer, 0, w_b // n_st + n_groups + g))),
        pl.BlockSpec((None, 1, gw), pm(lambda g: (layer, 0, g))),
        pl.BlockSpec((None, 1, n_st), pm(lambda g: (layer, 0, w_b // n_st + g))),
        pl.BlockSpec((None, 1, n_st), pm(lambda g: (layer, 0, w_b // n_st + n_groups + g))),
        pl.BlockSpec((None, 1, LANES), pm(lambda g: (layer, 0, 0))),
        pl.BlockSpec((None, 1, LANES), pm(lambda g: (layer, 0, 0))),
        pl.BlockSpec((None, 1, gw), pm(lambda g: (layer, 0, g))),
        pl.BlockSpec((None, 1, gw), pm(lambda g: (layer, 0, g))),
        pl.BlockSpec((None, LANES, LANES), pm(lambda g: (g, 0, 0))),
        pl.BlockSpec((LANES, gw), pm(lambda g: (0, 0))),
        pl.BlockSpec((gw, LANES), pm(lambda g: (0, 0))),
    ]
    args += [cw, cw, cw, cb, cb, cb, prm["ssd_dt_bias3"], prm["ssd_a3"], prm["ssd_d3"], prm["ssd_norm3"],
             sel, e, et]
    if prompt:
        y_rows = out_rows
        out_specs = [pl.BlockSpec((r, gw), lambda b, g, c: (b * nc + c, g)),
                     pl.BlockSpec((None, gw, n_st), lambda b, g, c: (b, g, 0))]
        s_shape = (nb, w_b, n_st)
    else:
        nseq = r // q
        s_off = layer * (nb // nseq)
        in_specs.append(pl.BlockSpec((nseq, gw, n_st), lambda i, g: (i + s_off, g, 0)))
        args.append(state_in)
        y_rows = nb * q
        out_specs = [pl.BlockSpec((r, gw), lambda i, g: (i, g)),
                     pl.BlockSpec((nseq, gw, n_st), lambda i, g: (i + s_off, g, 0))]
        s_shape = state_in.shape
        if state_prev is not None:
            in_specs.append(pl.BlockSpec(memory_space=pl.ANY))
            args.append(state_prev)
            aliases = {len(args) - 1: 1}
    kern = functools.partial(_ssd_kernel, prompt=prompt, q=q, n_heads=n_heads, hd=hd,
                             has_prev=state_prev is not None)
    return pl.pallas_call(
        kern, grid=grid, in_specs=in_specs, out_specs=out_specs, input_output_aliases=aliases,
        out_shape=[jax.ShapeDtypeStruct((y_rows, w_b), BF16), jax.ShapeDtypeStruct(s_shape, F32)],
        scratch_shapes=[pltpu.VMEM((r + SUBLANES, gw), F32), pltpu.VMEM((r + SUBLANES, n_st), F32),
                        pltpu.VMEM((r + SUBLANES, n_st), F32), pltpu.VMEM((r, gw), F32)],
        compiler_params=_cparams(("parallel", "parallel", "arbitrary") if prompt else ("parallel", "parallel")),
        name="ssd_prompt" if prompt else "ssd_sample",
    )(*args)


def _conf_prompt_kernel(g_ref, halo_ref, w_ref, b_ref, lg_ref, lb_ref, out_ref, pad_ref, conv_ref, *, rt):
    tt, d = g_ref.shape
    halo = halo_ref.shape[0]
    kk = w_ref.shape[1]
    nlt = d // LANES
    keep = jnp.where(pl.program_id(1) == 0, 0.0, 1.0)
    for lt in range(nlt):
        cols = slice(lt * LANES, (lt + 1) * LANES)
        pad_ref[lt, 0:halo, :] = halo_ref[:, cols] * keep
        pad_ref[lt, halo:halo + tt, :] = g_ref[:, cols]
    base = halo - (kk - 1)

    def lane_tile(lt, carry):
        wt = w_ref[lt]
        bt = b_ref[lt]
        for r0 in range(0, tt, rt):
            acc = jnp.broadcast_to(bt, (rt, LANES))
            for k in range(kk):
                acc = acc + wt[k:k + 1, :] * pad_ref[lt, base + r0 + k:base + r0 + k + rt, :]
            conv_ref[lt, r0:r0 + rt, :] = acc
        return carry

    lax.fori_loop(0, nlt, lane_tile, 0)

    s1 = conv_ref[0]
    for lt in range(1, nlt):
        s1 = s1 + conv_ref[lt]
    mu = jnp.sum(s1, axis=-1, keepdims=True) * (1.0 / d)
    s2 = None
    for lt in range(nlt):
        dv = conv_ref[lt] - mu
        s2 = dv * dv if s2 is None else s2 + dv * dv
    rstd = lax.rsqrt(jnp.sum(s2, axis=-1, keepdims=True) * (1.0 / d) + LN_EPS)
    for lt in range(nlt):
        cols = slice(lt * LANES, (lt + 1) * LANES)
        v = (conv_ref[lt] - mu) * rstd * lg_ref[:, cols] + lb_ref[:, cols]
        out_ref[:, cols] = _silu(v).astype(out_ref.dtype)


def _conf_prompt(g, layer, dw_w, dw_b3, ln_g3, ln_b3, nb, t, out_rows, tt=CONF_TT, rt=CONF_RT, halo=CONF_HALO):
    d = g.shape[1]
    n_layers, kk, _ = dw_w.shape
    assert t % tt == 0 and tt % rt == 0 and halo >= kk - 1 and tt % halo == 0
    nt = t // tt
    nlt = d // LANES
    w_tiles = jnp.transpose(dw_w.reshape(n_layers, kk, nlt, LANES), (0, 2, 1, 3))
    b_tiles = dw_b3.reshape(n_layers, nlt, 1, LANES)
    par = pl.BlockSpec((None, 1, d), lambda b, i: (layer, 0, 0))
    return pl.pallas_call(
        functools.partial(_conf_prompt_kernel, rt=rt),
        grid=(nb, nt),
        in_specs=[pl.BlockSpec((tt, d), lambda b, i: (b * nt + i, 0)),
                  pl.BlockSpec((halo, d), lambda b, i: (jnp.maximum((b * t + i * tt) // halo - 1, 0), 0)),
                  pl.BlockSpec((None, nlt, kk, LANES), lambda b, i: (layer, 0, 0, 0)),
                  pl.BlockSpec((None, nlt, 1, LANES), lambda b, i: (layer, 0, 0, 0)), par, par],
        out_specs=pl.BlockSpec((tt, d), lambda b, i: (b * nt + i, 0)),
        out_shape=jax.ShapeDtypeStruct((out_rows, d), BF16),
        scratch_shapes=[pltpu.VMEM((d // LANES, halo + tt, LANES), F32), pltpu.VMEM((d // LANES, tt, LANES), F32)],
        compiler_params=_cparams(("parallel", "parallel")),
        name="conf_prompt",
    )(g, g, w_tiles, b_tiles, ln_g3, ln_b3)


def _conf_sample_kernel(ext_ref, w_ref, b_ref, lg_ref, lb_ref, out_ref):
    kk = w_ref.shape[0]
    t_new = out_ref.shape[0]
    for t in range(t_new):
        acc = b_ref[...] + w_ref[0:1, :] * ext_ref[t]
        for k in range(1, kk):
            acc = acc + w_ref[k:k + 1, :] * ext_ref[t + k]
        out_ref[t] = _silu(_ln_rows(acc, lg_ref[...], lb_ref[...])).astype(out_ref.dtype)


def _conf_sample(ext_t, layer, dw_w, dw_b3, ln_g3, ln_b3, nbt=16):
    rows, nb, d = ext_t.shape
    kk = dw_w.shape[1]
    t_new = rows - (kk - 1)
    par = pl.BlockSpec((None, 1, d), lambda i: (layer, 0, 0))
    return pl.pallas_call(
        _conf_sample_kernel,
        grid=(nb // nbt,),
        in_specs=[pl.BlockSpec((rows, nbt, d), lambda i: (0, i, 0)),
                  pl.BlockSpec((None, kk, d), lambda i: (layer, 0, 0)), par, par, par],
        out_specs=pl.BlockSpec((t_new, nbt, d), lambda i: (0, i, 0)),
        out_shape=jax.ShapeDtypeStruct((t_new, nb, d), BF16),
        compiler_params=_cparams(("parallel",)),
        name="conf_sample",
    )(ext_t, dw_w, dw_b3, ln_g3, ln_b3)


def kernel(x_prompt, x_sample, state_s5_re, state_s5_im, state_ssm, state_ssd_conv, state_conformer_conv, w_in_even, s5_lam_re, s5_lam_im, s5_log_dt, s5_b_re, s5_b_im, s5_c_re, s5_c_im, s5_d, s5_w_glu, s5_b_glu, ssd_conv_w, ssd_conv_b, ssd_dt_bias, ssd_a_log, ssd_d, ssd_norm_w, w_out_even, conf_w_pw1, conf_b_pw1, conf_dw_w, conf_dw_b, conf_ln_g, conf_ln_b, conf_w_pw2, conf_b_pw2, ln_g, ln_b, moe_w_group, moe_b_group, moe_w_expert, moe_b_expert, moe_w_gate, moe_w_up, moe_w_down):
    bsz, seq, d = x_prompt.shape
    nb_s, t_s, _ = x_sample.shape
    depth = ln_g.shape[0]
    alpha = (2.0 * depth) ** 0.25
    mp, ms = bsz * seq, nb_s * t_s
    x = jnp.concatenate([x_prompt.reshape(mp, d), x_sample.reshape(ms, d)], axis=0)
    xbf = x.astype(BF16)

    g_a = s5_lam_re.shape[1]
    w_a = s5_d.shape[1]
    w_b = ssd_norm_w.shape[1]
    n_heads = ssd_dt_bias.shape[1]
    conv_dim = ssd_conv_w.shape[2]
    n_st = (conv_dim - w_b) // (2 * SSD_GROUPS)
    o_z, o_x = w_a, w_a + w_b
    o_dt = o_x + conv_dim
    in_even = o_dt + n_heads
    cols = (o_z, o_x, o_dt, w_b, n_st)
    ssd_prm = _ssd_params(ssd_conv_w, ssd_conv_b, ssd_dt_bias, ssd_a_log, ssd_d, ssd_norm_w)
    ssm_in = state_ssm.reshape(state_ssm.shape[0] * nb_s, w_b, n_st)

    n_eg, per_g = moe_w_expert.shape[1], moe_w_expert.shape[3]
    n_route = n_eg + n_eg * per_g
    w_route = jnp.concatenate(
        [moe_w_group, jnp.transpose(moe_w_expert, (0, 2, 1, 3)).reshape(depth, d, n_eg * per_g),
         jnp.zeros((depth, d, LANES - n_route), F32)], axis=-1)
    b_route = jnp.concatenate(
        [moe_b_group, moe_b_expert.reshape(depth, n_eg * per_g), jnp.zeros((depth, LANES - n_route), F32)],
        axis=-1)[:, None, :]

    conf_dw_b3 = conf_dw_b[:, None, :]
    conf_ln_g3 = conf_ln_g[:, None, :]
    conf_ln_b3 = conf_ln_b[:, None, :]
    s5_zero = jnp.zeros((g_a // 2, bsz, 2 * s5_lam_re.shape[2]), F32)

    out = dict(re_p=[], im_p=[], ssm_p=[], sh_p=[], ch_p=[], re_s=[], im_s=[], sh_s=[], ch_s=[])
    ssm_s = None
    for layer in range(depth):
        i = layer // 2
        if layer % 2 == 0:
            u = _matmul([xbf], w_in_even, i, n_out=in_even)
            prm = (s5_lam_re[i], s5_lam_im[i], s5_log_dt[i], s5_b_re[i], s5_b_im[i], s5_c_re[i], s5_c_im[i],
                   s5_d[i])
            nk = seq // S5_CHUNK
            h, fr_p, fi_p = _s5_mixer(u, _s5_prepare(*prm, S5_CHUNK), s5_zero, s5_zero, bsz, nk,
                                      natural_rows=mp + ms)
            hc_s, fr_s, fi_s = _s5_mixer(_to_chunks(u[mp:, :w_a], nb_s, 1, t_s, g_a),
                                         _s5_prepare(*prm, t_s), _state_to_pairs(state_s5_re[i]),
                                         _state_to_pairs(state_s5_im[i]), nb_s, 1)
            h = _put_rows(h, _from_chunks(hc_s, nb_s, 1, t_s, g_a), mp)
            ya = _matmul([h], s5_w_glu, i, bias=s5_b_glu, mode="gate", gate=h, out_dtype=BF16)
            out["re_p"].append(_pairs_to_state(fr_p)); out["im_p"].append(_pairs_to_state(fi_p))
            out["re_s"].append(_pairs_to_state(fr_s)); out["im_s"].append(_pairs_to_state(fi_s))
            yb, ssm_p = _ssd_mixer(u, i, ssd_prm, cols, prompt=True, nb=bsz, t=seq, out_rows=mp + ms)
            us3 = u[mp:].reshape(nb_s, t_s, in_even)
            dead = SAMPLE_SEQ_ROWS - t_s
            hist = state_ssd_conv[i]
            rows8 = lambda parts: jnp.concatenate(parts, axis=1).reshape(nb_s * SAMPLE_SEQ_ROWS, -1)
            ext_z = rows8([jnp.zeros((nb_s, dead, w_b), F32), us3[:, :, o_z:o_z + w_b]])
            ext_x = rows8([jnp.zeros((nb_s, dead - hist.shape[1], conv_dim), F32), hist,
                           us3[:, :, o_x:o_x + conv_dim]])
            ext_dt = rows8([jnp.zeros((nb_s, dead, LANES), F32),
                            jnp.pad(us3[:, :, o_dt:], ((0, 0), (0, 0), (0, LANES - n_heads)))])
            yb_s, ssm_s = _ssd_mixer((ext_z, ext_x, ext_dt), i, ssd_prm, (0, 0, 0, w_b, n_st), prompt=False,
                                     nb=nb_s, t=t_s, state_in=ssm_in, state_prev=ssm_s)
            yb_s = yb_s.reshape(nb_s, SAMPLE_SEQ_ROWS, w_b)[:, dead:].reshape(ms, w_b)
            yb = _put_rows(yb, yb_s, mp)
            mix = _matmul([ya, yb], w_out_even, i)
            kh = hist.shape[1]
            out["ssm_p"].append(ssm_p.reshape(bsz, n_heads, SSD_HEAD_DIM, n_st))
            out["sh_p"].append(jnp.stack([u[(b + 1) * seq - kh:(b + 1) * seq, o_x:o_x + conv_dim]
                                          for b in range(bsz)]))
            out["sh_s"].append(jnp.concatenate([hist, us3[:, :, o_x:o_x + conv_dim]], axis=1)[:, t_s:])
        else:
            gg = _matmul([xbf], conf_w_pw1, i, bias=conf_b_pw1, mode="glu")
            c = _conf_prompt(gg, i, conf_dw_w, conf_dw_b3, conf_ln_g3, conf_ln_b3, bsz, seq, mp + ms)
            hist = state_conformer_conv[i]
            g_s = gg[mp:].reshape(nb_s, t_s, d)
            ext = jnp.concatenate([hist, g_s], axis=1)
            c_s = _conf_sample(jnp.transpose(ext, (1, 0, 2)), i, conf_dw_w, conf_dw_b3, conf_ln_g3, conf_ln_b3)
            c = _put_rows(c, jnp.transpose(c_s, (1, 0, 2)).reshape(ms, d), mp)
            mix = _matmul([c], conf_w_pw2, i, bias=conf_b_pw2)
            kh = hist.shape[1]
            out["ch_p"].append(jnp.stack([gg[(b + 1) * seq - kh:(b + 1) * seq] for b in range(bsz)]))
            out["ch_s"].append(ext[:, t_s:])
        x, xbf, eid, gates = _ln_residual(x, mix, ln_g, ln_b, layer, 0, alpha,
                                          router=(w_route[layer], b_route[layer], n_eg, per_g))
        yb_moe, slots = _hier_moe(x, eid[:, :TOP_K], moe_w_gate, moe_w_up, moe_w_down, layer)
        x, xbf = _ln_combine(x, yb_moe, slots, gates, ln_g, ln_b, layer, 1, alpha)

    st = lambda k: jnp.stack(out[k])
    return (x[:mp].reshape(bsz, seq, d), x[mp:].reshape(nb_s, t_s, d),
            st("re_p"), st("im_p"), st("ssm_p"), st("sh_p"), st("ch_p"),
            st("re_s"), st("im_s"), ssm_s.reshape(state_ssm.shape), st("sh_s"), st("ch_s"))
```

```python
import functools
import math

import jax
import jax.numpy as jnp
from jax import lax
from jax.experimental import pallas as pl
from jax.experimental.pallas import tpu as pltpu

F32 = jnp.float32
BF16 = jnp.bfloat16
HIGHEST = lax.Precision.HIGHEST

LN_EPS = 1e-5
V7X_VMEM_LIMIT_BYTES = 56 * 1024 * 1024
LANES = 128
SUBLANES = 8

S5_GROUP = 16
SSD_HEAD_DIM = 64
SSD_GROUPS = 4
SSD_CHUNK = 128
TOP_K = 2

MM_TM = 512
MM_TN = 512
LN_TM = 256
MOE_BM = 256
MOE_CAST_ROWS = 256
MOE_ROW_BUFS = 3
S5_CHUNK = 16
S5_GB = 8
S5_RELAYOUT_UNROLL = 4
CONF_TT = 256
CONF_RT = 32
CONF_HALO = 32
SAMPLE_SEQ_ROWS = 8


def _cparams(sem):
    return pltpu.CompilerParams(dimension_semantics=sem, vmem_limit_bytes=V7X_VMEM_LIMIT_BYTES)


def _dot(a, b):
    return jnp.dot(a, b, preferred_element_type=F32)


def _split3(a):
    hi = a.astype(BF16)
    r1 = a - hi.astype(F32)
    mid = r1.astype(BF16)
    lo = (r1 - mid.astype(F32)).astype(BF16)
    return hi, mid, lo


def _dot_select_lhs(a, onehot):
    hi, mid, lo = _split3(a)
    return (_dot(hi, onehot) + _dot(mid, onehot)) + _dot(lo, onehot)


def _dot_select_rhs(onehot, b):
    hi, mid, lo = _split3(b)
    return (_dot(onehot, hi) + _dot(onehot, mid)) + _dot(onehot, lo)


def _dot_nt(a, b):
    return lax.dot_general(a, b, (((1,), (1,)), ((), ())), preferred_element_type=F32)


def _dot_tn(a, b):
    return lax.dot_general(a, b, (((0,), (0,)), ((), ())), preferred_element_type=F32)


def _pack_bf16_pairs(v):
    half = v.shape[1] // 2
    lo = lax.bitcast_convert_type(v[:, :half].astype(BF16).astype(F32), jnp.uint32)
    hi = lax.bitcast_convert_type(v[:, half:].astype(BF16).astype(F32), jnp.uint32)
    return hi | (lo >> 16)


def _unpack_bf16_pairs(w):
    lo = lax.bitcast_convert_type(w << 16, F32)
    hi = lax.bitcast_convert_type(w & jnp.uint32(0xFFFF0000), F32)
    return lo, hi


def _sigmoid(x):
    return 1.0 / (1.0 + jnp.exp(-x))


def _silu(x):
    return x * _sigmoid(x)


def _mm_kernel(*refs, n_lhs, n_col, has_bias, mode):
    pos = 0
    x_refs = refs[pos:pos + n_lhs]; pos += n_lhs
    w_refs = refs[pos:pos + n_lhs * n_col]; pos += n_lhs * n_col
    b_refs = ()
    if has_bias:
        b_refs = refs[pos:pos + n_col]; pos += n_col
    gate_ref = None
    if mode == "gate":
        gate_ref = refs[pos]; pos += 1
    out_ref = refs[pos]; pos += 1
    wbf_refs = refs[pos:]

    @pl.when(pl.program_id(1) == 0)
    def _():
        for w_ref, wbf_ref in zip(w_refs, wbf_refs):
            wbf_ref[...] = w_ref[...].astype(BF16)

    accs = []
    for c in range(n_col):
        acc = None
        for l in range(n_lhs):
            part = _dot(x_refs[l][...].astype(BF16), wbf_refs[l * n_col + c][...])
            acc = part if acc is None else acc + part
        if has_bias:
            acc = acc + b_refs[c][...]
        accs.append(acc)
    if mode == "glu":
        res = accs[0] * _sigmoid(accs[1])
    elif mode == "gate":
        res = gate_ref[...] * _sigmoid(accs[0])
    else:
        res = accs[0]
    out_ref[...] = res.astype(out_ref.dtype)


def _matmul(xs, w, layer, bias=None, mode="plain", gate=None, n_out=None, out_dtype=F32,
            tm=MM_TM, tn=MM_TN):
    n_lhs = len(xs)
    m, k = xs[0].shape
    n_total = w.shape[2]
    n_col = 2 if mode == "glu" else 1
    if n_out is None:
        n_out = n_total // n_col
    assert m % tm == 0 and w.shape[1] == n_lhs * k
    nj = pl.cdiv(n_out, tn)
    glu_off = (n_total // 2) // tn if mode == "glu" else 0
    if mode == "glu":
        assert (n_total // 2) % tn == 0

    in_specs = [pl.BlockSpec((tm, k), lambda j, i: (i, 0)) for _ in range(n_lhs)]
    args = list(xs)
    for l in range(n_lhs):
        for c in range(n_col):
            in_specs.append(pl.BlockSpec((None, k, tn),
                                         functools.partial(lambda j, i, l, c: (layer, l, j + c * glu_off), l=l, c=c)))
            args.append(w)
    if bias is not None:
        b3 = bias.reshape(bias.shape[0], 1, bias.shape[1])
        for c in range(n_col):
            in_specs.append(pl.BlockSpec((None, 1, tn),
                                         functools.partial(lambda j, i, c: (layer, 0, j + c * glu_off), c=c)))
            args.append(b3)
    if mode == "gate":
        in_specs.append(pl.BlockSpec((tm, tn), lambda j, i: (i, j)))
        args.append(gate)
    kern = functools.partial(_mm_kernel, n_lhs=n_lhs, n_col=n_col, has_bias=bias is not None, mode=mode)
    return pl.pallas_call(
        kern,
        grid=(nj, m // tm),
        in_specs=in_specs,
        out_specs=pl.BlockSpec((tm, tn), lambda j, i: (i, j)),
        out_shape=jax.ShapeDtypeStruct((m, n_out), out_dtype),
        scratch_shapes=[pltpu.VMEM((k, tn), BF16) for _ in range(n_lhs * n_col)],
        compiler_params=_cparams(("arbitrary", "arbitrary")),
        name=f"mm_{mode}",
    )(*args)


def _put_rows_kernel(small_ref, big_ref, out_ref):
    del big_ref
    out_ref[...] = small_ref[...]


def _put_rows(big, small, row0):
    n, w = small.shape
    assert row0 % n == 0 and big.shape[1] == w and big.dtype == small.dtype
    return pl.pallas_call(
        _put_rows_kernel, grid=(1,),
        in_specs=[pl.BlockSpec((n, w), lambda i: (0, 0)), pl.BlockSpec(memory_space=pl.ANY)],
        out_specs=pl.BlockSpec((n, w), lambda i: (row0 // n, 0)),
        out_shape=jax.ShapeDtypeStruct(big.shape, big.dtype),
        input_output_aliases={1: 0}, compiler_params=_cparams(("arbitrary",)), name="put_rows",
    )(small, big)


def _ln_rows(y, g, b):
    mu = jnp.mean(y, axis=-1, keepdims=True)
    yc = y - mu
    var = jnp.mean(yc * yc, axis=-1, keepdims=True)
    return yc * lax.rsqrt(var + LN_EPS) * g + b


def _first_argmax(v, lane, big):
    m = jnp.max(v, axis=-1, keepdims=True)
    idx = jnp.min(jnp.where(v == m, lane, big), axis=-1, keepdims=True)
    return m, idx


def _ln_router_kernel(x_ref, mix_ref, g_ref, b_ref, wrh_ref, wrl_ref, br_ref,
                      out_ref, outpk_ref, eid_ref, gate_ref, *, alpha, n_groups, per_group):
    y = alpha * x_ref[...] + mix_ref[...]
    out = _ln_rows(y, g_ref[...], b_ref[...])
    out_ref[...] = out
    outpk_ref[...] = _pack_bf16_pairs(out)
    out_hi = out.astype(BF16)
    out_lo = (out - out_hi.astype(F32)).astype(BF16)
    wr_hi = wrh_ref[...]
    lg = _dot(out_hi, wr_hi) + (_dot(out_lo, wr_hi) + _dot(out_hi, wrl_ref[...])) + br_ref[...]
    lane = lax.broadcasted_iota(jnp.int32, lg.shape, 1)
    neg = jnp.float32(-jnp.inf)
    big = jnp.int32(LANES)
    gl = jnp.where(lane < n_groups, lg, neg)
    gmax, grp = _first_argmax(gl, lane, big)
    pg_top = 1.0 / jnp.sum(jnp.exp(gl - gmax), axis=-1, keepdims=True)
    lo = n_groups + grp * per_group
    el = jnp.where((lane >= lo) & (lane < lo + per_group), lg, neg)
    m1, i1 = _first_argmax(el, lane, big)
    el2 = jnp.where(lane == i1, neg, el)
    m2, i2 = _first_argmax(el2, lane, big)
    e2 = jnp.exp(m2 - m1)
    g1 = pg_top / (1.0 + e2)
    g2 = pg_top * e2 / (1.0 + e2)
    eid_ref[...] = jnp.where(lane == 0, i1 - n_groups, jnp.where(lane == 1, i2 - n_groups, 0))
    gate_ref[...] = jnp.where(lane == 0, g1, jnp.where(lane == 1, g2, 0.0))


def _ln_router(x, mix, ln_g, ln_b, layer, which, alpha, wr, br, n_groups, per_group, tm=LN_TM):
    m, d = x.shape
    assert m % tm == 0
    row = pl.BlockSpec((tm, d), lambda i: (i, 0))
    par = pl.BlockSpec((None, None, 1, d), lambda i: (layer, which, 0, 0))
    small = pl.BlockSpec((tm, LANES), lambda i: (i, 0))
    wspec = pl.BlockSpec((d, LANES), lambda i: (0, 0))
    g4 = ln_g.reshape(ln_g.shape[0], ln_g.shape[1], 1, d)
    b4 = ln_b.reshape(ln_b.shape[0], ln_b.shape[1], 1, d)
    wr_hi = wr.astype(BF16)
    wr_lo = (wr - wr_hi.astype(F32)).astype(BF16)
    kern = functools.partial(_ln_router_kernel, alpha=alpha, n_groups=n_groups, per_group=per_group)
    return pl.pallas_call(
        kern, grid=(m // tm,),
        in_specs=[row, row, par, par, wspec, wspec, pl.BlockSpec((1, LANES), lambda i: (0, 0))],
        out_specs=[row, pl.BlockSpec((tm, d // 2), lambda i: (i, 0)), small, small],
        out_shape=[jax.ShapeDtypeStruct((m, d), F32), jax.ShapeDtypeStruct((m, d // 2), jnp.uint32),
                   jax.ShapeDtypeStruct((m, LANES), jnp.int32), jax.ShapeDtypeStruct((m, LANES), F32)],
        compiler_params=_cparams(("parallel",)), name="ln_router",
    )(x, mix, g4, b4, wr_hi, wr_lo, br)


def _pair_row_copy(y_hbm, buf_ref, sem_ref, slot, k, src_row, dst_row, n_rows):
    return pltpu.make_async_copy(y_hbm.at[pl.ds(src_row, n_rows), :],
                                 buf_ref.at[slot, k, pl.ds(dst_row, n_rows), :], sem_ref.at[slot])


def _ln_combine_kernel(slot_ref, x_ref, gate_ref, y_hbm, g_ref, b_ref, out_ref, outbf_ref, buf_ref, sem_ref, *,
                       alpha):
    i = pl.program_id(0)
    tm = x_ref.shape[0]
    cur = i % 2

    def issue(tile, dst):
        base = tile * (tm * TOP_K)

        def body(r, carry):
            for k in range(TOP_K):
                _pair_row_copy(y_hbm, buf_ref, sem_ref, dst, k, slot_ref[base + TOP_K * r + k], r, 1).start()
            return carry

        lax.fori_loop(0, tm, body, 0, unroll=4)

    @pl.when(i == 0)
    def _():
        issue(0, 0)

    @pl.when(i + 1 < pl.num_programs(0))
    def _():
        base = (i + 1) * (tm * TOP_K)
        for r in range(tm):
            for k in range(TOP_K):
                _pair_row_copy(y_hbm, buf_ref, sem_ref, 1 - cur, k, slot_ref[base + TOP_K * r + k], r, 1).start()

    for k in range(TOP_K):
        _pair_row_copy(y_hbm, buf_ref, sem_ref, cur, k, 0, 0, tm).wait()
    gates = gate_ref[...]
    lo0, hi0 = _unpack_bf16_pairs(buf_ref[cur, 0])
    lo1, hi1 = _unpack_bf16_pairs(buf_ref[cur, 1])
    g0, g1 = gates[:, 0:1], gates[:, 1:2]
    y = alpha * x_ref[...] + jnp.concatenate([lo0 * g0 + lo1 * g1, hi0 * g0 + hi1 * g1], axis=1)
    out = _ln_rows(y, g_ref[...], b_ref[...])
    out_ref[...] = out
    outbf_ref[...] = out.astype(BF16)


def _ln_combine(x, yb, slot_flat, gates, ln_g, ln_b, layer, which, alpha, tm=LN_TM):
    m, d = x.shape
    assert m % tm == 0 and TOP_K == 2
    row = pl.BlockSpec((tm, d), lambda i, s: (i, 0))
    gate_spec = pl.BlockSpec((tm, LANES), lambda i, s: (i, 0))
    par = pl.BlockSpec((None, None, 1, d), lambda i, s: (layer, which, 0, 0))
    g4 = ln_g.reshape(ln_g.shape[0], ln_g.shape[1], 1, d)
    b4 = ln_b.reshape(ln_b.shape[0], ln_b.shape[1], 1, d)
    return pl.pallas_call(
        functools.partial(_ln_combine_kernel, alpha=alpha),
        grid_spec=pltpu.PrefetchScalarGridSpec(
            num_scalar_prefetch=1, grid=(m // tm,),
            in_specs=[row, gate_spec, pl.BlockSpec(memory_space=pl.ANY), par, par],
            out_specs=[row, row],
            scratch_shapes=[pltpu.VMEM((2, TOP_K, tm, d // 2), jnp.uint32), pltpu.SemaphoreType.DMA((2,))]),
        out_shape=[jax.ShapeDtypeStruct((m, d), F32), jax.ShapeDtypeStruct((m, d), BF16)],
        compiler_params=pltpu.CompilerParams(dimension_semantics=("arbitrary",),
                                             vmem_limit_bytes=V7X_VMEM_LIMIT_BYTES,
                                             disable_bounds_checks=True),
        name="ln_combine",
    )(slot_flat, x, gates, yb, g4, b4)


def _moe_row_copy(x_hbm, xbuf_ref, sem_ref, slot, src_row, dst_row, n_rows):
    return pltpu.make_async_copy(x_hbm.at[pl.ds(src_row, n_rows), :],
                                 xbuf_ref.at[slot, pl.ds(dst_row, n_rows), :], sem_ref.at[slot])


def _expert_weight_stream(i, n_used, be_ref, nxt_ref, w_hbms, wbuf_ref, wsem_ref, layer, on_arrival):
    def copies(expert):
        return [pltpu.make_async_copy(w.at[layer, expert], wbuf_ref.at[k], wsem_ref.at[0])
                for k, w in enumerate(w_hbms)]

    @pl.when(i == 0)
    def _():
        for c in copies(be_ref[0]):
            c.start()

    prev = be_ref[jnp.maximum(i - 1, 0)]

    @pl.when((i == 0) | ((i < n_used) & (be_ref[i] != prev)))
    def _():
        for c in copies(be_ref[i]):
            c.wait()
        on_arrival()
        nxt = nxt_ref[i]

        @pl.when(nxt >= 0)
        def _():
            for c in copies(nxt):
                c.start()


def _moe_up_kernel(be_ref, nxt_ref, nu_ref, tok_ref, x_hbm, wg_hbm, wu_hbm, h_ref,
                   wbuf_ref, wgbf_ref, wubf_ref, xbuf_ref, sem_ref, wsem_ref, *, layer):
    i = pl.program_id(0)
    n_used = nu_ref[0]
    n_buf, bm = xbuf_ref.shape[0], xbuf_ref.shape[1]
    depth = n_buf - 1
    slot = lax.rem(i, n_buf)

    def start_row(blk, r):
        _moe_row_copy(x_hbm, xbuf_ref, sem_ref, lax.rem(blk, n_buf), tok_ref[blk * bm + r], r, 1).start()

    @pl.when(i == 0)
    def _():
        def body(k, carry):
            start_row(k // bm, lax.rem(k, bm))
            return carry

        lax.fori_loop(0, jnp.minimum(depth, n_used) * bm, body, 0)

    @pl.when(i + depth < n_used)
    def _():
        for r in range(bm):
            start_row(i + depth, r)

    def cast_weights():
        rows = wgbf_ref.shape[0]
        for r0 in range(0, rows, MOE_CAST_ROWS):
            sl = slice(r0, r0 + MOE_CAST_ROWS)
            wgbf_ref[sl, :] = wbuf_ref[0, sl, :].astype(BF16)
            wubf_ref[sl, :] = wbuf_ref[1, sl, :].astype(BF16)

    _expert_weight_stream(i, n_used, be_ref, nxt_ref, (wg_hbm, wu_hbm), wbuf_ref, wsem_ref, layer, cast_weights)

    @pl.when(i < n_used)
    def _():
        _moe_row_copy(x_hbm, xbuf_ref, sem_ref, slot, 0, 0, bm).wait()
        x = jnp.concatenate(_unpack_bf16_pairs(xbuf_ref[slot]), axis=1).astype(BF16)
        h = _silu(_dot(x, wgbf_ref[...])) * _dot(x, wubf_ref[...])
        h_ref[...] = h.astype(BF16)

    @pl.when(i >= n_used)
    def _():
        h_ref[...] = jnp.zeros(h_ref.shape, BF16)


def _moe_down_kernel(be_ref, nxt_ref, nu_ref, h_ref, wd_hbm, out_ref, wbuf_ref, wdbf_ref, wsem_ref, *, layer):
    i = pl.program_id(0)

    def cast_weights():
        wdbf_ref[...] = wbuf_ref[0].astype(BF16)

    _expert_weight_stream(i, nu_ref[0], be_ref, nxt_ref, (wd_hbm,), wbuf_ref, wsem_ref, layer, cast_weights)

    @pl.when(i < nu_ref[0])
    def _():
        out_ref[...] = _pack_bf16_pairs(_dot(h_ref[...], wdbf_ref[...]))

    @pl.when(i >= nu_ref[0])
    def _():
        out_ref[...] = jnp.zeros(out_ref.shape, jnp.uint32)


def _moe_experts(xpk, tok_buf, blk_expert, blk_next, n_used, w_gate, w_up, w_down, layer, bm=MOE_BM):
    cap = tok_buf.shape[0]
    d = w_gate.shape[2]
    f = w_gate.shape[3]
    n_blocks = cap // bm
    anyspace = pl.BlockSpec(memory_space=pl.ANY)
    gather_params = pltpu.CompilerParams(dimension_semantics=("arbitrary",),
                                         vmem_limit_bytes=V7X_VMEM_LIMIT_BYTES, disable_bounds_checks=True)
    hidden = pl.pallas_call(
        functools.partial(_moe_up_kernel, layer=layer),
        grid_spec=pltpu.PrefetchScalarGridSpec(
            num_scalar_prefetch=4,
            grid=(n_blocks,),
            in_specs=[anyspace, anyspace, anyspace],
            out_specs=pl.BlockSpec((bm, f), lambda i, be, nx, nu, tok: (i, 0)),
            scratch_shapes=[pltpu.VMEM((2, d, f), F32), pltpu.VMEM((d, f), BF16), pltpu.VMEM((d, f), BF16),
                            pltpu.VMEM((MOE_ROW_BUFS, bm, d // 2), jnp.uint32),
                            pltpu.SemaphoreType.DMA((MOE_ROW_BUFS,)), pltpu.SemaphoreType.DMA((1,))],
        ),
        out_shape=jax.ShapeDtypeStruct((cap, f), BF16),
        compiler_params=gather_params,
        name="moe_up",
    )(blk_expert, blk_next, n_used, tok_buf, xpk, w_gate, w_up)
    return pl.pallas_call(
        functools.partial(_moe_down_kernel, layer=layer),
        grid_spec=pltpu.PrefetchScalarGridSpec(
            num_scalar_prefetch=3,
            grid=(n_blocks,),
            in_specs=[pl.BlockSpec((bm, f), lambda i, be, nx, nu: (jnp.minimum(i, nu[0] - 1), 0)), anyspace],
            out_specs=pl.BlockSpec((bm, d // 2), lambda i, be, nx, nu: (i, 0)),
            scratch_shapes=[pltpu.VMEM((1, f, d), F32), pltpu.VMEM((f, d), BF16),
                            pltpu.SemaphoreType.DMA((1,))],
        ),
        out_shape=jax.ShapeDtypeStruct((cap, d // 2), jnp.uint32),
        compiler_params=_cparams(("arbitrary",)),
        name="moe_down",
    )(blk_expert, blk_next, n_used, hidden, w_down)


def _moe_dispatch(eid, n_experts, bm):
    t = eid.shape[0]
    n_assign = t * TOP_K
    flat_e = eid.reshape(-1)
    onehot = (flat_e[:, None] == jnp.arange(n_experts, dtype=jnp.int32)[None, :]).astype(jnp.int32)
    csum = jnp.cumsum(onehot, axis=0)
    rank = jnp.sum(csum * onehot, axis=1) - 1
    counts = csum[-1]
    padded = (counts + bm - 1) // bm * bm
    pad_end = jnp.cumsum(padded)
    pad_start = pad_end - padded
    slot = (pad_start[flat_e] + rank).astype(jnp.int32)
    n_blocks = -(-n_assign // bm) + n_experts
    cap = n_blocks * bm
    tok_buf = jnp.zeros((cap,), jnp.int32).at[slot].set(jnp.arange(n_assign, dtype=jnp.int32) // TOP_K)
    blk_expert = jnp.minimum(
        jnp.searchsorted(pad_end, jnp.arange(n_blocks, dtype=jnp.int32) * bm, side="right"),
        n_experts - 1).astype(jnp.int32)
    n_used = (pad_end[-1] // bm).astype(jnp.int32).reshape(1)
    ids = jnp.arange(n_experts, dtype=jnp.int32)
    at_or_after = lax.cummin(jnp.where(counts > 0, ids, n_experts), axis=0, reverse=True)
    after = jnp.concatenate([at_or_after[1:], jnp.full((1,), n_experts, jnp.int32)])
    blk_next = jnp.where(after < n_experts, after, -1)[blk_expert].astype(jnp.int32)
    return slot, tok_buf, blk_expert, blk_next, n_used


def _hier_moe(x, eid, w_gate, w_up, w_down, layer):
    n_experts = w_gate.shape[1]
    slot, tok_buf, blk_expert, blk_next, n_used = _moe_dispatch(eid, n_experts, MOE_BM)
    return _moe_experts(x, tok_buf, blk_expert, blk_next, n_used, w_gate, w_up, w_down, layer), slot


def _s5_prepare(lam_re, lam_im, log_dt, b_re, b_im, c_re, c_im, d_skip, lc):
    g, p, c = b_re.shape
    dt = jnp.exp(log_dt.astype(F32))[:, None]
    lr, li = lam_re.astype(F32), lam_im.astype(F32)
    mag = jnp.exp(lr * dt)
    ab_re, ab_im = mag * jnp.cos(li * dt), mag * jnp.sin(li * dt)
    den = lr * lr + li * li
    q_re = ((ab_re - 1.0) * lr + ab_im * li) / den
    q_im = (ab_im * lr - (ab_re - 1.0) * li) / den
    bb_re = q_re[..., None] * b_re - q_im[..., None] * b_im
    bb_im = q_re[..., None] * b_im + q_im[..., None] * b_re
    ks = jnp.arange(lc + 1, dtype=F32)[:, None, None]
    pmag = jnp.exp(lr * dt * ks)
    pw_re, pw_im = pmag * jnp.cos(li * dt * ks), pmag * jnp.sin(li * dt * ks)
    t_re = pw_re[:lc, :, :, None] * bb_re[None] - pw_im[:lc, :, :, None] * bb_im[None]
    t_im = pw_re[:lc, :, :, None] * bb_im[None] + pw_im[:lc, :, :, None] * bb_re[None]
    kt = (jnp.einsum("gop,kgpi->gkoi", c_re, t_re, precision=HIGHEST)
          - jnp.einsum("gop,kgpi->gkoi", c_im, t_im, precision=HIGHEST))
    toe = jnp.stack([jnp.pad(kt[:, :lc - s], ((0, 0), (s, 0), (0, 0), (0, 0))) for s in range(lc)],
                    axis=1)
    m_intra = jnp.transpose(toe, (0, 1, 4, 2, 3)).reshape(g, lc * c, lc * c)
    rev = pw_re[:lc][::-1], pw_im[:lc][::-1]
    wst_re = rev[0][:, :, :, None] * bb_re[None] - rev[1][:, :, :, None] * bb_im[None]
    wst_im = rev[0][:, :, :, None] * bb_im[None] + rev[1][:, :, :, None] * bb_re[None]
    wst_re = jnp.transpose(wst_re, (1, 0, 3, 2)).reshape(g, lc * c, p)
    wst_im = jnp.transpose(wst_im, (1, 0, 3, 2)).reshape(g, lc * c, p)
    half = (jnp.arange(g) % 2)[:, None] == jnp.arange(2)[None, :]

    def place_cols(wm):
        return jnp.where(half[:, None, :, None], wm[:, :, None, :], 0.0).reshape(g, lc * c, 2 * p)

    wy_re = c_re[:, None] * pw_re[1:].transpose(1, 0, 2)[:, :, None, :] \
        - c_im[:, None] * pw_im[1:].transpose(1, 0, 2)[:, :, None, :]
    wy_im = -(c_re[:, None] * pw_im[1:].transpose(1, 0, 2)[:, :, None, :]
              + c_im[:, None] * pw_re[1:].transpose(1, 0, 2)[:, :, None, :])
    wy_re = jnp.transpose(wy_re, (0, 3, 1, 2)).reshape(g, p, lc * c)
    wy_im = jnp.transpose(wy_im, (0, 3, 1, 2)).reshape(g, p, lc * c)

    def place_rows(wm):
        return jnp.where(half[:, :, None, None], wm[:, None, :, :], 0.0).reshape(g, 2 * p, lc * c)

    al_re = pw_re[lc].reshape(g // 2, 1, 2 * p)
    al_im = pw_im[lc].reshape(g // 2, 1, 2 * p)
    d_ch = jnp.tile(d_skip.astype(F32).reshape(g, 1, c), (1, lc, 1)).reshape(g, 1, lc * c)
    return dict(m=m_intra.astype(BF16), wst_re=place_cols(wst_re).astype(BF16),
                wst_im=place_cols(wst_im).astype(BF16), wy_re=place_rows(wy_re).astype(BF16),
                wy_im=place_rows(wy_im).astype(BF16), al_re=al_re, al_im=al_im, d=d_ch)


def _block_transpose8(vs, blk):
    vs = list(vs)
    lane_blk = lax.broadcasted_iota(jnp.int32, vs[0].shape, 1) // blk
    for s in (4, 2, 1):
        low = (lane_blk & s) == 0
        for i in range(8):
            if i & s:
                continue
            a, b = vs[i], vs[i + s]
            vs[i] = jnp.where(low, a, pltpu.roll(b, s * blk, axis=1))
            vs[i + s] = jnp.where(low, pltpu.roll(a, LANES - s * blk, axis=1), b)
    return vs


def _rows_to_chunks(u_ref, uc_ref, lc):
    gb, rows, w = uc_ref.shape
    blk = w // lc

    def body(rb, carry):
        r0 = pl.multiple_of(rb * SUBLANES, SUBLANES)
        for half in range(lc // 8):
            vs = [u_ref[pl.ds(r0 * lc + half * 8 + l, SUBLANES, stride=lc), :] for l in range(8)]
            ws = _block_transpose8(vs, blk)
            for g in range(gb):
                uc_ref[g, pl.ds(r0, SUBLANES), half * LANES:(half + 1) * LANES] = ws[g]
        return carry

    lax.fori_loop(0, rows // SUBLANES, body, 0, unroll=S5_RELAYOUT_UNROLL)


def _chunks_to_rows(yc_ref, h_ref, lc):
    gb, rows, w = yc_ref.shape
    blk = w // lc

    def body(rb, carry):
        r0 = pl.multiple_of(rb * SUBLANES, SUBLANES)
        for half in range(lc // 8):
            ws = [yc_ref[g, pl.ds(r0, SUBLANES), half * LANES:(half + 1) * LANES] for g in range(gb)]
            vs = _block_transpose8(ws, blk)
            for l in range(8):
                h_ref[pl.ds(r0 * lc + half * 8 + l, SUBLANES, stride=lc), :] = vs[l]
        return carry

    lax.fori_loop(0, rows // SUBLANES, body, 0, unroll=S5_RELAYOUT_UNROLL)


def _s5_kernel(*refs, gb, nb, nk, lc, natural):
    (u_ref, m_ref, wsr_ref, wsi_ref, wyr_ref, wyi_ref, alr_ref, ali_ref, d_ref, s0r_ref, s0i_ref,
     h_ref, sfr_ref, sfi_ref, locr_ref, loci_ref, str_ref, sti_ref) = refs[:18]
    if natural:
        uc_ref, yc_ref = refs[18:]
        _rows_to_chunks(u_ref, uc_ref, lc)
    else:
        uc_ref, yc_ref = u_ref, h_ref
    for pair in range(gb // 2):
        g0, g1 = 2 * pair, 2 * pair + 1
        u0 = uc_ref[g0]
        u1 = uc_ref[g1]
        ub0 = u0.astype(BF16)
        ub1 = u1.astype(BF16)
        loc_re = _dot(ub0, wsr_ref[g0]) + _dot(ub1, wsr_ref[g1])
        loc_im = _dot(ub0, wsi_ref[g0]) + _dot(ub1, wsi_ref[g1])
        a_re = alr_ref[pair]
        a_im = ali_ref[pair]
        s_re = s0r_ref[pair]
        s_im = s0i_ref[pair]
        if nk == 1:
            st_re, st_im = s_re, s_im
            f_re = a_re * s_re - a_im * s_im + loc_re
            f_im = a_re * s_im + a_im * s_re + loc_im
        else:
            locr_ref[...] = loc_re
            loci_ref[...] = loc_im

            def step(k, carry):
                c_re, c_im = carry
                rows = pl.ds(k, nb, stride=nk)
                str_ref[rows, :] = c_re
                sti_ref[rows, :] = c_im
                l_re = locr_ref[rows, :]
                l_im = loci_ref[rows, :]
                return (a_re * c_re - a_im * c_im + l_re, a_re * c_im + a_im * c_re + l_im)

            f_re, f_im = lax.fori_loop(0, nk, step, (s_re, s_im))
            st_re = str_ref[...]
            st_im = sti_ref[...]
        sfr_ref[pair] = f_re
        sfi_ref[pair] = f_im
        sb_re = st_re.astype(BF16)
        sb_im = st_im.astype(BF16)
        for gi, u, ub in ((g0, u0, ub0), (g1, u1, ub1)):
            y = (_dot(ub, m_ref[gi]) + _dot(sb_re, wyr_ref[gi]) + _dot(sb_im, wyi_ref[gi])
                 + d_ref[gi] * u)
            yc_ref[gi] = jax.nn.gelu(y)
    if natural:
        _chunks_to_rows(yc_ref, h_ref, lc)


def _s5_mixer(u, prep, s0_re, s0_im, nb, nk, gb=S5_GB, natural_rows=None):
    g, w, _ = prep["m"].shape
    p2 = prep["al_re"].shape[2]
    r = nb * nk
    natural = natural_rows is not None
    grp = lambda *shape: pl.BlockSpec((gb,) + shape, lambda i: (i, 0, 0))
    pr = lambda *shape: pl.BlockSpec((gb // 2,) + shape, lambda i: (i, 0, 0))
    scratch = [pltpu.VMEM((r, p2), F32) for _ in range(4)]
    if natural:
        lc = w // S5_GROUP
        assert gb == 8 and gb * S5_GROUP == LANES and lc % 8 == 0 and r % SUBLANES == 0
        io_spec = pl.BlockSpec((r * lc, LANES), lambda i: (0, i))
        out_h = jax.ShapeDtypeStruct((natural_rows, g * S5_GROUP), F32)
        scratch += [pltpu.VMEM((gb, r, w), F32), pltpu.VMEM((gb, r, w), F32)]
    else:
        lc = 0
        io_spec = grp(r, w)
        out_h = jax.ShapeDtypeStruct((g, r, w), F32)
    kern = functools.partial(_s5_kernel, gb=gb, nb=nb, nk=nk, lc=lc, natural=natural)
    return pl.pallas_call(
        kern,
        grid=(g // gb,),
        in_specs=[io_spec, grp(w, w), grp(w, p2), grp(w, p2), grp(p2, w), grp(p2, w),
                  pr(1, p2), pr(1, p2), grp(1, w), pr(nb, p2), pr(nb, p2)],
        out_specs=[io_spec, pr(nb, p2), pr(nb, p2)],
        out_shape=[out_h,
                   jax.ShapeDtypeStruct((g // 2, nb, p2), F32),
                   jax.ShapeDtypeStruct((g // 2, nb, p2), F32)],
        scratch_shapes=scratch,
        compiler_params=_cparams(("parallel",)),
        name="s5_prompt" if natural else "s5_sample",
    )(u, prep["m"], prep["wst_re"], prep["wst_im"], prep["wy_re"], prep["wy_im"],
      prep["al_re"], prep["al_im"], prep["d"], s0_re, s0_im)


def _to_chunks(u2d, nb, nk, lc, g):
    c = u2d.shape[1] // g
    u5 = u2d.reshape(nb, nk, lc, g, c)
    return jnp.transpose(u5, (3, 0, 1, 2, 4)).reshape(g, nb * nk, lc * c)


def _from_chunks(hc, nb, nk, lc, g):
    c = hc.shape[2] // lc
    h5 = hc.reshape(g, nb, nk, lc, c)
    return jnp.transpose(h5, (1, 2, 3, 0, 4)).reshape(nb * nk * lc, g * c)


def _state_to_pairs(s):
    b, g, p = s.shape
    return jnp.transpose(s.reshape(b, g // 2, 2 * p), (1, 0, 2))


def _pairs_to_state(f):
    g2, b, p2 = f.shape
    return jnp.transpose(f, (1, 0, 2)).reshape(b, g2 * 2, p2 // 2)


def _softplus(x):
    return jnp.maximum(x, 0.0) + jnp.log1p(jnp.exp(-jnp.abs(x)))


def _ssd_kernel(*refs, prompt, q, n_heads, hd, has_prev):
    it = iter(refs)
    z_ref, xs_ref, bm_ref, cm_ref, dt_ref = (next(it) for _ in range(5))
    if prompt:
        hx_ref, hb_ref, hc_ref = (next(it) for _ in range(3))
    cwx_ref, cwb_ref, cwc_ref, cbx_ref, cbb_ref, cbc_ref = (next(it) for _ in range(6))
    dtb_ref, a_ref, dsk_ref, nw_ref, sel_ref, e_ref, et_ref = (next(it) for _ in range(7))
    if not prompt:
        sin_ref = next(it)
    if has_prev:
        next(it)
    y_ref, sout_ref = next(it), next(it)
    padx_ref, padb_ref, padc_ref, yacc_ref = (next(it) for _ in range(4))

    r = z_ref.shape[0]
    nseq = r // q
    hpg = xs_ref.shape[1] // hd
    row = lax.broadcasted_iota(jnp.int32, (r, 1), 0)
    if prompt:
        first = pl.program_id(2) == 0
        keep = jnp.where(first, 0.0, 1.0)
        live = None
    else:
        live = (row % q) >= (q // 2)

    def conv(x_ref, halo_ref, pad_ref, w_ref, b_ref):
        if prompt:
            pad_ref[0:SUBLANES, :] = halo_ref[...] * keep
        else:
            pad_ref[0:SUBLANES, :] = jnp.zeros((SUBLANES, pad_ref.shape[1]), F32)
        pad_ref[SUBLANES:SUBLANES + r, :] = x_ref[...]
        kk = w_ref.shape[0]
        acc = b_ref[...]
        for k in range(kk):
            acc = acc + w_ref[k:k + 1, :] * pad_ref[pl.ds(SUBLANES - (kk - 1) + k, r), :]
        return _silu(acc)

    xs = conv(xs_ref, hx_ref if prompt else None, padx_ref, cwx_ref, cbx_ref)
    bc = conv(bm_ref, hb_ref if prompt else None, padb_ref, cwb_ref, cbb_ref)
    cc = conv(cm_ref, hc_ref if prompt else None, padc_ref, cwc_ref, cbc_ref)

    lane = lax.broadcasted_iota(jnp.int32, (r, LANES), 1)
    dtv = jnp.where(lane < n_heads, _softplus(dt_ref[...] + dtb_ref[...]), 0.0)
    if not prompt:
        dtv = jnp.where(live, dtv, 0.0)
    sel = sel_ref[...]
    both = _dot_select_lhs(jnp.concatenate([dtv, dtv * a_ref[...]], axis=0), sel)
    dt8, adt8 = both[0:r], both[r:2 * r]

    ri = lax.broadcasted_iota(jnp.int32, (r, r), 0)
    ci = lax.broadcasted_iota(jnp.int32, (r, r), 1)
    same = (ri // q) == (ci // q)
    tri = same & (ci <= ri)
    masks = jnp.concatenate([jnp.where(tri, 1.0, 0.0), jnp.where(same, 1.0, 0.0)], axis=0).astype(BF16)
    sums = _dot_select_rhs(masks, adt8)
    acum, atot = sums[0:r], sums[r:2 * r]
    acum_t = acum.T
    expand = _dot_select_lhs(jnp.concatenate([jnp.exp(acum), dt8, jnp.exp(atot - acum)], axis=0), e_ref[...])
    eac_x, dt_x, dte_x = expand[0:r], expand[r:2 * r], expand[2 * r:3 * r]
    dec = _dot_select_rhs(et_ref[...], jnp.exp(atot).T)

    cb = _dot_nt(cc.astype(BF16), bc.astype(BF16))
    xdt = xs * dt_x
    neg = jnp.float32(-jnp.inf)
    lane_hd = lax.broadcasted_iota(jnp.int32, (r, 2 * hd), 1)
    for pair in range(hpg // 2):
        xp = xdt[:, pair * 2 * hd:(pair + 1) * 2 * hd]
        acc = None
        for half in range(2):
            hl = 2 * pair + half
            seg = acum[:, hl:hl + 1] - acum_t[hl:hl + 1, :]
            scores = cb * jnp.exp(jnp.where(tri, seg, neg))
            mask = (lane_hd < hd) if half == 0 else (lane_hd >= hd)
            part = _dot(scores.astype(BF16), jnp.where(mask, xp, 0.0).astype(BF16))
            acc = part if acc is None else acc + part
        yacc_ref[:, pair * 2 * hd:(pair + 1) * 2 * hd] = acc

    wgt = (xdt * dte_x).astype(BF16)
    bcb = bc.astype(BF16)
    ccb = cc.astype(BF16)
    if prompt:
        @pl.when(first)
        def _():
            sout_ref[...] = jnp.zeros(sout_ref.shape, F32)

        s_old = sout_ref[...]
        yacc_ref[...] += _dot_nt(ccb, s_old.astype(BF16)) * eac_x
        sout_ref[...] = dec[:, 0:1] * s_old + _dot_tn(wgt, bcb)
    else:
        for j in range(nseq):
            rows = slice(j * q, (j + 1) * q)
            s_old = sin_ref[j]
            yacc_ref[rows, :] += _dot_nt(ccb[rows], s_old.astype(BF16)) * eac_x[rows]
            wj = jnp.where((row // q) == j, wgt, jnp.zeros_like(wgt))
            sout_ref[j] = dec[:, j * q:j * q + 1] * s_old + _dot_tn(wj, bcb)

    y = yacc_ref[...] + dsk_ref[...] * xs
    y = y * _silu(z_ref[...])
    y = y * lax.rsqrt(jnp.mean(y * y, axis=-1, keepdims=True) + LN_EPS)
    y_ref[...] = (y * nw_ref[...]).astype(y_ref.dtype)


def _ssd_constants(n_heads, hd, n_groups):
    hpg = n_heads // n_groups
    lane = jnp.arange(LANES)
    sel = (lane[None, :, None] == (jnp.arange(n_groups)[:, None, None] * hpg + lane[None, None, :])) \
        & (lane[None, None, :] < hpg)
    e = (lane[:, None] == (jnp.arange(hpg * hd) // hd)[None, :])
    return sel.astype(BF16), e.astype(BF16), e.T.astype(BF16)


def _pad_lanes(v):
    return jnp.pad(v, ((0, 0), (0, LANES - v.shape[1])))[:, None, :]


def _ssd_params(conv_w, conv_b, dt_bias, a_log, d_skip, norm_w):
    n_heads = dt_bias.shape[1]
    hd = norm_w.shape[1] // n_heads
    return dict(
        ssd_conv_w=conv_w, ssd_conv_b3=conv_b[:, None, :],
        ssd_dt_bias3=_pad_lanes(dt_bias.astype(F32)),
        ssd_a3=_pad_lanes(-jnp.exp(a_log.astype(F32))),
        ssd_d3=jnp.repeat(d_skip.astype(F32), hd, axis=1)[:, None, :],
        ssd_norm3=norm_w[:, None, :],
        ssd_consts=_ssd_constants(n_heads, hd, SSD_GROUPS))


def _ssd_mixer(u, layer, prm, cols, *, prompt, nb, t, state_in=None, out_rows=None, state_prev=None):
    aliases = {}
    u_z, u_x, u_dt = u if isinstance(u, tuple) else (u, u, u)
    o_z, o_x, o_dt, w_b, n_st = cols
    n_groups = SSD_GROUPS
    hd = SSD_HEAD_DIM
    n_heads = w_b // hd
    gw = w_b // n_groups
    r = SSD_CHUNK
    sel, e, et = prm["ssd_consts"]
    o_b = o_x + w_b
    o_c = o_b + n_groups * n_st
    assert o_z % gw == 0 and o_x % gw == 0 and o_b % n_st == 0 and o_dt % LANES == 0 and n_st == LANES
    cw, cb = prm["ssd_conv_w"], prm["ssd_conv_b3"]

    if prompt:
        nc = t // r
        grid = (nb, n_groups, nc)
        rowblk = lambda b, g, c: b * nc + c
        halo = lambda b, g, c: jnp.maximum((b * t + c * r) // SUBLANES - 1, 0)
        im = lambda colf: (lambda b, g, c: (rowblk(b, g, c), colf(g)))
        hm = lambda colf: (lambda b, g, c: (halo(b, g, c), colf(g)))
        pm = lambda f: (lambda b, g, c: f(g))
        q = r
    else:
        q = SAMPLE_SEQ_ROWS
        grid = (nb * q // r, n_groups)
        im = lambda colf: (lambda i, g: (i, colf(g)))
        pm = lambda f: (lambda i, g: f(g))
    col_z = lambda g: o_z // gw + g
    col_x = lambda g: o_x // gw + g
    col_b = lambda g: o_b // n_st + g
    col_c = lambda g: o_c // n_st + g
    col_dt = lambda g: o_dt // LANES

    in_specs = [pl.BlockSpec((r, gw), im(col_z)), pl.BlockSpec((r, gw), im(col_x)),
                pl.BlockSpec((r, n_st), im(col_b)), pl.BlockSpec((r, n_st), im(col_c)),
                pl.BlockSpec((r, LANES), im(col_dt))]
    args = [u_z, u_x, u_x, u_x, u_dt]
    if prompt:
        in_specs += [pl.BlockSpec((SUBLANES, gw), hm(col_x)), pl.BlockSpec((SUBLANES, n_st), hm(col_b)),
                     pl.BlockSpec((SUBLANES, n_st), hm(col_c))]
        args += [u_x, u_x, u_x]
    kk = cw.shape[1]
    in_specs += [
        pl.BlockSpec((None, kk, gw), pm(lambda g: (layer, 0, g))),
        pl.BlockSpec((None, kk, n_st), pm(lambda g: (layer, 0, w_b // n_st + g))),
        pl.BlockSpec((None, kk, n_st), pm(lambda g: (layer, 0, w_b // n_st + n_groups + g))),
        pl.BlockSpec((None, 1, gw), pm(lambda g: (layer, 0, g))),
        pl.BlockSpec((None, 1, n_st), pm(lambda g: (layer, 0, w_b // n_st + g))),
        pl.BlockSpec((None, 1, n_st), pm(lambda g: (layer, 0, w_b // n_st + n_groups + g))),
        pl.BlockSpec((None, 1, LANES), pm(lambda g: (layer, 0, 0))),
        pl.BlockSpec((None, 1, LANES), pm(lambda g: (layer, 0, 0))),
        pl.BlockSpec((None, 1, gw), pm(lambda g: (layer, 0, g))),
        pl.BlockSpec((None, 1, gw), pm(lambda g: (layer, 0, g))),
        pl.BlockSpec((None, LANES, LANES), pm(lambda g: (g, 0, 0))),
        pl.BlockSpec((LANES, gw), pm(lambda g: (0, 0))),
        pl.BlockSpec((gw, LANES), pm(lambda g: (0, 0))),
    ]
    args += [cw, cw, cw, cb, cb, cb, prm["ssd_dt_bias3"], prm["ssd_a3"], prm["ssd_d3"], prm["ssd_norm3"],
             sel, e, et]
    if prompt:
        y_rows = out_rows
        out_specs = [pl.BlockSpec((r, gw), lambda b, g, c: (b * nc + c, g)),
                     pl.BlockSpec((None, gw, n_st), lambda b, g, c: (b, g, 0))]
        s_shape = (nb, w_b, n_st)
    else:
        nseq = r // q
        s_off = layer * (nb // nseq)
        in_specs.append(pl.BlockSpec((nseq, gw, n_st), lambda i, g: (i + s_off, g, 0)))
        args.append(state_in)
        y_rows = nb * q
        out_specs = [pl.BlockSpec((r, gw), lambda i, g: (i, g)),
                     pl.BlockSpec((nseq, gw, n_st), lambda i, g: (i + s_off, g, 0))]
        s_shape = state_in.shape
        if state_prev is not None:
            in_specs.append(pl.BlockSpec(memory_space=pl.ANY))
            args.append(state_prev)
            aliases = {len(args) - 1: 1}
    kern = functools.partial(_ssd_kernel, prompt=prompt, q=q, n_heads=n_heads, hd=hd,
                             has_prev=state_prev is not None)
    return pl.pallas_call(
        kern, grid=grid, in_specs=in_specs, out_specs=out_specs, input_output_aliases=aliases,
        out_shape=[jax.ShapeDtypeStruct((y_rows, w_b), BF16), jax.ShapeDtypeStruct(s_shape, F32)],
        scratch_shapes=[pltpu.VMEM((r + SUBLANES, gw), F32), pltpu.VMEM((r + SUBLANES, n_st), F32),
                        pltpu.VMEM((r + SUBLANES, n_st), F32), pltpu.VMEM((r, gw), F32)],
        compiler_params=_cparams(("parallel", "parallel", "arbitrary") if prompt else ("parallel", "parallel")),
        name="ssd_prompt" if prompt else "ssd_sample",
    )(*args)


def _conf_prompt_kernel(g_ref, halo_ref, w_ref, b_ref, lg_ref, lb_ref, out_ref, pad_ref, conv_ref, *, rt):
    tt, d = g_ref.shape
    halo = halo_ref.shape[0]
    kk = w_ref.shape[1]
    nlt = d // LANES
    keep = jnp.where(pl.program_id(1) == 0, 0.0, 1.0)
    for lt in range(nlt):
        cols = slice(lt * LANES, (lt + 1) * LANES)
        pad_ref[lt, 0:halo, :] = halo_ref[:, cols] * keep
        pad_ref[lt, halo:halo + tt, :] = g_ref[:, cols]
    base = halo - (kk - 1)

    def lane_tile(lt, carry):
        wt = w_ref[lt]
        bt = b_ref[lt]
        for r0 in range(0, tt, rt):
            acc = jnp.broadcast_to(bt, (rt, LANES))
            for k in range(kk):
                acc = acc + wt[k:k + 1, :] * pad_ref[lt, base + r0 + k:base + r0 + k + rt, :]
            conv_ref[lt, r0:r0 + rt, :] = acc
        return carry

    lax.fori_loop(0, nlt, lane_tile, 0)

    s1 = conv_ref[0]
    for lt in range(1, nlt):
        s1 = s1 + conv_ref[lt]
    mu = jnp.sum(s1, axis=-1, keepdims=True) * (1.0 / d)
    s2 = None
    for lt in range(nlt):
        dv = conv_ref[lt] - mu
        s2 = dv * dv if s2 is None else s2 + dv * dv
    rstd = lax.rsqrt(jnp.sum(s2, axis=-1, keepdims=True) * (1.0 / d) + LN_EPS)
    for lt in range(nlt):
        cols = slice(lt * LANES, (lt + 1) * LANES)
        v = (conv_ref[lt] - mu) * rstd * lg_ref[:, cols] + lb_ref[:, cols]
        out_ref[:, cols] = _silu(v).astype(out_ref.dtype)


def _conf_prompt(g, layer, dw_w, dw_b3, ln_g3, ln_b3, nb, t, out_rows, tt=CONF_TT, rt=CONF_RT, halo=CONF_HALO):
    d = g.shape[1]
    n_layers, kk, _ = dw_w.shape
    assert t % tt == 0 and tt % rt == 0 and halo >= kk - 1 and tt % halo == 0
    nt = t // tt
    nlt = d // LANES
    w_tiles = jnp.transpose(dw_w.reshape(n_layers, kk, nlt, LANES), (0, 2, 1, 3))
    b_tiles = dw_b3.reshape(n_layers, nlt, 1, LANES)
    par = pl.BlockSpec((None, 1, d), lambda b, i: (layer, 0, 0))
    return pl.pallas_call(
        functools.partial(_conf_prompt_kernel, rt=rt),
        grid=(nb, nt),
        in_specs=[pl.BlockSpec((tt, d), lambda b, i: (b * nt + i, 0)),
                  pl.BlockSpec((halo, d), lambda b, i: (jnp.maximum((b * t + i * tt) // halo - 1, 0), 0)),
                  pl.BlockSpec((None, nlt, kk, LANES), lambda b, i: (layer, 0, 0, 0)),
                  pl.BlockSpec((None, nlt, 1, LANES), lambda b, i: (layer, 0, 0, 0)), par, par],
        out_specs=pl.BlockSpec((tt, d), lambda b, i: (b * nt + i, 0)),
        out_shape=jax.ShapeDtypeStruct((out_rows, d), BF16),
        scratch_shapes=[pltpu.VMEM((d // LANES, halo + tt, LANES), F32), pltpu.VMEM((d // LANES, tt, LANES), F32)],
        compiler_params=_cparams(("parallel", "parallel")),
        name="conf_prompt",
    )(g, g, w_tiles, b_tiles, ln_g3, ln_b3)


def _conf_sample_kernel(ext_ref, w_ref, b_ref, lg_ref, lb_ref, out_ref):
    kk = w_ref.shape[0]
    t_new = out_ref.shape[0]
    for t in range(t_new):
        acc = b_ref[...] + w_ref[0:1, :] * ext_ref[t]
        for k in range(1, kk):
            acc = acc + w_ref[k:k + 1, :] * ext_ref[t + k]
        out_ref[t] = _silu(_ln_rows(acc, lg_ref[...], lb_ref[...])).astype(out_ref.dtype)


def _conf_sample(ext_t, layer, dw_w, dw_b3, ln_g3, ln_b3, nbt=16):
    rows, nb, d = ext_t.shape
    kk = dw_w.shape[1]
    t_new = rows - (kk - 1)
    par = pl.BlockSpec((None, 1, d), lambda i: (layer, 0, 0))
    return pl.pallas_call(
        _conf_sample_kernel,
        grid=(nb // nbt,),
        in_specs=[pl.BlockSpec((rows, nbt, d), lambda i: (0, i, 0)),
                  pl.BlockSpec((None, kk, d), lambda i: (layer, 0, 0)), par, par, par],
        out_specs=pl.BlockSpec((t_new, nbt, d), lambda i: (0, i, 0)),
        out_shape=jax.ShapeDtypeStruct((t_new, nb, d), BF16),
        compiler_params=_cparams(("parallel",)),
        name="conf_sample",
    )(ext_t, dw_w, dw_b3, ln_g3, ln_b3)


def kernel(x_prompt, x_sample, state_s5_re, state_s5_im, state_ssm, state_ssd_conv, state_conformer_conv, w_in_even, s5_lam_re, s5_lam_im, s5_log_dt, s5_b_re, s5_b_im, s5_c_re, s5_c_im, s5_d, s5_w_glu, s5_b_glu, ssd_conv_w, ssd_conv_b, ssd_dt_bias, ssd_a_log, ssd_d, ssd_norm_w, w_out_even, conf_w_pw1, conf_b_pw1, conf_dw_w, conf_dw_b, conf_ln_g, conf_ln_b, conf_w_pw2, conf_b_pw2, ln_g, ln_b, moe_w_group, moe_b_group, moe_w_expert, moe_b_expert, moe_w_gate, moe_w_up, moe_w_down):
    bsz, seq, d = x_prompt.shape
    nb_s, t_s, _ = x_sample.shape
    depth = ln_g.shape[0]
    alpha = (2.0 * depth) ** 0.25
    mp, ms = bsz * seq, nb_s * t_s
    x = jnp.concatenate([x_prompt.reshape(mp, d), x_sample.reshape(ms, d)], axis=0)
    xbf = x.astype(BF16)

    g_a = s5_lam_re.shape[1]
    w_a = s5_d.shape[1]
    w_b = ssd_norm_w.shape[1]
    n_heads = ssd_dt_bias.shape[1]
    conv_dim = ssd_conv_w.shape[2]
    n_st = (conv_dim - w_b) // (2 * SSD_GROUPS)
    o_z, o_x = w_a, w_a + w_b
    o_dt = o_x + conv_dim
    in_even = o_dt + n_heads
    cols = (o_z, o_x, o_dt, w_b, n_st)
    ssd_prm = _ssd_params(ssd_conv_w, ssd_conv_b, ssd_dt_bias, ssd_a_log, ssd_d, ssd_norm_w)
    ssm_in = state_ssm.reshape(state_ssm.shape[0] * nb_s, w_b, n_st)

    n_eg, per_g = moe_w_expert.shape[1], moe_w_expert.shape[3]
    n_route = n_eg + n_eg * per_g
    w_route = jnp.concatenate(
        [moe_w_group, jnp.transpose(moe_w_expert, (0, 2, 1, 3)).reshape(depth, d, n_eg * per_g),
         jnp.zeros((depth, d, LANES - n_route), F32)], axis=-1)
    b_route = jnp.concatenate(
        [moe_b_group, moe_b_expert.reshape(depth, n_eg * per_g), jnp.zeros((depth, LANES - n_route), F32)],
        axis=-1)[:, None, :]

    conf_dw_b3 = conf_dw_b[:, None, :]
    conf_ln_g3 = conf_ln_g[:, None, :]
    conf_ln_b3 = conf_ln_b[:, None, :]
    s5_zero = jnp.zeros((g_a // 2, bsz, 2 * s5_lam_re.shape[2]), F32)

    out = dict(re_p=[], im_p=[], ssm_p=[], sh_p=[], ch_p=[], re_s=[], im_s=[], sh_s=[], ch_s=[])
    ssm_s = None
    for layer in range(depth):
        i = layer // 2
        if layer % 2 == 0:
            u = _matmul([xbf], w_in_even, i, n_out=in_even)
            prm = (s5_lam_re[i], s5_lam_im[i], s5_log_dt[i], s5_b_re[i], s5_b_im[i], s5_c_re[i], s5_c_im[i],
                   s5_d[i])
            nk = seq // S5_CHUNK
            h, fr_p, fi_p = _s5_mixer(u, _s5_prepare(*prm, S5_CHUNK), s5_zero, s5_zero, bsz, nk,
                                      natural_rows=mp + ms)
            hc_s, fr_s, fi_s = _s5_mixer(_to_chunks(u[mp:, :w_a], nb_s, 1, t_s, g_a),
                                         _s5_prepare(*prm, t_s), _state_to_pairs(state_s5_re[i]),
                                         _state_to_pairs(state_s5_im[i]), nb_s, 1)
            h = _put_rows(h, _from_chunks(hc_s, nb_s, 1, t_s, g_a), mp)
            ya = _matmul([h], s5_w_glu, i, bias=s5_b_glu, mode="gate", gate=h, out_dtype=BF16)
            out["re_p"].append(_pairs_to_state(fr_p)); out["im_p"].append(_pairs_to_state(fi_p))
            out["re_s"].append(_pairs_to_state(fr_s)); out["im_s"].append(_pairs_to_state(fi_s))
            yb, ssm_p = _ssd_mixer(u, i, ssd_prm, cols, prompt=True, nb=bsz, t=seq, out_rows=mp + ms)
            us3 = u[mp:].reshape(nb_s, t_s, in_even)
            dead = SAMPLE_SEQ_ROWS - t_s
            hist = state_ssd_conv[i]
            rows8 = lambda parts: jnp.concatenate(parts, axis=1).reshape(nb_s * SAMPLE_SEQ_ROWS, -1)
            ext_z = rows8([jnp.zeros((nb_s, dead, w_b), F32), us3[:, :, o_z:o_z + w_b]])
            ext_x = rows8([jnp.zeros((nb_s, dead - hist.shape[1], conv_dim), F32), hist,
                           us3[:, :, o_x:o_x + conv_dim]])
            ext_dt = rows8([jnp.zeros((nb_s, dead, LANES), F32),
                            jnp.pad(us3[:, :, o_dt:], ((0, 0), (0, 0), (0, LANES - n_heads)))])
            yb_s, ssm_s = _ssd_mixer((ext_z, ext_x, ext_dt), i, ssd_prm, (0, 0, 0, w_b, n_st), prompt=False,
                                     nb=nb_s, t=t_s, state_in=ssm_in, state_prev=ssm_s)
            yb_s = yb_s.reshape(nb_s, SAMPLE_SEQ_ROWS, w_b)[:, dead:].reshape(ms, w_b)
            yb = _put_rows(yb, yb_s, mp)
            mix = _matmul([ya, yb], w_out_even, i)
            kh = hist.shape[1]
            out["ssm_p"].append(ssm_p.reshape(bsz, n_heads, SSD_HEAD_DIM, n_st))
            out["sh_p"].append(jnp.stack([u[(b + 1) * seq - kh:(b + 1) * seq, o_x:o_x + conv_dim]
                                          for b in range(bsz)]))
            out["sh_s"].append(jnp.concatenate([hist, us3[:, :, o_x:o_x + conv_dim]], axis=1)[:, t_s:])
        else:
            gg = _matmul([xbf], conf_w_pw1, i, bias=conf_b_pw1, mode="glu")
            c = _conf_prompt(gg, i, conf_dw_w, conf_dw_b3, conf_ln_g3, conf_ln_b3, bsz, seq, mp + ms)
            hist = state_conformer_conv[i]
            g_s = gg[mp:].reshape(nb_s, t_s, d)
            ext = jnp.concatenate([hist, g_s], axis=1)
            c_s = _conf_sample(jnp.transpose(ext, (1, 0, 2)), i, conf_dw_w, conf_dw_b3, conf_ln_g3, conf_ln_b3)
            c = _put_rows(c, jnp.transpose(c_s, (1, 0, 2)).reshape(ms, d), mp)
            mix = _matmul([c], conf_w_pw2, i, bias=conf_b_pw2)
            kh = hist.shape[1]
            out["ch_p"].append(jnp.stack([gg[(b + 1) * seq - kh:(b + 1) * seq] for b in range(bsz)]))
            out["ch_s"].append(ext[:, t_s:])
        x, xpk, eid, gates = _ln_router(x, mix, ln_g, ln_b, layer, 0, alpha,
                                        w_route[layer], b_route[layer], n_eg, per_g)
        yb_moe, slots = _hier_moe(xpk, eid[:, :TOP_K], moe_w_gate, moe_w_up, moe_w_down, layer)
        x, xbf = _ln_combine(x, yb_moe, slots, gates, ln_g, ln_b, layer, 1, alpha)

    st = lambda k: jnp.stack(out[k])
    return (x[:mp].reshape(bsz, seq, d), x[mp:].reshape(nb_s, t_s, d),
            st("re_p"), st("im_p"), st("ssm_p"), st("sh_p"), st("ch_p"),
            st("re_s"), st("im_s"), ssm_s.reshape(state_ssm.shape), st("sh_s"), st("ch_s"))
```

```python
import functools
import math

import jax
import jax.numpy as jnp
from jax import lax
from jax.experimental import pallas as pl
from jax.experimental.pallas import tpu as pltpu

F32 = jnp.float32
BF16 = jnp.bfloat16
HIGHEST = lax.Precision.HIGHEST

LN_EPS = 1e-5
V7X_VMEM_LIMIT_BYTES = 56 * 1024 * 1024
LANES = 128
SUBLANES = 8

S5_GROUP = 16
SSD_HEAD_DIM = 64
SSD_GROUPS = 4
SSD_CHUNK = 128
TOP_K = 2

MM_TM = 512
MM_TN = 512
LN_TM = 256
MOE_BM = 256
MOE_CAST_ROWS = 256
MOE_ROW_BUFS = 3
S5_CHUNK = 16
S5_GB = 8
S5_RELAYOUT_UNROLL = 4
CONF_TT = 256
CONF_RT = 32
CONF_HALO = 32
SAMPLE_SEQ_ROWS = 8


def _cparams(sem):
    return pltpu.CompilerParams(dimension_semantics=sem, vmem_limit_bytes=V7X_VMEM_LIMIT_BYTES)


def _dot(a, b):
    return jnp.dot(a, b, preferred_element_type=F32)


def _split3(a):
    hi = a.astype(BF16)
    r1 = a - hi.astype(F32)
    mid = r1.astype(BF16)
    lo = (r1 - mid.astype(F32)).astype(BF16)
    return hi, mid, lo


def _dot_select_lhs(a, onehot):
    hi, mid, lo = _split3(a)
    return (_dot(hi, onehot) + _dot(mid, onehot)) + _dot(lo, onehot)


def _dot_select_rhs(onehot, b):
    hi, mid, lo = _split3(b)
    return (_dot(onehot, hi) + _dot(onehot, mid)) + _dot(onehot, lo)


def _dot_nt(a, b):
    return lax.dot_general(a, b, (((1,), (1,)), ((), ())), preferred_element_type=F32)


def _dot_tn(a, b):
    return lax.dot_general(a, b, (((0,), (0,)), ((), ())), preferred_element_type=F32)


def _pack_bf16_pairs(v):
    half = v.shape[1] // 2
    lo = lax.bitcast_convert_type(v[:, :half].astype(BF16).astype(F32), jnp.uint32)
    hi = lax.bitcast_convert_type(v[:, half:].astype(BF16).astype(F32), jnp.uint32)
    return hi | (lo >> 16)


def _unpack_bf16_pairs(w):
    lo = lax.bitcast_convert_type(w << 16, F32)
    hi = lax.bitcast_convert_type(w & jnp.uint32(0xFFFF0000), F32)
    return lo, hi


def _sigmoid(x):
    return 1.0 / (1.0 + jnp.exp(-x))


def _silu(x):
    return x * _sigmoid(x)


def _mm_kernel(*refs, n_lhs, n_col, has_bias, mode):
    pos = 0
    x_refs = refs[pos:pos + n_lhs]; pos += n_lhs
    w_refs = refs[pos:pos + n_lhs * n_col]; pos += n_lhs * n_col
    b_refs = ()
    if has_bias:
        b_refs = refs[pos:pos + n_col]; pos += n_col
    gate_ref = None
    if mode == "gate":
        gate_ref = refs[pos]; pos += 1
    out_ref = refs[pos]; pos += 1
    wbf_refs = refs[pos:]

    @pl.when(pl.program_id(1) == 0)
    def _():
        for w_ref, wbf_ref in zip(w_refs, wbf_refs):
            wbf_ref[...] = w_ref[...].astype(BF16)

    accs = []
    for c in range(n_col):
        acc = None
        for l in range(n_lhs):
            part = _dot(x_refs[l][...].astype(BF16), wbf_refs[l * n_col + c][...])
            acc = part if acc is None else acc + part
        if has_bias:
            acc = acc + b_refs[c][...]
        accs.append(acc)
    if mode == "glu":
        res = accs[0] * _sigmoid(accs[1])
    elif mode == "gate":
        res = gate_ref[...] * _sigmoid(accs[0])
    else:
        res = accs[0]
    out_ref[...] = res.astype(out_ref.dtype)


def _matmul(xs, w, layer, bias=None, mode="plain", gate=None, n_out=None, out_dtype=F32,
            tm=MM_TM, tn=MM_TN):
    n_lhs = len(xs)
    m, k = xs[0].shape
    n_total = w.shape[2]
    n_col = 2 if mode == "glu" else 1
    if n_out is None:
        n_out = n_total // n_col
    assert m % tm == 0 and w.shape[1] == n_lhs * k
    nj = pl.cdiv(n_out, tn)
    glu_off = (n_total // 2) // tn if mode == "glu" else 0
    if mode == "glu":
        assert (n_total // 2) % tn == 0

    in_specs = [pl.BlockSpec((tm, k), lambda j, i: (i, 0)) for _ in range(n_lhs)]
    args = list(xs)
    for l in range(n_lhs):
        for c in range(n_col):
            in_specs.append(pl.BlockSpec((None, k, tn),
                                         functools.partial(lambda j, i, l, c: (layer, l, j + c * glu_off), l=l, c=c)))
            args.append(w)
    if bias is not None:
        b3 = bias.reshape(bias.shape[0], 1, bias.shape[1])
        for c in range(n_col):
            in_specs.append(pl.BlockSpec((None, 1, tn),
                                         functools.partial(lambda j, i, c: (layer, 0, j + c * glu_off), c=c)))
            args.append(b3)
    if mode == "gate":
        in_specs.append(pl.BlockSpec((tm, tn), lambda j, i: (i, j)))
        args.append(gate)
    kern = functools.partial(_mm_kernel, n_lhs=n_lhs, n_col=n_col, has_bias=bias is not None, mode=mode)
    return pl.pallas_call(
        kern,
        grid=(nj, m // tm),
        in_specs=in_specs,
        out_specs=pl.BlockSpec((tm, tn), lambda j, i: (i, j)),
        out_shape=jax.ShapeDtypeStruct((m, n_out), out_dtype),
        scratch_shapes=[pltpu.VMEM((k, tn), BF16) for _ in range(n_lhs * n_col)],
        compiler_params=_cparams(("arbitrary", "arbitrary")),
        name=f"mm_{mode}",
    )(*args)


def _put_rows_kernel(small_ref, big_ref, out_ref):
    del big_ref
    out_ref[...] = small_ref[...]


def _put_rows(big, small, row0):
    n, w = small.shape
    assert row0 % n == 0 and big.shape[1] == w and big.dtype == small.dtype
    return pl.pallas_call(
        _put_rows_kernel, grid=(1,),
        in_specs=[pl.BlockSpec((n, w), lambda i: (0, 0)), pl.BlockSpec(memory_space=pl.ANY)],
        out_specs=pl.BlockSpec((n, w), lambda i: (row0 // n, 0)),
        out_shape=jax.ShapeDtypeStruct(big.shape, big.dtype),
        input_output_aliases={1: 0}, compiler_params=_cparams(("arbitrary",)), name="put_rows",
    )(small, big)


def _ln_rows(y, g, b):
    mu = jnp.mean(y, axis=-1, keepdims=True)
    yc = y - mu
    var = jnp.mean(yc * yc, axis=-1, keepdims=True)
    return yc * lax.rsqrt(var + LN_EPS) * g + b


def _first_argmax(v, lane, big):
    m = jnp.max(v, axis=-1, keepdims=True)
    idx = jnp.min(jnp.where(v == m, lane, big), axis=-1, keepdims=True)
    return m, idx


def _ln_router_kernel(x_ref, mix_ref, g_ref, b_ref, wrh_ref, wrl_ref, br_ref,
                      out_ref, outpk_ref, eid_ref, gate_ref, *, alpha, n_groups, per_group):
    y = alpha * x_ref[...] + mix_ref[...].astype(F32)
    out = _ln_rows(y, g_ref[...], b_ref[...])
    out_ref[...] = out
    outpk_ref[...] = _pack_bf16_pairs(out)
    out_hi = out.astype(BF16)
    out_lo = (out - out_hi.astype(F32)).astype(BF16)
    wr_hi = wrh_ref[...]
    lg = _dot(out_hi, wr_hi) + (_dot(out_lo, wr_hi) + _dot(out_hi, wrl_ref[...])) + br_ref[...]
    lane = lax.broadcasted_iota(jnp.int32, lg.shape, 1)
    neg = jnp.float32(-jnp.inf)
    big = jnp.int32(LANES)
    gl = jnp.where(lane < n_groups, lg, neg)
    gmax, grp = _first_argmax(gl, lane, big)
    pg_top = 1.0 / jnp.sum(jnp.exp(gl - gmax), axis=-1, keepdims=True)
    lo = n_groups + grp * per_group
    el = jnp.where((lane >= lo) & (lane < lo + per_group), lg, neg)
    m1, i1 = _first_argmax(el, lane, big)
    el2 = jnp.where(lane == i1, neg, el)
    m2, i2 = _first_argmax(el2, lane, big)
    e2 = jnp.exp(m2 - m1)
    g1 = pg_top / (1.0 + e2)
    g2 = pg_top * e2 / (1.0 + e2)
    eid_ref[...] = jnp.where(lane == 0, i1 - n_groups, jnp.where(lane == 1, i2 - n_groups, 0))
    gate_ref[...] = jnp.where(lane == 0, g1, jnp.where(lane == 1, g2, 0.0))


def _ln_router(x, mix, ln_g, ln_b, layer, which, alpha, wr, br, n_groups, per_group, tm=LN_TM):
    m, d = x.shape
    assert m % tm == 0
    row = pl.BlockSpec((tm, d), lambda i: (i, 0))
    par = pl.BlockSpec((None, None, 1, d), lambda i: (layer, which, 0, 0))
    small = pl.BlockSpec((tm, LANES), lambda i: (i, 0))
    wspec = pl.BlockSpec((d, LANES), lambda i: (0, 0))
    g4 = ln_g.reshape(ln_g.shape[0], ln_g.shape[1], 1, d)
    b4 = ln_b.reshape(ln_b.shape[0], ln_b.shape[1], 1, d)
    wr_hi = wr.astype(BF16)
    wr_lo = (wr - wr_hi.astype(F32)).astype(BF16)
    kern = functools.partial(_ln_router_kernel, alpha=alpha, n_groups=n_groups, per_group=per_group)
    return pl.pallas_call(
        kern, grid=(m // tm,),
        in_specs=[row, row, par, par, wspec, wspec, pl.BlockSpec((1, LANES), lambda i: (0, 0))],
        out_specs=[row, pl.BlockSpec((tm, d // 2), lambda i: (i, 0)), small, small],
        out_shape=[jax.ShapeDtypeStruct((m, d), F32), jax.ShapeDtypeStruct((m, d // 2), jnp.uint32),
                   jax.ShapeDtypeStruct((m, LANES), jnp.int32), jax.ShapeDtypeStruct((m, LANES), F32)],
        compiler_params=_cparams(("parallel",)), name="ln_router",
    )(x, mix, g4, b4, wr_hi, wr_lo, br)


def _pair_row_copy(y_hbm, buf_ref, sem_ref, slot, k, src_row, dst_row, n_rows):
    return pltpu.make_async_copy(y_hbm.at[pl.ds(src_row, n_rows), :],
                                 buf_ref.at[slot, k, pl.ds(dst_row, n_rows), :], sem_ref.at[slot])


def _ln_combine_kernel(slot_ref, x_ref, gate_ref, y_hbm, g_ref, b_ref, out_ref, outbf_ref, buf_ref, sem_ref, *,
                       alpha):
    i = pl.program_id(0)
    tm = x_ref.shape[0]
    cur = i % 2

    def issue(tile, dst):
        base = tile * (tm * TOP_K)

        def body(r, carry):
            for k in range(TOP_K):
                _pair_row_copy(y_hbm, buf_ref, sem_ref, dst, k, slot_ref[base + TOP_K * r + k], r, 1).start()
            return carry

        lax.fori_loop(0, tm, body, 0, unroll=4)

    @pl.when(i == 0)
    def _():
        issue(0, 0)

    @pl.when(i + 1 < pl.num_programs(0))
    def _():
        base = (i + 1) * (tm * TOP_K)
        for r in range(tm):
            for k in range(TOP_K):
                _pair_row_copy(y_hbm, buf_ref, sem_ref, 1 - cur, k, slot_ref[base + TOP_K * r + k], r, 1).start()

    for k in range(TOP_K):
        _pair_row_copy(y_hbm, buf_ref, sem_ref, cur, k, 0, 0, tm).wait()
    gates = gate_ref[...]
    lo0, hi0 = _unpack_bf16_pairs(buf_ref[cur, 0])
    lo1, hi1 = _unpack_bf16_pairs(buf_ref[cur, 1])
    g0, g1 = gates[:, 0:1], gates[:, 1:2]
    y = alpha * x_ref[...] + jnp.concatenate([lo0 * g0 + lo1 * g1, hi0 * g0 + hi1 * g1], axis=1)
    out = _ln_rows(y, g_ref[...], b_ref[...])
    out_ref[...] = out
    outbf_ref[...] = out.astype(BF16)


def _ln_combine(x, yb, slot_flat, gates, ln_g, ln_b, layer, which, alpha, tm=LN_TM):
    m, d = x.shape
    assert m % tm == 0 and TOP_K == 2
    row = pl.BlockSpec((tm, d), lambda i, s: (i, 0))
    gate_spec = pl.BlockSpec((tm, LANES), lambda i, s: (i, 0))
    par = pl.BlockSpec((None, None, 1, d), lambda i, s: (layer, which, 0, 0))
    g4 = ln_g.reshape(ln_g.shape[0], ln_g.shape[1], 1, d)
    b4 = ln_b.reshape(ln_b.shape[0], ln_b.shape[1], 1, d)
    return pl.pallas_call(
        functools.partial(_ln_combine_kernel, alpha=alpha),
        grid_spec=pltpu.PrefetchScalarGridSpec(
            num_scalar_prefetch=1, grid=(m // tm,),
            in_specs=[row, gate_spec, pl.BlockSpec(memory_space=pl.ANY), par, par],
            out_specs=[row, row],
            scratch_shapes=[pltpu.VMEM((2, TOP_K, tm, d // 2), jnp.uint32), pltpu.SemaphoreType.DMA((2,))]),
        out_shape=[jax.ShapeDtypeStruct((m, d), F32), jax.ShapeDtypeStruct((m, d), BF16)],
        compiler_params=pltpu.CompilerParams(dimension_semantics=("arbitrary",),
                                             vmem_limit_bytes=V7X_VMEM_LIMIT_BYTES,
                                             disable_bounds_checks=True),
        name="ln_combine",
    )(slot_flat, x, gates, yb, g4, b4)


def _moe_row_copy(x_hbm, xbuf_ref, sem_ref, slot, src_row, dst_row, n_rows):
    return pltpu.make_async_copy(x_hbm.at[pl.ds(src_row, n_rows), :],
                                 xbuf_ref.at[slot, pl.ds(dst_row, n_rows), :], sem_ref.at[slot])


def _expert_weight_stream(i, n_used, be_ref, nxt_ref, w_hbms, wbuf_ref, wsem_ref, layer, on_arrival):
    def copies(expert):
        return [pltpu.make_async_copy(w.at[layer, expert], wbuf_ref.at[k], wsem_ref.at[0])
                for k, w in enumerate(w_hbms)]

    @pl.when(i == 0)
    def _():
        for c in copies(be_ref[0]):
            c.start()

    prev = be_ref[jnp.maximum(i - 1, 0)]

    @pl.when((i == 0) | ((i < n_used) & (be_ref[i] != prev)))
    def _():
        for c in copies(be_ref[i]):
            c.wait()
        on_arrival()
        nxt = nxt_ref[i]

        @pl.when(nxt >= 0)
        def _():
            for c in copies(nxt):
                c.start()


def _moe_up_kernel(be_ref, nxt_ref, nu_ref, tok_ref, x_hbm, wg_hbm, wu_hbm, h_ref,
                   wbuf_ref, wgbf_ref, wubf_ref, xbuf_ref, sem_ref, wsem_ref, *, layer):
    i = pl.program_id(0)
    n_used = nu_ref[0]
    n_buf, bm = xbuf_ref.shape[0], xbuf_ref.shape[1]
    depth = n_buf - 1
    slot = lax.rem(i, n_buf)

    def start_row(blk, r):
        _moe_row_copy(x_hbm, xbuf_ref, sem_ref, lax.rem(blk, n_buf), tok_ref[blk * bm + r], r, 1).start()

    @pl.when(i == 0)
    def _():
        def body(k, carry):
            start_row(k // bm, lax.rem(k, bm))
            return carry

        lax.fori_loop(0, jnp.minimum(depth, n_used) * bm, body, 0)

    @pl.when(i + depth < n_used)
    def _():
        for r in range(bm):
            start_row(i + depth, r)

    def cast_weights():
        rows = wgbf_ref.shape[0]
        for r0 in range(0, rows, MOE_CAST_ROWS):
            sl = slice(r0, r0 + MOE_CAST_ROWS)
            wgbf_ref[sl, :] = wbuf_ref[0, sl, :].astype(BF16)
            wubf_ref[sl, :] = wbuf_ref[1, sl, :].astype(BF16)

    _expert_weight_stream(i, n_used, be_ref, nxt_ref, (wg_hbm, wu_hbm), wbuf_ref, wsem_ref, layer, cast_weights)

    @pl.when(i < n_used)
    def _():
        _moe_row_copy(x_hbm, xbuf_ref, sem_ref, slot, 0, 0, bm).wait()
        x = jnp.concatenate(_unpack_bf16_pairs(xbuf_ref[slot]), axis=1).astype(BF16)
        h = _silu(_dot(x, wgbf_ref[...])) * _dot(x, wubf_ref[...])
        h_ref[...] = h.astype(BF16)

    @pl.when(i >= n_used)
    def _():
        h_ref[...] = jnp.zeros(h_ref.shape, BF16)


def _moe_down_kernel(be_ref, nxt_ref, nu_ref, h_ref, wd_hbm, out_ref, wbuf_ref, wdbf_ref, wsem_ref, *, layer):
    i = pl.program_id(0)

    def cast_weights():
        wdbf_ref[...] = wbuf_ref[0].astype(BF16)

    _expert_weight_stream(i, nu_ref[0], be_ref, nxt_ref, (wd_hbm,), wbuf_ref, wsem_ref, layer, cast_weights)

    @pl.when(i < nu_ref[0])
    def _():
        out_ref[...] = _pack_bf16_pairs(_dot(h_ref[...], wdbf_ref[...]))

    @pl.when(i >= nu_ref[0])
    def _():
        out_ref[...] = jnp.zeros(out_ref.shape, jnp.uint32)


def _moe_experts(xpk, tok_buf, blk_expert, blk_next, n_used, w_gate, w_up, w_down, layer, bm=MOE_BM):
    cap = tok_buf.shape[0]
    d = w_gate.shape[2]
    f = w_gate.shape[3]
    n_blocks = cap // bm
    anyspace = pl.BlockSpec(memory_space=pl.ANY)
    gather_params = pltpu.CompilerParams(dimension_semantics=("arbitrary",),
                                         vmem_limit_bytes=V7X_VMEM_LIMIT_BYTES, disable_bounds_checks=True)
    hidden = pl.pallas_call(
        functools.partial(_moe_up_kernel, layer=layer),
        grid_spec=pltpu.PrefetchScalarGridSpec(
            num_scalar_prefetch=4,
            grid=(n_blocks,),
            in_specs=[anyspace, anyspace, anyspace],
            out_specs=pl.BlockSpec((bm, f), lambda i, be, nx, nu, tok: (i, 0)),
            scratch_shapes=[pltpu.VMEM((2, d, f), F32), pltpu.VMEM((d, f), BF16), pltpu.VMEM((d, f), BF16),
                            pltpu.VMEM((MOE_ROW_BUFS, bm, d // 2), jnp.uint32),
                            pltpu.SemaphoreType.DMA((MOE_ROW_BUFS,)), pltpu.SemaphoreType.DMA((1,))],
        ),
        out_shape=jax.ShapeDtypeStruct((cap, f), BF16),
        compiler_params=gather_params,
        name="moe_up",
    )(blk_expert, blk_next, n_used, tok_buf, xpk, w_gate, w_up)
    return pl.pallas_call(
        functools.partial(_moe_down_kernel, layer=layer),
        grid_spec=pltpu.PrefetchScalarGridSpec(
            num_scalar_prefetch=3,
            grid=(n_blocks,),
            in_specs=[pl.BlockSpec((bm, f), lambda i, be, nx, nu: (jnp.minimum(i, nu[0] - 1), 0)), anyspace],
            out_specs=pl.BlockSpec((bm, d // 2), lambda i, be, nx, nu: (i, 0)),
            scratch_shapes=[pltpu.VMEM((1, f, d), F32), pltpu.VMEM((f, d), BF16),
                            pltpu.SemaphoreType.DMA((1,))],
        ),
        out_shape=jax.ShapeDtypeStruct((cap, d // 2), jnp.uint32),
        compiler_params=_cparams(("arbitrary",)),
        name="moe_down",
    )(blk_expert, blk_next, n_used, hidden, w_down)


def _moe_dispatch(eid, n_experts, bm):
    t = eid.shape[0]
    n_assign = t * TOP_K
    flat_e = eid.reshape(-1)
    ids = jnp.arange(n_experts, dtype=jnp.int32)
    onehot = (flat_e[:, None] == ids[None, :]).astype(jnp.int32)
    csum = jnp.cumsum(onehot, axis=0)
    counts = csum[-1]
    padded = (counts + bm - 1) // bm * bm
    pad_end = jnp.cumsum(padded)
    pad_start = pad_end - padded
    slot = jnp.sum(onehot * (csum - 1 + pad_start[None, :]), axis=1).astype(jnp.int32)
    n_blocks = -(-n_assign // bm) + n_experts
    cap = n_blocks * bm
    tok_buf = jnp.zeros((cap,), jnp.int32).at[slot].set(jnp.arange(n_assign, dtype=jnp.int32) // TOP_K)
    blk_start = jnp.arange(n_blocks, dtype=jnp.int32) * bm
    blk_expert = jnp.minimum(jnp.sum((pad_end[None, :] <= blk_start[:, None]).astype(jnp.int32), axis=1),
                             n_experts - 1).astype(jnp.int32)
    n_used = (pad_end[-1] // bm).astype(jnp.int32).reshape(1)
    at_or_after = lax.cummin(jnp.where(counts > 0, ids, n_experts), axis=0, reverse=True)
    after = jnp.concatenate([at_or_after[1:], jnp.full((1,), n_experts, jnp.int32)])
    after = jnp.where(after < n_experts, after, -1)
    blk_next = jnp.sum(jnp.where(blk_expert[:, None] == ids[None, :], after[None, :], 0), axis=1).astype(jnp.int32)
    return slot, tok_buf, blk_expert, blk_next, n_used


def _hier_moe(x, eid, w_gate, w_up, w_down, layer):
    n_experts = w_gate.shape[1]
    slot, tok_buf, blk_expert, blk_next, n_used = _moe_dispatch(eid, n_experts, MOE_BM)
    return _moe_experts(x, tok_buf, blk_expert, blk_next, n_used, w_gate, w_up, w_down, layer), slot


def _s5_prepare(lam_re, lam_im, log_dt, b_re, b_im, c_re, c_im, d_skip, lc):
    g, p, c = b_re.shape
    dt = jnp.exp(log_dt.astype(F32))[:, None]
    lr, li = lam_re.astype(F32), lam_im.astype(F32)
    mag = jnp.exp(lr * dt)
    ab_re, ab_im = mag * jnp.cos(li * dt), mag * jnp.sin(li * dt)
    den = lr * lr + li * li
    q_re = ((ab_re - 1.0) * lr + ab_im * li) / den
    q_im = (ab_im * lr - (ab_re - 1.0) * li) / den
    bb_re = q_re[..., None] * b_re - q_im[..., None] * b_im
    bb_im = q_re[..., None] * b_im + q_im[..., None] * b_re
    ks = jnp.arange(lc + 1, dtype=F32)[:, None, None]
    pmag = jnp.exp(lr * dt * ks)
    pw_re, pw_im = pmag * jnp.cos(li * dt * ks), pmag * jnp.sin(li * dt * ks)
    t_re = pw_re[:lc, :, :, None] * bb_re[None] - pw_im[:lc, :, :, None] * bb_im[None]
    t_im = pw_re[:lc, :, :, None] * bb_im[None] + pw_im[:lc, :, :, None] * bb_re[None]
    kt = (jnp.einsum("gop,kgpi->gkoi", c_re, t_re, precision=HIGHEST)
          - jnp.einsum("gop,kgpi->gkoi", c_im, t_im, precision=HIGHEST))
    padk = jnp.pad(kt, ((0, 0), (lc, 1), (0, 0), (0, 0)))
    toe = jnp.tile(padk, (1, lc, 1, 1))[:, :2 * lc * lc].reshape(g, lc, 2 * lc, c, c)[:, :, lc:]
    m_intra = jnp.transpose(toe, (0, 1, 4, 2, 3)).reshape(g, lc * c, lc * c)
    rev = pw_re[:lc][::-1], pw_im[:lc][::-1]
    wst_re = rev[0][:, :, :, None] * bb_re[None] - rev[1][:, :, :, None] * bb_im[None]
    wst_im = rev[0][:, :, :, None] * bb_im[None] + rev[1][:, :, :, None] * bb_re[None]
    wst_re = jnp.transpose(wst_re, (1, 0, 3, 2)).reshape(g, lc * c, p)
    wst_im = jnp.transpose(wst_im, (1, 0, 3, 2)).reshape(g, lc * c, p)
    half = (jnp.arange(g) % 2)[:, None] == jnp.arange(2)[None, :]

    def place_cols(wm):
        return jnp.where(half[:, None, :, None], wm[:, :, None, :], 0.0).reshape(g, lc * c, 2 * p)

    wy_re = c_re[:, None] * pw_re[1:].transpose(1, 0, 2)[:, :, None, :] \
        - c_im[:, None] * pw_im[1:].transpose(1, 0, 2)[:, :, None, :]
    wy_im = -(c_re[:, None] * pw_im[1:].transpose(1, 0, 2)[:, :, None, :]
              + c_im[:, None] * pw_re[1:].transpose(1, 0, 2)[:, :, None, :])
    wy_re = jnp.transpose(wy_re, (0, 3, 1, 2)).reshape(g, p, lc * c)
    wy_im = jnp.transpose(wy_im, (0, 3, 1, 2)).reshape(g, p, lc * c)

    def place_rows(wm):
        return jnp.where(half[:, :, None, None], wm[:, None, :, :], 0.0).reshape(g, 2 * p, lc * c)

    al_re = pw_re[lc].reshape(g // 2, 1, 2 * p)
    al_im = pw_im[lc].reshape(g // 2, 1, 2 * p)
    d_ch = jnp.tile(d_skip.astype(F32).reshape(g, 1, c), (1, lc, 1)).reshape(g, 1, lc * c)
    return dict(m=m_intra.astype(BF16), wst_re=place_cols(wst_re).astype(BF16),
                wst_im=place_cols(wst_im).astype(BF16), wy_re=place_rows(wy_re).astype(BF16),
                wy_im=place_rows(wy_im).astype(BF16), al_re=al_re, al_im=al_im, d=d_ch)


def _block_transpose8(vs, blk):
    vs = list(vs)
    lane_blk = lax.broadcasted_iota(jnp.int32, vs[0].shape, 1) // blk
    for s in (4, 2, 1):
        low = (lane_blk & s) == 0
        for i in range(8):
            if i & s:
                continue
            a, b = vs[i], vs[i + s]
            vs[i] = jnp.where(low, a, pltpu.roll(b, s * blk, axis=1))
            vs[i + s] = jnp.where(low, pltpu.roll(a, LANES - s * blk, axis=1), b)
    return vs


def _rows_to_chunks(u_ref, uc_ref, lc):
    gb, rows, w = uc_ref.shape
    blk = w // lc

    def body(rb, carry):
        r0 = pl.multiple_of(rb * SUBLANES, SUBLANES)
        for half in range(lc // 8):
            vs = [u_ref[pl.ds(r0 * lc + half * 8 + l, SUBLANES, stride=lc), :] for l in range(8)]
            ws = _block_transpose8(vs, blk)
            for g in range(gb):
                uc_ref[g, pl.ds(r0, SUBLANES), half * LANES:(half + 1) * LANES] = ws[g]
        return carry

    lax.fori_loop(0, rows // SUBLANES, body, 0, unroll=S5_RELAYOUT_UNROLL)


def _chunks_to_rows(yc_ref, h_ref, lc):
    gb, rows, w = yc_ref.shape
    blk = w // lc

    def body(rb, carry):
        r0 = pl.multiple_of(rb * SUBLANES, SUBLANES)
        for half in range(lc // 8):
            ws = [yc_ref[g, pl.ds(r0, SUBLANES), half * LANES:(half + 1) * LANES] for g in range(gb)]
            vs = _block_transpose8(ws, blk)
            for l in range(8):
                h_ref[pl.ds(r0 * lc + half * 8 + l, SUBLANES, stride=lc), :] = vs[l]
        return carry

    lax.fori_loop(0, rows // SUBLANES, body, 0, unroll=S5_RELAYOUT_UNROLL)


def _s5_kernel(*refs, gb, nb, nk, lc, natural):
    (u_ref, m_ref, wsr_ref, wsi_ref, wyr_ref, wyi_ref, alr_ref, ali_ref, d_ref, s0r_ref, s0i_ref,
     h_ref, sfr_ref, sfi_ref, locr_ref, loci_ref, str_ref, sti_ref) = refs[:18]
    if natural:
        uc_ref, yc_ref = refs[18:]
        _rows_to_chunks(u_ref, uc_ref, lc)
    else:
        uc_ref, yc_ref = u_ref, h_ref
    for pair in range(gb // 2):
        g0, g1 = 2 * pair, 2 * pair + 1
        u0 = uc_ref[g0]
        u1 = uc_ref[g1]
        ub0 = u0.astype(BF16)
        ub1 = u1.astype(BF16)
        loc_re = _dot(ub0, wsr_ref[g0]) + _dot(ub1, wsr_ref[g1])
        loc_im = _dot(ub0, wsi_ref[g0]) + _dot(ub1, wsi_ref[g1])
        a_re = alr_ref[pair]
        a_im = ali_ref[pair]
        s_re = s0r_ref[pair]
        s_im = s0i_ref[pair]
        if nk == 1:
            st_re, st_im = s_re, s_im
            f_re = a_re * s_re - a_im * s_im + loc_re
            f_im = a_re * s_im + a_im * s_re + loc_im
        else:
            locr_ref[...] = loc_re
            loci_ref[...] = loc_im

            def step(k, carry):
                c_re, c_im = carry
                rows = pl.ds(k, nb, stride=nk)
                str_ref[rows, :] = c_re
                sti_ref[rows, :] = c_im
                l_re = locr_ref[rows, :]
                l_im = loci_ref[rows, :]
                return (a_re * c_re - a_im * c_im + l_re, a_re * c_im + a_im * c_re + l_im)

            f_re, f_im = lax.fori_loop(0, nk, step, (s_re, s_im))
            st_re = str_ref[...]
            st_im = sti_ref[...]
        sfr_ref[pair] = f_re
        sfi_ref[pair] = f_im
        sb_re = st_re.astype(BF16)
        sb_im = st_im.astype(BF16)
        for gi, u, ub in ((g0, u0, ub0), (g1, u1, ub1)):
            y = (_dot(ub, m_ref[gi]) + _dot(sb_re, wyr_ref[gi]) + _dot(sb_im, wyi_ref[gi])
                 + d_ref[gi] * u)
            yc_ref[gi] = jax.nn.gelu(y)
    if natural:
        _chunks_to_rows(yc_ref, h_ref, lc)


def _s5_mixer(u, prep, s0_re, s0_im, nb, nk, gb=S5_GB, natural_rows=None):
    g, w, _ = prep["m"].shape
    p2 = prep["al_re"].shape[2]
    r = nb * nk
    natural = natural_rows is not None
    grp = lambda *shape: pl.BlockSpec((gb,) + shape, lambda i: (i, 0, 0))
    pr = lambda *shape: pl.BlockSpec((gb // 2,) + shape, lambda i: (i, 0, 0))
    scratch = [pltpu.VMEM((r, p2), F32) for _ in range(4)]
    if natural:
        lc = w // S5_GROUP
        assert gb == 8 and gb * S5_GROUP == LANES and lc % 8 == 0 and r % SUBLANES == 0
        io_spec = pl.BlockSpec((r * lc, LANES), lambda i: (0, i))
        out_h = jax.ShapeDtypeStruct((natural_rows, g * S5_GROUP), F32)
        scratch += [pltpu.VMEM((gb, r, w), F32), pltpu.VMEM((gb, r, w), F32)]
    else:
        lc = 0
        io_spec = grp(r, w)
        out_h = jax.ShapeDtypeStruct((g, r, w), F32)
    kern = functools.partial(_s5_kernel, gb=gb, nb=nb, nk=nk, lc=lc, natural=natural)
    return pl.pallas_call(
        kern,
        grid=(g // gb,),
        in_specs=[io_spec, grp(w, w), grp(w, p2), grp(w, p2), grp(p2, w), grp(p2, w),
                  pr(1, p2), pr(1, p2), grp(1, w), pr(nb, p2), pr(nb, p2)],
        out_specs=[io_spec, pr(nb, p2), pr(nb, p2)],
        out_shape=[out_h,
                   jax.ShapeDtypeStruct((g // 2, nb, p2), F32),
                   jax.ShapeDtypeStruct((g // 2, nb, p2), F32)],
        scratch_shapes=scratch,
        compiler_params=_cparams(("parallel",)),
        name="s5_prompt" if natural else "s5_sample",
    )(u, prep["m"], prep["wst_re"], prep["wst_im"], prep["wy_re"], prep["wy_im"],
      prep["al_re"], prep["al_im"], prep["d"], s0_re, s0_im)


def _to_chunks(u2d, nb, nk, lc, g):
    c = u2d.shape[1] // g
    u5 = u2d.reshape(nb, nk, lc, g, c)
    return jnp.transpose(u5, (3, 0, 1, 2, 4)).reshape(g, nb * nk, lc * c)


def _from_chunks(hc, nb, nk, lc, g):
    c = hc.shape[2] // lc
    h5 = hc.reshape(g, nb, nk, lc, c)
    return jnp.transpose(h5, (1, 2, 3, 0, 4)).reshape(nb * nk * lc, g * c)


def _state_to_pairs(s):
    b, g, p = s.shape
    return jnp.transpose(s.reshape(b, g // 2, 2 * p), (1, 0, 2))


def _pairs_to_state(f):
    g2, b, p2 = f.shape
    return jnp.transpose(f, (1, 0, 2)).reshape(b, g2 * 2, p2 // 2)


def _softplus(x):
    return jnp.maximum(x, 0.0) + jnp.log1p(jnp.exp(-jnp.abs(x)))


def _ssd_kernel(*refs, prompt, q, n_heads, hd, has_prev):
    it = iter(refs)
    z_ref, xs_ref, bm_ref, cm_ref, dt_ref = (next(it) for _ in range(5))
    if prompt:
        hx_ref, hb_ref, hc_ref = (next(it) for _ in range(3))
    cwx_ref, cwb_ref, cwc_ref, cbx_ref, cbb_ref, cbc_ref = (next(it) for _ in range(6))
    dtb_ref, a_ref, dsk_ref, nw_ref, sel_ref, e_ref, et_ref = (next(it) for _ in range(7))
    if not prompt:
        sin_ref = next(it)
    if has_prev:
        next(it)
    y_ref, sout_ref = next(it), next(it)
    padx_ref, padb_ref, padc_ref, yacc_ref = (next(it) for _ in range(4))

    r = z_ref.shape[0]
    nseq = r // q
    hpg = xs_ref.shape[1] // hd
    row = lax.broadcasted_iota(jnp.int32, (r, 1), 0)
    if prompt:
        first = pl.program_id(2) == 0
        keep = jnp.where(first, 0.0, 1.0)
        live = None
    else:
        live = (row % q) >= (q // 2)

    def conv(x_ref, halo_ref, pad_ref, w_ref, b_ref):
        if prompt:
            pad_ref[0:SUBLANES, :] = halo_ref[...] * keep
        else:
            pad_ref[0:SUBLANES, :] = jnp.zeros((SUBLANES, pad_ref.shape[1]), F32)
        pad_ref[SUBLANES:SUBLANES + r, :] = x_ref[...]
        kk = w_ref.shape[0]
        acc = b_ref[...]
        for k in range(kk):
            acc = acc + w_ref[k:k + 1, :] * pad_ref[pl.ds(SUBLANES - (kk - 1) + k, r), :]
        return _silu(acc)

    xs = conv(xs_ref, hx_ref if prompt else None, padx_ref, cwx_ref, cbx_ref)
    bc = conv(bm_ref, hb_ref if prompt else None, padb_ref, cwb_ref, cbb_ref)
    cc = conv(cm_ref, hc_ref if prompt else None, padc_ref, cwc_ref, cbc_ref)

    lane = lax.broadcasted_iota(jnp.int32, (r, LANES), 1)
    dtv = jnp.where(lane < n_heads, _softplus(dt_ref[...] + dtb_ref[...]), 0.0)
    if not prompt:
        dtv = jnp.where(live, dtv, 0.0)
    sel = sel_ref[...]
    both = _dot_select_lhs(jnp.concatenate([dtv, dtv * a_ref[...]], axis=0), sel)
    dt8, adt8 = both[0:r], both[r:2 * r]

    ri = lax.broadcasted_iota(jnp.int32, (r, r), 0)
    ci = lax.broadcasted_iota(jnp.int32, (r, r), 1)
    same = (ri // q) == (ci // q)
    tri = same & (ci <= ri)
    masks = jnp.concatenate([jnp.where(tri, 1.0, 0.0), jnp.where(same, 1.0, 0.0)], axis=0).astype(BF16)
    sums = _dot_select_rhs(masks, adt8)
    acum, atot = sums[0:r], sums[r:2 * r]
    acum_t = acum.T
    expand = _dot_select_lhs(jnp.concatenate([jnp.exp(acum), dt8, jnp.exp(atot - acum)], axis=0), e_ref[...])
    eac_x, dt_x, dte_x = expand[0:r], expand[r:2 * r], expand[2 * r:3 * r]
    dec = _dot_select_rhs(et_ref[...], jnp.exp(atot).T)

    cb = _dot_nt(cc.astype(BF16), bc.astype(BF16))
    xdt = xs * dt_x
    neg = jnp.float32(-jnp.inf)
    lane_hd = lax.broadcasted_iota(jnp.int32, (r, 2 * hd), 1)
    for pair in range(hpg // 2):
        xp = xdt[:, pair * 2 * hd:(pair + 1) * 2 * hd]
        acc = None
        for half in range(2):
            hl = 2 * pair + half
            seg = acum[:, hl:hl + 1] - acum_t[hl:hl + 1, :]
            scores = cb * jnp.exp(jnp.where(tri, seg, neg))
            mask = (lane_hd < hd) if half == 0 else (lane_hd >= hd)
            part = _dot(scores.astype(BF16), jnp.where(mask, xp, 0.0).astype(BF16))
            acc = part if acc is None else acc + part
        yacc_ref[:, pair * 2 * hd:(pair + 1) * 2 * hd] = acc

    wgt = (xdt * dte_x).astype(BF16)
    bcb = bc.astype(BF16)
    ccb = cc.astype(BF16)
    if prompt:
        @pl.when(first)
        def _():
            sout_ref[...] = jnp.zeros(sout_ref.shape, F32)

        s_old = sout_ref[...]
        yacc_ref[...] += _dot_nt(ccb, s_old.astype(BF16)) * eac_x
        sout_ref[...] = dec[:, 0:1] * s_old + _dot_tn(wgt, bcb)
    else:
        for j in range(nseq):
            rows = slice(j * q, (j + 1) * q)
            s_old = sin_ref[j]
            yacc_ref[rows, :] += _dot_nt(ccb[rows], s_old.astype(BF16)) * eac_x[rows]
            wj = jnp.where((row // q) == j, wgt, jnp.zeros_like(wgt))
            sout_ref[j] = dec[:, j * q:j * q + 1] * s_old + _dot_tn(wj, bcb)

    y = yacc_ref[...] + dsk_ref[...] * xs
    y = y * _silu(z_ref[...])
    y = y * lax.rsqrt(jnp.mean(y * y, axis=-1, keepdims=True) + LN_EPS)
    y_ref[...] = (y * nw_ref[...]).astype(y_ref.dtype)


def _ssd_constants(n_heads, hd, n_groups):
    hpg = n_heads // n_groups
    lane = jnp.arange(LANES)
    sel = (lane[None, :, None] == (jnp.arange(n_groups)[:, None, None] * hpg + lane[None, None, :])) \
        & (lane[None, None, :] < hpg)
    e = (lane[:, None] == (jnp.arange(hpg * hd) // hd)[None, :])
    return sel.astype(BF16), e.astype(BF16), e.T.astype(BF16)


def _pad_lanes(v):
    return jnp.pad(v, ((0, 0), (0, LANES - v.shape[1])))[:, None, :]


def _ssd_params(conv_w, conv_b, dt_bias, a_log, d_skip, norm_w):
    n_heads = dt_bias.shape[1]
    hd = norm_w.shape[1] // n_heads
    return dict(
        ssd_conv_w=conv_w, ssd_conv_b3=conv_b[:, None, :],
        ssd_dt_bias3=_pad_lanes(dt_bias.astype(F32)),
        ssd_a3=_pad_lanes(-jnp.exp(a_log.astype(F32))),
        ssd_d3=jnp.repeat(d_skip.astype(F32), hd, axis=1)[:, None, :],
        ssd_norm3=norm_w[:, None, :],
        ssd_consts=_ssd_constants(n_heads, hd, SSD_GROUPS))


def _ssd_mixer(u, layer, prm, cols, *, prompt, nb, t, state_in=None, out_rows=None, state_prev=None):
    aliases = {}
    u_z, u_x, u_dt = u if isinstance(u, tuple) else (u, u, u)
    o_z, o_x, o_dt, w_b, n_st = cols
    n_groups = SSD_GROUPS
    hd = SSD_HEAD_DIM
    n_heads = w_b // hd
    gw = w_b // n_groups
    r = SSD_CHUNK
    sel, e, et = prm["ssd_consts"]
    o_b = o_x + w_b
    o_c = o_b + n_groups * n_st
    assert o_z % gw == 0 and o_x % gw == 0 and o_b % n_st == 0 and o_dt % LANES == 0 and n_st == LANES
    cw, cb = prm["ssd_conv_w"], prm["ssd_conv_b3"]

    if prompt:
        nc = t // r
        grid = (nb, n_groups, nc)
        rowblk = lambda b, g, c: b * nc + c
        halo = lambda b, g, c: jnp.maximum((b * t + c * r) // SUBLANES - 1, 0)
        im = lambda colf: (lambda b, g, c: (rowblk(b, g, c), colf(g)))
        hm = lambda colf: (lambda b, g, c: (halo(b, g, c), colf(g)))
        pm = lambda f: (lambda b, g, c: f(g))
        q = r
    else:
        q = SAMPLE_SEQ_ROWS
        grid = (nb * q // r, n_groups)
        im = lambda colf: (lambda i, g: (i, colf(g)))
        pm = lambda f: (lambda i, g: f(g))
    col_z = lambda g: o_z // gw + g
    col_x = lambda g: o_x // gw + g
    col_b = lambda g: o_b // n_st + g
    col_c = lambda g: o_c // n_st + g
    col_dt = lambda g: o_dt // LANES

    in_specs = [pl.BlockSpec((r, gw), im(col_z)), pl.BlockSpec((r, gw), im(col_x)),
                pl.BlockSpec((r, n_st), im(col_b)), pl.BlockSpec((r, n_st), im(col_c)),
                pl.BlockSpec((r, LANES), im(col_dt))]
    args = [u_z, u_x, u_x, u_x, u_dt]
    if prompt:
        in_specs += [pl.BlockSpec((SUBLANES, gw), hm(col_x)), pl.BlockSpec((SUBLANES, n_st), hm(col_b)),
                     pl.BlockSpec((SUBLANES, n_st), hm(col_c))]
        args += [u_x, u_x, u_x]
    kk = cw.shape[1]
    in_specs += [
        pl.BlockSpec((None, kk, gw), pm(lambda g: (layer, 0, g))),
        pl.BlockSpec((None, kk, n_st), pm(lambda g: (layer, 0, w_b // n_st + g))),
        pl.BlockSpec((None, kk, n_st), pm(lambda g: (layer, 0, w_b // n_st + n_groups + g))),
        pl.BlockSpec((None, 1, gw), pm(lambda g: (layer, 0, g))),
        pl.BlockSpec((None, 1, n_st), pm(lambda g: (layer, 0, w_b // n_st + g))),
        pl.BlockSpec((None, 1, n_st), pm(lambda g: (layer, 0, w_b // n_st + n_groups + g))),
        pl.BlockSpec((None, 1, LANES), pm(lambda g: (layer, 0, 0))),
        pl.BlockSpec((None, 1, LANES), pm(lambda g: (layer, 0, 0))),
        pl.BlockSpec((None, 1, gw), pm(lambda g: (layer, 0, g))),
        pl.BlockSpec((None, 1, gw), pm(lambda g: (layer, 0, g))),
        pl.BlockSpec((None, LANES, LANES), pm(lambda g: (g, 0, 0))),
        pl.BlockSpec((LANES, gw), pm(lambda g: (0, 0))),
        pl.BlockSpec((gw, LANES), pm(lambda g: (0, 0))),
    ]
    args += [cw, cw, cw, cb, cb, cb, prm["ssd_dt_bias3"], prm["ssd_a3"], prm["ssd_d3"], prm["ssd_norm3"],
             sel, e, et]
    if prompt:
        y_rows = out_rows
        out_specs = [pl.BlockSpec((r, gw), lambda b, g, c: (b * nc + c, g)),
                     pl.BlockSpec((None, gw, n_st), lambda b, g, c: (b, g, 0))]
        s_shape = (nb, w_b, n_st)
    else:
        nseq = r // q
        s_off = layer * (nb // nseq)
        in_specs.append(pl.BlockSpec((nseq, gw, n_st), lambda i, g: (i + s_off, g, 0)))
        args.append(state_in)
        y_rows = nb * q
        out_specs = [pl.BlockSpec((r, gw), lambda i, g: (i, g)),
                     pl.BlockSpec((nseq, gw, n_st), lambda i, g: (i + s_off, g, 0))]
        s_shape = state_in.shape
        if state_prev is not None:
            in_specs.append(pl.BlockSpec(memory_space=pl.ANY))
            args.append(state_prev)
            aliases = {len(args) - 1: 1}
    kern = functools.partial(_ssd_kernel, prompt=prompt, q=q, n_heads=n_heads, hd=hd,
                             has_prev=state_prev is not None)
    return pl.pallas_call(
        kern, grid=grid, in_specs=in_specs, out_specs=out_specs, input_output_aliases=aliases,
        out_shape=[jax.ShapeDtypeStruct((y_rows, w_b), BF16), jax.ShapeDtypeStruct(s_shape, F32)],
        scratch_shapes=[pltpu.VMEM((r + SUBLANES, gw), F32), pltpu.VMEM((r + SUBLANES, n_st), F32),
                        pltpu.VMEM((r + SUBLANES, n_st), F32), pltpu.VMEM((r, gw), F32)],
        compiler_params=_cparams(("parallel", "parallel", "arbitrary") if prompt else ("parallel", "parallel")),
        name="ssd_prompt" if prompt else "ssd_sample",
    )(*args)


def _conf_prompt_kernel(g_ref, halo_ref, w_ref, b_ref, lg_ref, lb_ref, out_ref, pad_ref, conv_ref, *, rt):
    tt, d = g_ref.shape
    halo = halo_ref.shape[0]
    kk = w_ref.shape[1]
    nlt = d // LANES
    keep = jnp.where(pl.program_id(1) == 0, 0.0, 1.0)
    for lt in range(nlt):
        cols = slice(lt * LANES, (lt + 1) * LANES)
        pad_ref[lt, 0:halo, :] = halo_ref[:, cols] * keep
        pad_ref[lt, halo:halo + tt, :] = g_ref[:, cols]
    base = halo - (kk - 1)

    def lane_tile(lt, carry):
        wt = w_ref[lt]
        bt = b_ref[lt]
        for r0 in range(0, tt, rt):
            acc = jnp.broadcast_to(bt, (rt, LANES))
            for k in range(kk):
                acc = acc + wt[k:k + 1, :] * pad_ref[lt, base + r0 + k:base + r0 + k + rt, :]
            conv_ref[lt, r0:r0 + rt, :] = acc
        return carry

    lax.fori_loop(0, nlt, lane_tile, 0)

    s1 = conv_ref[0]
    for lt in range(1, nlt):
        s1 = s1 + conv_ref[lt]
    mu = jnp.sum(s1, axis=-1, keepdims=True) * (1.0 / d)
    s2 = None
    for lt in range(nlt):
        dv = conv_ref[lt] - mu
        s2 = dv * dv if s2 is None else s2 + dv * dv
    rstd = lax.rsqrt(jnp.sum(s2, axis=-1, keepdims=True) * (1.0 / d) + LN_EPS)
    for lt in range(nlt):
        cols = slice(lt * LANES, (lt + 1) * LANES)
        v = (conv_ref[lt] - mu) * rstd * lg_ref[:, cols] + lb_ref[:, cols]
        out_ref[:, cols] = _silu(v).astype(out_ref.dtype)


def _conf_prompt(g, layer, dw_w, dw_b3, ln_g3, ln_b3, nb, t, out_rows, tt=CONF_TT, rt=CONF_RT, halo=CONF_HALO):
    d = g.shape[1]
    n_layers, kk, _ = dw_w.shape
    assert t % tt == 0 and tt % rt == 0 and halo >= kk - 1 and tt % halo == 0
    nt = t // tt
    nlt = d // LANES
    w_tiles = jnp.transpose(dw_w.reshape(n_layers, kk, nlt, LANES), (0, 2, 1, 3))
    b_tiles = dw_b3.reshape(n_layers, nlt, 1, LANES)
    par = pl.BlockSpec((None, 1, d), lambda b, i: (layer, 0, 0))
    return pl.pallas_call(
        functools.partial(_conf_prompt_kernel, rt=rt),
        grid=(nb, nt),
        in_specs=[pl.BlockSpec((tt, d), lambda b, i: (b * nt + i, 0)),
                  pl.BlockSpec((halo, d), lambda b, i: (jnp.maximum((b * t + i * tt) // halo - 1, 0), 0)),
                  pl.BlockSpec((None, nlt, kk, LANES), lambda b, i: (layer, 0, 0, 0)),
                  pl.BlockSpec((None, nlt, 1, LANES), lambda b, i: (layer, 0, 0, 0)), par, par],
        out_specs=pl.BlockSpec((tt, d), lambda b, i: (b * nt + i, 0)),
        out_shape=jax.ShapeDtypeStruct((out_rows, d), BF16),
        scratch_shapes=[pltpu.VMEM((d // LANES, halo + tt, LANES), F32), pltpu.VMEM((d // LANES, tt, LANES), F32)],
        compiler_params=_cparams(("parallel", "parallel")),
        name="conf_prompt",
    )(g, g, w_tiles, b_tiles, ln_g3, ln_b3)


def _conf_sample_kernel(ext_ref, w_ref, b_ref, lg_ref, lb_ref, out_ref):
    kk = w_ref.shape[0]
    t_new = out_ref.shape[0]
    for t in range(t_new):
        acc = b_ref[...] + w_ref[0:1, :] * ext_ref[t]
        for k in range(1, kk):
            acc = acc + w_ref[k:k + 1, :] * ext_ref[t + k]
        out_ref[t] = _silu(_ln_rows(acc, lg_ref[...], lb_ref[...])).astype(out_ref.dtype)


def _conf_sample(ext_t, layer, dw_w, dw_b3, ln_g3, ln_b3, nbt=16):
    rows, nb, d = ext_t.shape
    kk = dw_w.shape[1]
    t_new = rows - (kk - 1)
    par = pl.BlockSpec((None, 1, d), lambda i: (layer, 0, 0))
    return pl.pallas_call(
        _conf_sample_kernel,
        grid=(nb // nbt,),
        in_specs=[pl.BlockSpec((rows, nbt, d), lambda i: (0, i, 0)),
                  pl.BlockSpec((None, kk, d), lambda i: (layer, 0, 0)), par, par, par],
        out_specs=pl.BlockSpec((t_new, nbt, d), lambda i: (0, i, 0)),
        out_shape=jax.ShapeDtypeStruct((t_new, nb, d), BF16),
        compiler_params=_cparams(("parallel",)),
        name="conf_sample",
    )(ext_t, dw_w, dw_b3, ln_g3, ln_b3)


def kernel(x_prompt, x_sample, state_s5_re, state_s5_im, state_ssm, state_ssd_conv, state_conformer_conv, w_in_even, s5_lam_re, s5_lam_im, s5_log_dt, s5_b_re, s5_b_im, s5_c_re, s5_c_im, s5_d, s5_w_glu, s5_b_glu, ssd_conv_w, ssd_conv_b, ssd_dt_bias, ssd_a_log, ssd_d, ssd_norm_w, w_out_even, conf_w_pw1, conf_b_pw1, conf_dw_w, conf_dw_b, conf_ln_g, conf_ln_b, conf_w_pw2, conf_b_pw2, ln_g, ln_b, moe_w_group, moe_b_group, moe_w_expert, moe_b_expert, moe_w_gate, moe_w_up, moe_w_down):
    bsz, seq, d = x_prompt.shape
    nb_s, t_s, _ = x_sample.shape
    depth = ln_g.shape[0]
    alpha = (2.0 * depth) ** 0.25
    mp, ms = bsz * seq, nb_s * t_s
    x = jnp.concatenate([x_prompt.reshape(mp, d), x_sample.reshape(ms, d)], axis=0)
    xbf = x.astype(BF16)

    g_a = s5_lam_re.shape[1]
    w_a = s5_d.shape[1]
    w_b = ssd_norm_w.shape[1]
    n_heads = ssd_dt_bias.shape[1]
    conv_dim = ssd_conv_w.shape[2]
    n_st = (conv_dim - w_b) // (2 * SSD_GROUPS)
    o_z, o_x = w_a, w_a + w_b
    o_dt = o_x + conv_dim
    in_even = o_dt + n_heads
    cols = (o_z, o_x, o_dt, w_b, n_st)
    ssd_prm = _ssd_params(ssd_conv_w, ssd_conv_b, ssd_dt_bias, ssd_a_log, ssd_d, ssd_norm_w)
    ssm_in = state_ssm.reshape(state_ssm.shape[0] * nb_s, w_b, n_st)

    n_eg, per_g = moe_w_expert.shape[1], moe_w_expert.shape[3]
    n_route = n_eg + n_eg * per_g
    w_route = jnp.concatenate(
        [moe_w_group, jnp.transpose(moe_w_expert, (0, 2, 1, 3)).reshape(depth, d, n_eg * per_g),
         jnp.zeros((depth, d, LANES - n_route), F32)], axis=-1)
    b_route = jnp.concatenate(
        [moe_b_group, moe_b_expert.reshape(depth, n_eg * per_g), jnp.zeros((depth, LANES - n_route), F32)],
        axis=-1)[:, None, :]

    conf_dw_b3 = conf_dw_b[:, None, :]
    conf_ln_g3 = conf_ln_g[:, None, :]
    conf_ln_b3 = conf_ln_b[:, None, :]
    s5_zero = jnp.zeros((g_a // 2, bsz, 2 * s5_lam_re.shape[2]), F32)

    out = dict(re_p=[], im_p=[], ssm_p=[], sh_p=[], ch_p=[], re_s=[], im_s=[], sh_s=[], ch_s=[])
    ssm_s = None
    for layer in range(depth):
        i = layer // 2
        if layer % 2 == 0:
            u = _matmul([xbf], w_in_even, i, n_out=in_even)
            prm = (s5_lam_re[i], s5_lam_im[i], s5_log_dt[i], s5_b_re[i], s5_b_im[i], s5_c_re[i], s5_c_im[i],
                   s5_d[i])
            nk = seq // S5_CHUNK
            h, fr_p, fi_p = _s5_mixer(u, _s5_prepare(*prm, S5_CHUNK), s5_zero, s5_zero, bsz, nk,
                                      natural_rows=mp + ms)
            hc_s, fr_s, fi_s = _s5_mixer(_to_chunks(u[mp:, :w_a], nb_s, 1, t_s, g_a),
                                         _s5_prepare(*prm, t_s), _state_to_pairs(state_s5_re[i]),
                                         _state_to_pairs(state_s5_im[i]), nb_s, 1)
            h = _put_rows(h, _from_chunks(hc_s, nb_s, 1, t_s, g_a), mp)
            ya = _matmul([h], s5_w_glu, i, bias=s5_b_glu, mode="gate", gate=h, out_dtype=BF16)
            out["re_p"].append(_pairs_to_state(fr_p)); out["im_p"].append(_pairs_to_state(fi_p))
            out["re_s"].append(_pairs_to_state(fr_s)); out["im_s"].append(_pairs_to_state(fi_s))
            yb, ssm_p = _ssd_mixer(u, i, ssd_prm, cols, prompt=True, nb=bsz, t=seq, out_rows=mp + ms)
            us3 = u[mp:].reshape(nb_s, t_s, in_even)
            dead = SAMPLE_SEQ_ROWS - t_s
            hist = state_ssd_conv[i]
            rows8 = lambda parts: jnp.concatenate(parts, axis=1).reshape(nb_s * SAMPLE_SEQ_ROWS, -1)
            ext_z = rows8([jnp.zeros((nb_s, dead, w_b), F32), us3[:, :, o_z:o_z + w_b]])
            ext_x = rows8([jnp.zeros((nb_s, dead - hist.shape[1], conv_dim), F32), hist,
                           us3[:, :, o_x:o_x + conv_dim]])
            ext_dt = rows8([jnp.zeros((nb_s, dead, LANES), F32),
                            jnp.pad(us3[:, :, o_dt:], ((0, 0), (0, 0), (0, LANES - n_heads)))])
            yb_s, ssm_s = _ssd_mixer((ext_z, ext_x, ext_dt), i, ssd_prm, (0, 0, 0, w_b, n_st), prompt=False,
                                     nb=nb_s, t=t_s, state_in=ssm_in, state_prev=ssm_s)
            yb_s = yb_s.reshape(nb_s, SAMPLE_SEQ_ROWS, w_b)[:, dead:].reshape(ms, w_b)
            yb = _put_rows(yb, yb_s, mp)
            mix = _matmul([ya, yb], w_out_even, i, out_dtype=BF16)
            kh = hist.shape[1]
            out["ssm_p"].append(ssm_p.reshape(bsz, n_heads, SSD_HEAD_DIM, n_st))
            out["sh_p"].append(jnp.stack([u[(b + 1) * seq - kh:(b + 1) * seq, o_x:o_x + conv_dim]
                                          for b in range(bsz)]))
            out["sh_s"].append(jnp.concatenate([hist, us3[:, :, o_x:o_x + conv_dim]], axis=1)[:, t_s:])
        else:
            gg = _matmul([xbf], conf_w_pw1, i, bias=conf_b_pw1, mode="glu")
            c = _conf_prompt(gg, i, conf_dw_w, conf_dw_b3, conf_ln_g3, conf_ln_b3, bsz, seq, mp + ms)
            hist = state_conformer_conv[i]
            g_s = gg[mp:].reshape(nb_s, t_s, d)
            ext = jnp.concatenate([hist, g_s], axis=1)
            c_s = _conf_sample(jnp.transpose(ext, (1, 0, 2)), i, conf_dw_w, conf_dw_b3, conf_ln_g3, conf_ln_b3)
            c = _put_rows(c, jnp.transpose(c_s, (1, 0, 2)).reshape(ms, d), mp)
            mix = _matmul([c], conf_w_pw2, i, bias=conf_b_pw2, out_dtype=BF16)
            kh = hist.shape[1]
            out["ch_p"].append(jnp.stack([gg[(b + 1) * seq - kh:(b + 1) * seq] for b in range(bsz)]))
            out["ch_s"].append(ext[:, t_s:])
        x, xpk, eid, gates = _ln_router(x, mix, ln_g, ln_b, layer, 0, alpha,
                                        w_route[layer], b_route[layer], n_eg, per_g)
        yb_moe, slots = _hier_moe(xpk, eid[:, :TOP_K], moe_w_gate, moe_w_up, moe_w_down, layer)
        x, xbf = _ln_combine(x, yb_moe, slots, gates, ln_g, ln_b, layer, 1, alpha)

    st = lambda k: jnp.stack(out[k])
    return (x[:mp].reshape(bsz, seq, d), x[mp:].reshape(nb_s, t_s, d),
            st("re_p"), st("im_p"), st("ssm_p"), st("sh_p"), st("ch_p"),
            st("re_s"), st("im_s"), ssm_s.reshape(state_ssm.shape), st("sh_s"), st("ch_s"))
```

```python
import functools
import math

import jax
import jax.numpy as jnp
from jax import lax
from jax.experimental import pallas as pl
from jax.experimental.pallas import tpu as pltpu

F32 = jnp.float32
BF16 = jnp.bfloat16
HIGHEST = lax.Precision.HIGHEST

LN_EPS = 1e-5
V7X_VMEM_LIMIT_BYTES = 56 * 1024 * 1024
LANES = 128
SUBLANES = 8

S5_GROUP = 16
SSD_HEAD_DIM = 64
SSD_GROUPS = 4
SSD_CHUNK = 128
TOP_K = 2

MM_TM = 512
MM_TN = 512
LN_TM = 256
MOE_BM = 256
MOE_CAST_ROWS = 256
MOE_ROW_BUFS = 3
S5_CHUNK = 16
S5_GB = 8
S5_RELAYOUT_UNROLL = 4
CONF_TT = 256
CONF_RT = 32
CONF_HALO = 32
SAMPLE_SEQ_ROWS = 8


def _cparams(sem):
    return pltpu.CompilerParams(dimension_semantics=sem, vmem_limit_bytes=V7X_VMEM_LIMIT_BYTES)


def _dot(a, b):
    return jnp.dot(a, b, preferred_element_type=F32)


def _split3(a):
    hi = a.astype(BF16)
    r1 = a - hi.astype(F32)
    mid = r1.astype(BF16)
    lo = (r1 - mid.astype(F32)).astype(BF16)
    return hi, mid, lo


def _dot_select_lhs(a, onehot):
    hi, mid, lo = _split3(a)
    return (_dot(hi, onehot) + _dot(mid, onehot)) + _dot(lo, onehot)


def _dot_select_rhs(onehot, b):
    hi, mid, lo = _split3(b)
    return (_dot(onehot, hi) + _dot(onehot, mid)) + _dot(onehot, lo)


def _dot_nt(a, b):
    return lax.dot_general(a, b, (((1,), (1,)), ((), ())), preferred_element_type=F32)


def _dot_tn(a, b):
    return lax.dot_general(a, b, (((0,), (0,)), ((), ())), preferred_element_type=F32)


def _pack_bf16_pairs(v):
    half = v.shape[1] // 2
    lo = lax.bitcast_convert_type(v[:, :half].astype(BF16).astype(F32), jnp.uint32)
    hi = lax.bitcast_convert_type(v[:, half:].astype(BF16).astype(F32), jnp.uint32)
    return hi | (lo >> 16)


def _unpack_bf16_pairs(w):
    lo = lax.bitcast_convert_type(w << 16, F32)
    hi = lax.bitcast_convert_type(w & jnp.uint32(0xFFFF0000), F32)
    return lo, hi


def _sigmoid(x):
    return 1.0 / (1.0 + jnp.exp(-x))


def _silu(x):
    return x * _sigmoid(x)


def _mm_kernel(*refs, n_lhs, n_col, has_bias, mode):
    pos = 0
    x_refs = refs[pos:pos + n_lhs]; pos += n_lhs
    w_refs = refs[pos:pos + n_lhs * n_col]; pos += n_lhs * n_col
    b_refs = ()
    if has_bias:
        b_refs = refs[pos:pos + n_col]; pos += n_col
    gate_ref = None
    if mode == "gate":
        gate_ref = refs[pos]; pos += 1
    out_ref = refs[pos]; pos += 1
    wbf_refs = refs[pos:]

    @pl.when(pl.program_id(1) == 0)
    def _():
        for w_ref, wbf_ref in zip(w_refs, wbf_refs):
            wbf_ref[...] = w_ref[...].astype(BF16)

    accs = []
    for c in range(n_col):
        acc = None
        for l in range(n_lhs):
            part = _dot(x_refs[l][...].astype(BF16), wbf_refs[l * n_col + c][...])
            acc = part if acc is None else acc + part
        if has_bias:
            acc = acc + b_refs[c][...]
        accs.append(acc)
    if mode == "glu":
        res = accs[0] * _sigmoid(accs[1])
    elif mode == "gate":
        res = gate_ref[...] * _sigmoid(accs[0])
    else:
        res = accs[0]
    out_ref[...] = res.astype(out_ref.dtype)


def _matmul(xs, w, layer, bias=None, mode="plain", gate=None, n_out=None, out_dtype=F32,
            tm=MM_TM, tn=MM_TN):
    n_lhs = len(xs)
    m, k = xs[0].shape
    n_total = w.shape[2]
    n_col = 2 if mode == "glu" else 1
    if n_out is None:
        n_out = n_total // n_col
    assert m % tm == 0 and w.shape[1] == n_lhs * k
    nj = pl.cdiv(n_out, tn)
    glu_off = (n_total // 2) // tn if mode == "glu" else 0
    if mode == "glu":
        assert (n_total // 2) % tn == 0

    in_specs = [pl.BlockSpec((tm, k), lambda j, i: (i, 0)) for _ in range(n_lhs)]
    args = list(xs)
    for l in range(n_lhs):
        for c in range(n_col):
            in_specs.append(pl.BlockSpec((None, k, tn),
                                         functools.partial(lambda j, i, l, c: (layer, l, j + c * glu_off), l=l, c=c)))
            args.append(w)
    if bias is not None:
        b3 = bias.reshape(bias.shape[0], 1, bias.shape[1])
        for c in range(n_col):
            in_specs.append(pl.BlockSpec((None, 1, tn),
                                         functools.partial(lambda j, i, c: (layer, 0, j + c * glu_off), c=c)))
            args.append(b3)
    if mode == "gate":
        in_specs.append(pl.BlockSpec((tm, tn), lambda j, i: (i, j)))
        args.append(gate)
    kern = functools.partial(_mm_kernel, n_lhs=n_lhs, n_col=n_col, has_bias=bias is not None, mode=mode)
    return pl.pallas_call(
        kern,
        grid=(nj, m // tm),
        in_specs=in_specs,
        out_specs=pl.BlockSpec((tm, tn), lambda j, i: (i, j)),
        out_shape=jax.ShapeDtypeStruct((m, n_out), out_dtype),
        scratch_shapes=[pltpu.VMEM((k, tn), BF16) for _ in range(n_lhs * n_col)],
        compiler_params=_cparams(("arbitrary", "arbitrary")),
        name=f"mm_{mode}",
    )(*args)


def _put_rows_kernel(small_ref, big_ref, out_ref):
    del big_ref
    out_ref[...] = small_ref[...]


def _put_rows(big, small, row0):
    n, w = small.shape
    assert row0 % n == 0 and big.shape[1] == w and big.dtype == small.dtype
    return pl.pallas_call(
        _put_rows_kernel, grid=(1,),
        in_specs=[pl.BlockSpec((n, w), lambda i: (0, 0)), pl.BlockSpec(memory_space=pl.ANY)],
        out_specs=pl.BlockSpec((n, w), lambda i: (row0 // n, 0)),
        out_shape=jax.ShapeDtypeStruct(big.shape, big.dtype),
        input_output_aliases={1: 0}, compiler_params=_cparams(("arbitrary",)), name="put_rows",
    )(small, big)


def _ln_rows(y, g, b):
    mu = jnp.mean(y, axis=-1, keepdims=True)
    yc = y - mu
    var = jnp.mean(yc * yc, axis=-1, keepdims=True)
    return yc * lax.rsqrt(var + LN_EPS) * g + b


def _first_argmax(v, lane, big):
    m = jnp.max(v, axis=-1, keepdims=True)
    idx = jnp.min(jnp.where(v == m, lane, big), axis=-1, keepdims=True)
    return m, idx


def _ln_router_kernel(x_ref, mix_ref, g_ref, b_ref, wrh_ref, wrl_ref, br_ref,
                      out_ref, outpk_ref, eid_ref, gate_ref, *, alpha, n_groups, per_group):
    y = alpha * x_ref[...] + mix_ref[...].astype(F32)
    out = _ln_rows(y, g_ref[...], b_ref[...])
    out_ref[...] = out
    outpk_ref[...] = _pack_bf16_pairs(out)
    out_hi = out.astype(BF16)
    out_lo = (out - out_hi.astype(F32)).astype(BF16)
    wr_hi = wrh_ref[...]
    lg = _dot(out_hi, wr_hi) + (_dot(out_lo, wr_hi) + _dot(out_hi, wrl_ref[...])) + br_ref[...]
    lane = lax.broadcasted_iota(jnp.int32, lg.shape, 1)
    neg = jnp.float32(-jnp.inf)
    big = jnp.int32(LANES)
    gl = jnp.where(lane < n_groups, lg, neg)
    gmax, grp = _first_argmax(gl, lane, big)
    pg_top = 1.0 / jnp.sum(jnp.exp(gl - gmax), axis=-1, keepdims=True)
    lo = n_groups + grp * per_group
    el = jnp.where((lane >= lo) & (lane < lo + per_group), lg, neg)
    m1, i1 = _first_argmax(el, lane, big)
    el2 = jnp.where(lane == i1, neg, el)
    m2, i2 = _first_argmax(el2, lane, big)
    e2 = jnp.exp(m2 - m1)
    g1 = pg_top / (1.0 + e2)
    g2 = pg_top * e2 / (1.0 + e2)
    eid_ref[...] = jnp.where(lane == 0, i1 - n_groups, jnp.where(lane == 1, i2 - n_groups, 0))
    gate_ref[...] = jnp.where(lane == 0, g1, jnp.where(lane == 1, g2, 0.0))


def _ln_router(x, mix, ln_g, ln_b, layer, which, alpha, wr, br, n_groups, per_group, tm=LN_TM):
    m, d = x.shape
    assert m % tm == 0
    row = pl.BlockSpec((tm, d), lambda i: (i, 0))
    par = pl.BlockSpec((None, None, 1, d), lambda i: (layer, which, 0, 0))
    small = pl.BlockSpec((tm, LANES), lambda i: (i, 0))
    wspec = pl.BlockSpec((d, LANES), lambda i: (0, 0))
    g4 = ln_g.reshape(ln_g.shape[0], ln_g.shape[1], 1, d)
    b4 = ln_b.reshape(ln_b.shape[0], ln_b.shape[1], 1, d)
    wr_hi = wr.astype(BF16)
    wr_lo = (wr - wr_hi.astype(F32)).astype(BF16)
    kern = functools.partial(_ln_router_kernel, alpha=alpha, n_groups=n_groups, per_group=per_group)
    return pl.pallas_call(
        kern, grid=(m // tm,),
        in_specs=[row, row, par, par, wspec, wspec, pl.BlockSpec((1, LANES), lambda i: (0, 0))],
        out_specs=[row, pl.BlockSpec((tm, d // 2), lambda i: (i, 0)), small, small],
        out_shape=[jax.ShapeDtypeStruct((m, d), F32), jax.ShapeDtypeStruct((m, d // 2), jnp.uint32),
                   jax.ShapeDtypeStruct((m, LANES), jnp.int32), jax.ShapeDtypeStruct((m, LANES), F32)],
        compiler_params=_cparams(("parallel",)), name="ln_router",
    )(x, mix, g4, b4, wr_hi, wr_lo, br)


def _pair_row_copy(y_hbm, buf_ref, sem_ref, slot, k, src_row, dst_row, n_rows):
    return pltpu.make_async_copy(y_hbm.at[pl.ds(src_row, n_rows), :],
                                 buf_ref.at[slot, k, pl.ds(dst_row, n_rows), :], sem_ref.at[slot])


def _ln_combine_kernel(slot_ref, x_ref, gate_ref, y_hbm, g_ref, b_ref, out_ref, outbf_ref, buf_ref, sem_ref, *,
                       alpha):
    i = pl.program_id(0)
    tm = x_ref.shape[0]
    cur = i % 2

    def issue(tile, dst):
        base = tile * (tm * TOP_K)

        def body(r, carry):
            for k in range(TOP_K):
                _pair_row_copy(y_hbm, buf_ref, sem_ref, dst, k, slot_ref[base + TOP_K * r + k], r, 1).start()
            return carry

        lax.fori_loop(0, tm, body, 0, unroll=4)

    @pl.when(i == 0)
    def _():
        issue(0, 0)

    @pl.when(i + 1 < pl.num_programs(0))
    def _():
        base = (i + 1) * (tm * TOP_K)
        for r in range(tm):
            for k in range(TOP_K):
                _pair_row_copy(y_hbm, buf_ref, sem_ref, 1 - cur, k, slot_ref[base + TOP_K * r + k], r, 1).start()

    for k in range(TOP_K):
        _pair_row_copy(y_hbm, buf_ref, sem_ref, cur, k, 0, 0, tm).wait()
    gates = gate_ref[...]
    lo0, hi0 = _unpack_bf16_pairs(buf_ref[cur, 0])
    lo1, hi1 = _unpack_bf16_pairs(buf_ref[cur, 1])
    g0, g1 = gates[:, 0:1], gates[:, 1:2]
    y = alpha * x_ref[...] + jnp.concatenate([lo0 * g0 + lo1 * g1, hi0 * g0 + hi1 * g1], axis=1)
    out = _ln_rows(y, g_ref[...], b_ref[...])
    out_ref[...] = out
    outbf_ref[...] = out.astype(BF16)


def _ln_combine(x, yb, slot_flat, gates, ln_g, ln_b, layer, which, alpha, tm=LN_TM):
    m, d = x.shape
    assert m % tm == 0 and TOP_K == 2
    row = pl.BlockSpec((tm, d), lambda i, s: (i, 0))
    gate_spec = pl.BlockSpec((tm, LANES), lambda i, s: (i, 0))
    par = pl.BlockSpec((None, None, 1, d), lambda i, s: (layer, which, 0, 0))
    g4 = ln_g.reshape(ln_g.shape[0], ln_g.shape[1], 1, d)
    b4 = ln_b.reshape(ln_b.shape[0], ln_b.shape[1], 1, d)
    return pl.pallas_call(
        functools.partial(_ln_combine_kernel, alpha=alpha),
        grid_spec=pltpu.PrefetchScalarGridSpec(
            num_scalar_prefetch=1, grid=(m // tm,),
            in_specs=[row, gate_spec, pl.BlockSpec(memory_space=pl.ANY), par, par],
            out_specs=[row, row],
            scratch_shapes=[pltpu.VMEM((2, TOP_K, tm, d // 2), jnp.uint32), pltpu.SemaphoreType.DMA((2,))]),
        out_shape=[jax.ShapeDtypeStruct((m, d), F32), jax.ShapeDtypeStruct((m, d), BF16)],
        compiler_params=pltpu.CompilerParams(dimension_semantics=("arbitrary",),
                                             vmem_limit_bytes=V7X_VMEM_LIMIT_BYTES,
                                             disable_bounds_checks=True),
        name="ln_combine",
    )(slot_flat, x, gates, yb, g4, b4)


def _moe_row_copy(x_hbm, xbuf_ref, sem_ref, slot, src_row, dst_row, n_rows):
    return pltpu.make_async_copy(x_hbm.at[pl.ds(src_row, n_rows), :],
                                 xbuf_ref.at[slot, pl.ds(dst_row, n_rows), :], sem_ref.at[slot])


def _expert_weight_stream(i, n_used, be_ref, nxt_ref, w_hbms, wbuf_ref, wsem_ref, layer, on_arrival):
    def copies(expert):
        return [pltpu.make_async_copy(w.at[layer, expert], wbuf_ref.at[k], wsem_ref.at[0])
                for k, w in enumerate(w_hbms)]

    @pl.when(i == 0)
    def _():
        for c in copies(be_ref[0]):
            c.start()

    prev = be_ref[jnp.maximum(i - 1, 0)]

    @pl.when((i == 0) | ((i < n_used) & (be_ref[i] != prev)))
    def _():
        for c in copies(be_ref[i]):
            c.wait()
        on_arrival()
        nxt = nxt_ref[i]

        @pl.when(nxt >= 0)
        def _():
            for c in copies(nxt):
                c.start()


def _moe_up_kernel(be_ref, nxt_ref, nu_ref, tok_ref, x_hbm, wg_hbm, wu_hbm, h_ref,
                   wbuf_ref, wgbf_ref, wubf_ref, xbuf_ref, sem_ref, wsem_ref, *, layer):
    i = pl.program_id(0)
    n_used = nu_ref[0]
    n_buf, bm = xbuf_ref.shape[0], xbuf_ref.shape[1]
    depth = n_buf - 1
    slot = lax.rem(i, n_buf)

    def start_row(blk, r):
        _moe_row_copy(x_hbm, xbuf_ref, sem_ref, lax.rem(blk, n_buf), tok_ref[blk * bm + r], r, 1).start()

    @pl.when(i == 0)
    def _():
        def body(k, carry):
            start_row(k // bm, lax.rem(k, bm))
            return carry

        lax.fori_loop(0, jnp.minimum(depth, n_used) * bm, body, 0)

    @pl.when(i + depth < n_used)
    def _():
        for r in range(bm):
            start_row(i + depth, r)

    def cast_weights():
        rows = wgbf_ref.shape[0]
        for r0 in range(0, rows, MOE_CAST_ROWS):
            sl = slice(r0, r0 + MOE_CAST_ROWS)
            wgbf_ref[sl, :] = wbuf_ref[0, sl, :].astype(BF16)
            wubf_ref[sl, :] = wbuf_ref[1, sl, :].astype(BF16)

    _expert_weight_stream(i, n_used, be_ref, nxt_ref, (wg_hbm, wu_hbm), wbuf_ref, wsem_ref, layer, cast_weights)

    @pl.when(i < n_used)
    def _():
        _moe_row_copy(x_hbm, xbuf_ref, sem_ref, slot, 0, 0, bm).wait()
        x = jnp.concatenate(_unpack_bf16_pairs(xbuf_ref[slot]), axis=1).astype(BF16)
        h = _silu(_dot(x, wgbf_ref[...])) * _dot(x, wubf_ref[...])
        h_ref[...] = h.astype(BF16)

    @pl.when(i >= n_used)
    def _():
        h_ref[...] = jnp.zeros(h_ref.shape, BF16)


def _moe_down_kernel(be_ref, nxt_ref, nu_ref, h_ref, wd_hbm, out_ref, wbuf_ref, wdbf_ref, wsem_ref, *, layer):
    i = pl.program_id(0)

    def cast_weights():
        wdbf_ref[...] = wbuf_ref[0].astype(BF16)

    _expert_weight_stream(i, nu_ref[0], be_ref, nxt_ref, (wd_hbm,), wbuf_ref, wsem_ref, layer, cast_weights)

    @pl.when(i < nu_ref[0])
    def _():
        out_ref[...] = _pack_bf16_pairs(_dot(h_ref[...], wdbf_ref[...]))

    @pl.when(i >= nu_ref[0])
    def _():
        out_ref[...] = jnp.zeros(out_ref.shape, jnp.uint32)


def _moe_experts(xpk, tok_buf, blk_expert, blk_next, n_used, w_gate, w_up, w_down, layer, bm=MOE_BM):
    cap = tok_buf.shape[0]
    d = w_gate.shape[2]
    f = w_gate.shape[3]
    n_blocks = cap // bm
    anyspace = pl.BlockSpec(memory_space=pl.ANY)
    gather_params = pltpu.CompilerParams(dimension_semantics=("arbitrary",),
                                         vmem_limit_bytes=V7X_VMEM_LIMIT_BYTES, disable_bounds_checks=True)
    hidden = pl.pallas_call(
        functools.partial(_moe_up_kernel, layer=layer),
        grid_spec=pltpu.PrefetchScalarGridSpec(
            num_scalar_prefetch=4,
            grid=(n_blocks,),
            in_specs=[anyspace, anyspace, anyspace],
            out_specs=pl.BlockSpec((bm, f), lambda i, be, nx, nu, tok: (i, 0)),
            scratch_shapes=[pltpu.VMEM((2, d, f), F32), pltpu.VMEM((d, f), BF16), pltpu.VMEM((d, f), BF16),
                            pltpu.VMEM((MOE_ROW_BUFS, bm, d // 2), jnp.uint32),
                            pltpu.SemaphoreType.DMA((MOE_ROW_BUFS,)), pltpu.SemaphoreType.DMA((1,))],
        ),
        out_shape=jax.ShapeDtypeStruct((cap, f), BF16),
        compiler_params=gather_params,
        name="moe_up",
    )(blk_expert, blk_next, n_used, tok_buf, xpk, w_gate, w_up)
    return pl.pallas_call(
        functools.partial(_moe_down_kernel, layer=layer),
        grid_spec=pltpu.PrefetchScalarGridSpec(
            num_scalar_prefetch=3,
            grid=(n_blocks,),
            in_specs=[pl.BlockSpec((bm, f), lambda i, be, nx, nu: (jnp.minimum(i, nu[0] - 1), 0)), anyspace],
            out_specs=pl.BlockSpec((bm, d // 2), lambda i, be, nx, nu: (i, 0)),
            scratch_shapes=[pltpu.VMEM((1, f, d), F32), pltpu.VMEM((f, d), BF16),
                            pltpu.SemaphoreType.DMA((1,))],
        ),
        out_shape=jax.ShapeDtypeStruct((cap, d // 2), jnp.uint32),
        compiler_params=_cparams(("arbitrary",)),
        name="moe_down",
    )(blk_expert, blk_next, n_used, hidden, w_down)


def _moe_dispatch(eid, n_experts, bm):
    t = eid.shape[0]
    n_assign = t * TOP_K
    flat_e = eid.reshape(-1)
    ids = jnp.arange(n_experts, dtype=jnp.int32)
    onehot = (flat_e[:, None] == ids[None, :]).astype(jnp.int32)
    csum = jnp.cumsum(onehot, axis=0)
    counts = csum[-1]
    padded = (counts + bm - 1) // bm * bm
    pad_end = jnp.cumsum(padded)
    pad_start = pad_end - padded
    slot = jnp.sum(onehot * (csum - 1 + pad_start[None, :]), axis=1).astype(jnp.int32)
    n_blocks = -(-n_assign // bm) + n_experts
    cap = n_blocks * bm
    tok_buf = jnp.zeros((cap,), jnp.int32).at[slot].set(jnp.arange(n_assign, dtype=jnp.int32) // TOP_K)
    blk_start = jnp.arange(n_blocks, dtype=jnp.int32) * bm
    blk_expert = jnp.minimum(jnp.sum((pad_end[None, :] <= blk_start[:, None]).astype(jnp.int32), axis=1),
                             n_experts - 1).astype(jnp.int32)
    n_used = (pad_end[-1] // bm).astype(jnp.int32).reshape(1)
    at_or_after = lax.cummin(jnp.where(counts > 0, ids, n_experts), axis=0, reverse=True)
    after = jnp.concatenate([at_or_after[1:], jnp.full((1,), n_experts, jnp.int32)])
    after = jnp.where(after < n_experts, after, -1)
    blk_next = jnp.sum(jnp.where(blk_expert[:, None] == ids[None, :], after[None, :], 0), axis=1).astype(jnp.int32)
    return slot, tok_buf, blk_expert, blk_next, n_used


def _hier_moe(x, eid, w_gate, w_up, w_down, layer):
    n_experts = w_gate.shape[1]
    slot, tok_buf, blk_expert, blk_next, n_used = _moe_dispatch(eid, n_experts, MOE_BM)
    return _moe_experts(x, tok_buf, blk_expert, blk_next, n_used, w_gate, w_up, w_down, layer), slot


def _s5_prepare(lam_re, lam_im, log_dt, b_re, b_im, c_re, c_im, d_skip, lc):
    g, p, c = b_re.shape
    dt = jnp.exp(log_dt.astype(F32))[:, None]
    lr, li = lam_re.astype(F32), lam_im.astype(F32)
    mag = jnp.exp(lr * dt)
    ab_re, ab_im = mag * jnp.cos(li * dt), mag * jnp.sin(li * dt)
    den = lr * lr + li * li
    q_re = ((ab_re - 1.0) * lr + ab_im * li) / den
    q_im = (ab_im * lr - (ab_re - 1.0) * li) / den
    bb_re = q_re[..., None] * b_re - q_im[..., None] * b_im
    bb_im = q_re[..., None] * b_im + q_im[..., None] * b_re
    ks = jnp.arange(lc + 1, dtype=F32)[:, None, None]
    pmag = jnp.exp(lr * dt * ks)
    pw_re, pw_im = pmag * jnp.cos(li * dt * ks), pmag * jnp.sin(li * dt * ks)
    t_re = pw_re[:lc, :, :, None] * bb_re[None] - pw_im[:lc, :, :, None] * bb_im[None]
    t_im = pw_re[:lc, :, :, None] * bb_im[None] + pw_im[:lc, :, :, None] * bb_re[None]
    kt = (jnp.einsum("gop,kgpi->gkoi", c_re, t_re, precision=HIGHEST)
          - jnp.einsum("gop,kgpi->gkoi", c_im, t_im, precision=HIGHEST))
    padk = jnp.pad(kt, ((0, 0), (lc, 1), (0, 0), (0, 0)))
    toe = jnp.tile(padk, (1, lc, 1, 1))[:, :2 * lc * lc].reshape(g, lc, 2 * lc, c, c)[:, :, lc:]
    m_intra = jnp.transpose(toe, (0, 1, 4, 2, 3)).reshape(g, lc * c, lc * c)
    rev = pw_re[:lc][::-1], pw_im[:lc][::-1]
    wst_re = rev[0][:, :, :, None] * bb_re[None] - rev[1][:, :, :, None] * bb_im[None]
    wst_im = rev[0][:, :, :, None] * bb_im[None] + rev[1][:, :, :, None] * bb_re[None]
    wst_re = jnp.transpose(wst_re, (1, 0, 3, 2)).reshape(g, lc * c, p)
    wst_im = jnp.transpose(wst_im, (1, 0, 3, 2)).reshape(g, lc * c, p)
    half = (jnp.arange(g) % 2)[:, None] == jnp.arange(2)[None, :]

    def place_cols(wm):
        return jnp.where(half[:, None, :, None], wm[:, :, None, :], 0.0).reshape(g, lc * c, 2 * p)

    wy_re = c_re[:, None] * pw_re[1:].transpose(1, 0, 2)[:, :, None, :] \
        - c_im[:, None] * pw_im[1:].transpose(1, 0, 2)[:, :, None, :]
    wy_im = -(c_re[:, None] * pw_im[1:].transpose(1, 0, 2)[:, :, None, :]
              + c_im[:, None] * pw_re[1:].transpose(1, 0, 2)[:, :, None, :])
    wy_re = jnp.transpose(wy_re, (0, 3, 1, 2)).reshape(g, p, lc * c)
    wy_im = jnp.transpose(wy_im, (0, 3, 1, 2)).reshape(g, p, lc * c)

    def place_rows(wm):
        return jnp.where(half[:, :, None, None], wm[:, None, :, :], 0.0).reshape(g, 2 * p, lc * c)

    al_re = pw_re[lc].reshape(g // 2, 1, 2 * p)
    al_im = pw_im[lc].reshape(g // 2, 1, 2 * p)
    d_ch = jnp.tile(d_skip.astype(F32).reshape(g, 1, c), (1, lc, 1)).reshape(g, 1, lc * c)
    return dict(m=m_intra.astype(BF16), wst_re=place_cols(wst_re).astype(BF16),
                wst_im=place_cols(wst_im).astype(BF16), wy_re=place_rows(wy_re).astype(BF16),
                wy_im=place_rows(wy_im).astype(BF16), al_re=al_re, al_im=al_im, d=d_ch)


def _block_transpose8(vs, blk):
    vs = list(vs)
    lane_blk = lax.broadcasted_iota(jnp.int32, vs[0].shape, 1) // blk
    for s in (4, 2, 1):
        low = (lane_blk & s) == 0
        for i in range(8):
            if i & s:
                continue
            a, b = vs[i], vs[i + s]
            vs[i] = jnp.where(low, a, pltpu.roll(b, s * blk, axis=1))
            vs[i + s] = jnp.where(low, pltpu.roll(a, LANES - s * blk, axis=1), b)
    return vs


def _rows_to_chunks(u_ref, uc_ref, lc):
    gb, rows, w = uc_ref.shape
    blk = w // lc

    def body(rb, carry):
        r0 = pl.multiple_of(rb * SUBLANES, SUBLANES)
        for half in range(lc // 8):
            vs = [u_ref[pl.ds(r0 * lc + half * 8 + l, SUBLANES, stride=lc), :] for l in range(8)]
            ws = _block_transpose8(vs, blk)
            for g in range(gb):
                uc_ref[g, pl.ds(r0, SUBLANES), half * LANES:(half + 1) * LANES] = ws[g]
        return carry

    lax.fori_loop(0, rows // SUBLANES, body, 0, unroll=S5_RELAYOUT_UNROLL)


def _chunks_to_rows(yc_ref, h_ref, lc):
    gb, rows, w = yc_ref.shape
    blk = w // lc

    def body(rb, carry):
        r0 = pl.multiple_of(rb * SUBLANES, SUBLANES)
        for half in range(lc // 8):
            ws = [yc_ref[g, pl.ds(r0, SUBLANES), half * LANES:(half + 1) * LANES] for g in range(gb)]
            vs = _block_transpose8(ws, blk)
            for l in range(8):
                h_ref[pl.ds(r0 * lc + half * 8 + l, SUBLANES, stride=lc), :] = vs[l]
        return carry

    lax.fori_loop(0, rows // SUBLANES, body, 0, unroll=S5_RELAYOUT_UNROLL)


def _s5_kernel(*refs, gb, nb, nk, lc, natural):
    (u_ref, m_ref, wsr_ref, wsi_ref, wyr_ref, wyi_ref, alr_ref, ali_ref, d_ref, s0r_ref, s0i_ref,
     h_ref, sfr_ref, sfi_ref, locr_ref, loci_ref, str_ref, sti_ref) = refs[:18]
    if natural:
        uc_ref, yc_ref = refs[18:]
        _rows_to_chunks(u_ref, uc_ref, lc)
    else:
        uc_ref, yc_ref = u_ref, h_ref
    for pair in range(gb // 2):
        g0, g1 = 2 * pair, 2 * pair + 1
        u0 = uc_ref[g0]
        u1 = uc_ref[g1]
        ub0 = u0.astype(BF16)
        ub1 = u1.astype(BF16)
        loc_re = _dot(ub0, wsr_ref[g0]) + _dot(ub1, wsr_ref[g1])
        loc_im = _dot(ub0, wsi_ref[g0]) + _dot(ub1, wsi_ref[g1])
        a_re = alr_ref[pair]
        a_im = ali_ref[pair]
        s_re = s0r_ref[pair]
        s_im = s0i_ref[pair]
        if nk == 1:
            st_re, st_im = s_re, s_im
            f_re = a_re * s_re - a_im * s_im + loc_re
            f_im = a_re * s_im + a_im * s_re + loc_im
        else:
            locr_ref[...] = loc_re
            loci_ref[...] = loc_im

            def step(k, carry):
                c_re, c_im = carry
                rows = pl.ds(k, nb, stride=nk)
                str_ref[rows, :] = c_re
                sti_ref[rows, :] = c_im
                l_re = locr_ref[rows, :]
                l_im = loci_ref[rows, :]
                return (a_re * c_re - a_im * c_im + l_re, a_re * c_im + a_im * c_re + l_im)

            f_re, f_im = lax.fori_loop(0, nk, step, (s_re, s_im))
            st_re = str_ref[...]
            st_im = sti_ref[...]
        sfr_ref[pair] = f_re
        sfi_ref[pair] = f_im
        sb_re = st_re.astype(BF16)
        sb_im = st_im.astype(BF16)
        for gi, u, ub in ((g0, u0, ub0), (g1, u1, ub1)):
            y = (_dot(ub, m_ref[gi]) + _dot(sb_re, wyr_ref[gi]) + _dot(sb_im, wyi_ref[gi])
                 + d_ref[gi] * u)
            yc_ref[gi] = jax.nn.gelu(y)
    if natural:
        _chunks_to_rows(yc_ref, h_ref, lc)


def _s5_mixer(u, prep, s0_re, s0_im, nb, nk, gb=S5_GB, natural_rows=None):
    g, w, _ = prep["m"].shape
    p2 = prep["al_re"].shape[2]
    r = nb * nk
    natural = natural_rows is not None
    grp = lambda *shape: pl.BlockSpec((gb,) + shape, lambda i: (i, 0, 0))
    pr = lambda *shape: pl.BlockSpec((gb // 2,) + shape, lambda i: (i, 0, 0))
    scratch = [pltpu.VMEM((r, p2), F32) for _ in range(4)]
    if natural:
        lc = w // S5_GROUP
        assert gb == 8 and gb * S5_GROUP == LANES and lc % 8 == 0 and r % SUBLANES == 0
        io_spec = pl.BlockSpec((r * lc, LANES), lambda i: (0, i))
        out_h = jax.ShapeDtypeStruct((natural_rows, g * S5_GROUP), F32)
        scratch += [pltpu.VMEM((gb, r, w), F32), pltpu.VMEM((gb, r, w), F32)]
    else:
        lc = 0
        io_spec = grp(r, w)
        out_h = jax.ShapeDtypeStruct((g, r, w), F32)
    kern = functools.partial(_s5_kernel, gb=gb, nb=nb, nk=nk, lc=lc, natural=natural)
    return pl.pallas_call(
        kern,
        grid=(g // gb,),
        in_specs=[io_spec, grp(w, w), grp(w, p2), grp(w, p2), grp(p2, w), grp(p2, w),
                  pr(1, p2), pr(1, p2), grp(1, w), pr(nb, p2), pr(nb, p2)],
        out_specs=[io_spec, pr(nb, p2), pr(nb, p2)],
        out_shape=[out_h,
                   jax.ShapeDtypeStruct((g // 2, nb, p2), F32),
                   jax.ShapeDtypeStruct((g // 2, nb, p2), F32)],
        scratch_shapes=scratch,
        compiler_params=_cparams(("parallel",)),
        name="s5_prompt" if natural else "s5_sample",
    )(u, prep["m"], prep["wst_re"], prep["wst_im"], prep["wy_re"], prep["wy_im"],
      prep["al_re"], prep["al_im"], prep["d"], s0_re, s0_im)


def _to_chunks(u2d, nb, nk, lc, g):
    c = u2d.shape[1] // g
    u5 = u2d.reshape(nb, nk, lc, g, c)
    return jnp.transpose(u5, (3, 0, 1, 2, 4)).reshape(g, nb * nk, lc * c)


def _from_chunks(hc, nb, nk, lc, g):
    c = hc.shape[2] // lc
    h5 = hc.reshape(g, nb, nk, lc, c)
    return jnp.transpose(h5, (1, 2, 3, 0, 4)).reshape(nb * nk * lc, g * c)


def _state_to_pairs(s):
    b, g, p = s.shape
    return jnp.transpose(s.reshape(b, g // 2, 2 * p), (1, 0, 2))


def _pairs_to_state(f):
    g2, b, p2 = f.shape
    return jnp.transpose(f, (1, 0, 2)).reshape(b, g2 * 2, p2 // 2)


def _softplus(x):
    return jnp.maximum(x, 0.0) + jnp.log1p(jnp.exp(-jnp.abs(x)))


def _ssd_kernel(*refs, prompt, q, n_heads, hd, has_prev):
    it = iter(refs)
    z_ref, xs_ref, bm_ref, cm_ref, dt_ref = (next(it) for _ in range(5))
    if prompt:
        hx_ref, hb_ref, hc_ref = (next(it) for _ in range(3))
    cwx_ref, cwb_ref, cwc_ref, cbx_ref, cbb_ref, cbc_ref = (next(it) for _ in range(6))
    dtb_ref, a_ref, dsk_ref, nw_ref, sel_ref, e_ref, et_ref = (next(it) for _ in range(7))
    if not prompt:
        sin_ref = next(it)
    if has_prev:
        next(it)
    y_ref, sout_ref = next(it), next(it)
    padx_ref, padb_ref, padc_ref, yacc_ref = (next(it) for _ in range(4))

    r = z_ref.shape[0]
    nseq = r // q
    hpg = xs_ref.shape[1] // hd
    row = lax.broadcasted_iota(jnp.int32, (r, 1), 0)
    if prompt:
        first = pl.program_id(2) == 0
        keep = jnp.where(first, 0.0, 1.0)
        live = None
    else:
        live = (row % q) >= (q // 2)

    def conv(x_ref, halo_ref, pad_ref, w_ref, b_ref):
        if prompt:
            pad_ref[0:SUBLANES, :] = halo_ref[...] * keep
        else:
            pad_ref[0:SUBLANES, :] = jnp.zeros((SUBLANES, pad_ref.shape[1]), F32)
        pad_ref[SUBLANES:SUBLANES + r, :] = x_ref[...]
        kk = w_ref.shape[0]
        acc = b_ref[...]
        for k in range(kk):
            acc = acc + w_ref[k:k + 1, :] * pad_ref[pl.ds(SUBLANES - (kk - 1) + k, r), :]
        return _silu(acc)

    xs = conv(xs_ref, hx_ref if prompt else None, padx_ref, cwx_ref, cbx_ref)
    bc = conv(bm_ref, hb_ref if prompt else None, padb_ref, cwb_ref, cbb_ref)
    cc = conv(cm_ref, hc_ref if prompt else None, padc_ref, cwc_ref, cbc_ref)

    lane = lax.broadcasted_iota(jnp.int32, (r, LANES), 1)
    dtv = jnp.where(lane < n_heads, _softplus(dt_ref[...] + dtb_ref[...]), 0.0)
    if not prompt:
        dtv = jnp.where(live, dtv, 0.0)
    sel = sel_ref[...]
    both = _dot_select_lhs(jnp.concatenate([dtv, dtv * a_ref[...]], axis=0), sel)
    dt8, adt8 = both[0:r], both[r:2 * r]

    ri = lax.broadcasted_iota(jnp.int32, (r, r), 0)
    ci = lax.broadcasted_iota(jnp.int32, (r, r), 1)
    same = (ri // q) == (ci // q)
    tri = same & (ci <= ri)
    masks = jnp.concatenate([jnp.where(tri, 1.0, 0.0), jnp.where(same, 1.0, 0.0)], axis=0).astype(BF16)
    sums = _dot_select_rhs(masks, adt8)
    acum, atot = sums[0:r], sums[r:2 * r]
    acum_t = acum.T
    expand = _dot_select_lhs(jnp.concatenate([jnp.exp(acum), dt8, jnp.exp(atot - acum)], axis=0), e_ref[...])
    eac_x, dt_x, dte_x = expand[0:r], expand[r:2 * r], expand[2 * r:3 * r]
    dec = _dot_select_rhs(et_ref[...], jnp.exp(atot).T)

    cb = _dot_nt(cc.astype(BF16), bc.astype(BF16))
    xdt = xs * dt_x
    neg = jnp.float32(-jnp.inf)
    lane_hd = lax.broadcasted_iota(jnp.int32, (r, 2 * hd), 1)
    for pair in range(hpg // 2):
        xp = xdt[:, pair * 2 * hd:(pair + 1) * 2 * hd]
        acc = None
        for half in range(2):
            hl = 2 * pair + half
            seg = acum[:, hl:hl + 1] - acum_t[hl:hl + 1, :]
            scores = cb * jnp.exp(jnp.where(tri, seg, neg))
            mask = (lane_hd < hd) if half == 0 else (lane_hd >= hd)
            part = _dot(scores.astype(BF16), jnp.where(mask, xp, 0.0).astype(BF16))
            acc = part if acc is None else acc + part
        yacc_ref[:, pair * 2 * hd:(pair + 1) * 2 * hd] = acc

    wgt = (xdt * dte_x).astype(BF16)
    bcb = bc.astype(BF16)
    ccb = cc.astype(BF16)
    if prompt:
        @pl.when(first)
        def _():
            sout_ref[...] = jnp.zeros(sout_ref.shape, F32)

        s_old = sout_ref[...]
        yacc_ref[...] += _dot_nt(ccb, s_old.astype(BF16)) * eac_x
        sout_ref[...] = dec[:, 0:1] * s_old + _dot_tn(wgt, bcb)
    else:
        for j in range(nseq):
            rows = slice(j * q, (j + 1) * q)
            s_old = sin_ref[j]
            yacc_ref[rows, :] += _dot_nt(ccb[rows], s_old.astype(BF16)) * eac_x[rows]
            wj = jnp.where((row // q) == j, wgt, jnp.zeros_like(wgt))
            sout_ref[j] = dec[:, j * q:j * q + 1] * s_old + _dot_tn(wj, bcb)

    y = yacc_ref[...] + dsk_ref[...] * xs
    y = y * _silu(z_ref[...])
    y = y * lax.rsqrt(jnp.mean(y * y, axis=-1, keepdims=True) + LN_EPS)
    y_ref[...] = (y * nw_ref[...]).astype(y_ref.dtype)


def _ssd_constants(n_heads, hd, n_groups):
    hpg = n_heads // n_groups
    lane = jnp.arange(LANES)
    sel = (lane[None, :, None] == (jnp.arange(n_groups)[:, None, None] * hpg + lane[None, None, :])) \
        & (lane[None, None, :] < hpg)
    e = (lane[:, None] == (jnp.arange(hpg * hd) // hd)[None, :])
    return sel.astype(BF16), e.astype(BF16), e.T.astype(BF16)


def _pad_lanes(v):
    return jnp.pad(v, ((0, 0), (0, LANES - v.shape[1])))[:, None, :]


def _ssd_params(conv_w, conv_b, dt_bias, a_log, d_skip, norm_w):
    n_heads = dt_bias.shape[1]
    hd = norm_w.shape[1] // n_heads
    return dict(
        ssd_conv_w=conv_w, ssd_conv_b3=conv_b[:, None, :],
        ssd_dt_bias3=_pad_lanes(dt_bias.astype(F32)),
        ssd_a3=_pad_lanes(-jnp.exp(a_log.astype(F32))),
        ssd_d3=jnp.repeat(d_skip.astype(F32), hd, axis=1)[:, None, :],
        ssd_norm3=norm_w[:, None, :],
        ssd_consts=_ssd_constants(n_heads, hd, SSD_GROUPS))


def _ssd_mixer(u, layer, prm, cols, *, prompt, nb, t, state_in=None, out_rows=None, state_prev=None):
    aliases = {}
    u_z, u_x, u_dt = u if isinstance(u, tuple) else (u, u, u)
    o_z, o_x, o_dt, w_b, n_st = cols
    n_groups = SSD_GROUPS
    hd = SSD_HEAD_DIM
    n_heads = w_b // hd
    gw = w_b // n_groups
    r = SSD_CHUNK
    sel, e, et = prm["ssd_consts"]
    o_b = o_x + w_b
    o_c = o_b + n_groups * n_st
    assert o_z % gw == 0 and o_x % gw == 0 and o_b % n_st == 0 and o_dt % LANES == 0 and n_st == LANES
    cw, cb = prm["ssd_conv_w"], prm["ssd_conv_b3"]

    if prompt:
        nc = t // r
        grid = (nb, n_groups, nc)
        rowblk = lambda b, g, c: b * nc + c
        halo = lambda b, g, c: jnp.maximum((b * t + c * r) // SUBLANES - 1, 0)
        im = lambda colf: (lambda b, g, c: (rowblk(b, g, c), colf(g)))
        hm = lambda colf: (lambda b, g, c: (halo(b, g, c), colf(g)))
        pm = lambda f: (lambda b, g, c: f(g))
        q = r
    else:
        q = SAMPLE_SEQ_ROWS
        grid = (nb * q // r, n_groups)
        im = lambda colf: (lambda i, g: (i, colf(g)))
        pm = lambda f: (lambda i, g: f(g))
    col_z = lambda g: o_z // gw + g
    col_x = lambda g: o_x // gw + g
    col_b = lambda g: o_b // n_st + g
    col_c = lambda g: o_c // n_st + g
    col_dt = lambda g: o_dt // LANES

    in_specs = [pl.BlockSpec((r, gw), im(col_z)), pl.BlockSpec((r, gw), im(col_x)),
                pl.BlockSpec((r, n_st), im(col_b)), pl.BlockSpec((r, n_st), im(col_c)),
                pl.BlockSpec((r, LANES), im(col_dt))]
    args = [u_z, u_x, u_x, u_x, u_dt]
    if prompt:
        in_specs += [pl.BlockSpec((SUBLANES, gw), hm(col_x)), pl.BlockSpec((SUBLANES, n_st), hm(col_b)),
                     pl.BlockSpec((SUBLANES, n_st), hm(col_c))]
        args += [u_x, u_x, u_x]
    kk = cw.shape[1]
    in_specs += [
        pl.BlockSpec((None, kk, gw), pm(lambda g: (layer, 0, g))),
        pl.BlockSpec((None, kk, n_st), pm(lambda g: (layer, 0, w_b // n_st + g))),
        pl.BlockSpec((None, kk, n_st), pm(lambda g: (layer, 0, w_b // n_st + n_groups + g))),
        pl.BlockSpec((None, 1, gw), pm(lambda g: (layer, 0, g))),
        pl.BlockSpec((None, 1, n_st), pm(lambda g: (layer, 0, w_b // n_st + g))),
        pl.BlockSpec((None, 1, n_st), pm(lambda g: (layer, 0, w_b // n_st + n_groups + g))),
        pl.BlockSpec((None, 1, LANES), pm(lambda g: (layer, 0, 0))),
        pl.BlockSpec((None, 1, LANES), pm(lambda g: (layer, 0, 0))),
        pl.BlockSpec((None, 1, gw), pm(lambda g: (layer, 0, g))),
        pl.BlockSpec((None, 1, gw), pm(lambda g: (layer, 0, g))),
        pl.BlockSpec((None, LANES, LANES), pm(lambda g: (g, 0, 0))),
        pl.BlockSpec((LANES, gw), pm(lambda g: (0, 0))),
        pl.BlockSpec((gw, LANES), pm(lambda g: (0, 0))),
    ]
    args += [cw, cw, cw, cb, cb, cb, prm["ssd_dt_bias3"], prm["ssd_a3"], prm["ssd_d3"], prm["ssd_norm3"],
             sel, e, et]
    if prompt:
        y_rows = out_rows
        out_specs = [pl.BlockSpec((r, gw), lambda b, g, c: (b * nc + c, g)),
                     pl.BlockSpec((None, gw, n_st), lambda b, g, c: (b, g, 0))]
        s_shape = (nb, w_b, n_st)
    else:
        nseq = r // q
        s_off = layer * (nb // nseq)
        in_specs.append(pl.BlockSpec((nseq, gw, n_st), lambda i, g: (i + s_off, g, 0)))
        args.append(state_in)
        y_rows = nb * q
        out_specs = [pl.BlockSpec((r, gw), lambda i, g: (i, g)),
                     pl.BlockSpec((nseq, gw, n_st), lambda i, g: (i + s_off, g, 0))]
        s_shape = state_in.shape
        if state_prev is not None:
            in_specs.append(pl.BlockSpec(memory_space=pl.ANY))
            args.append(state_prev)
            aliases = {len(args) - 1: 1}
    kern = functools.partial(_ssd_kernel, prompt=prompt, q=q, n_heads=n_heads, hd=hd,
                             has_prev=state_prev is not None)
    return pl.pallas_call(
        kern, grid=grid, in_specs=in_specs, out_specs=out_specs, input_output_aliases=aliases,
        out_shape=[jax.ShapeDtypeStruct((y_rows, w_b), BF16), jax.ShapeDtypeStruct(s_shape, F32)],
        scratch_shapes=[pltpu.VMEM((r + SUBLANES, gw), F32), pltpu.VMEM((r + SUBLANES, n_st), F32),
                        pltpu.VMEM((r + SUBLANES, n_st), F32), pltpu.VMEM((r, gw), F32)],
        compiler_params=_cparams(("parallel", "parallel", "arbitrary") if prompt else ("parallel", "parallel")),
        name="ssd_prompt" if prompt else "ssd_sample",
    )(*args)


def _conf_prompt_kernel(g_ref, halo_ref, w_ref, b_ref, lg_ref, lb_ref, out_ref, pad_ref, conv_ref, *, rt):
    tt, d = g_ref.shape
    halo = halo_ref.shape[0]
    kk = w_ref.shape[1]
    nlt = d // LANES
    keep = jnp.where(pl.program_id(1) == 0, 0.0, 1.0)
    for lt in range(nlt):
        cols = slice(lt * LANES, (lt + 1) * LANES)
        pad_ref[lt, 0:halo, :] = halo_ref[:, cols] * keep
        pad_ref[lt, halo:halo + tt, :] = g_ref[:, cols]
    base = halo - (kk - 1)

    def lane_tile(lt, carry):
        wt = w_ref[lt]
        bt = b_ref[lt]
        for r0 in range(0, tt, rt):
            acc = jnp.broadcast_to(bt, (rt, LANES))
            for k in range(kk):
                acc = acc + wt[k:k + 1, :] * pad_ref[lt, base + r0 + k:base + r0 + k + rt, :]
            conv_ref[lt, r0:r0 + rt, :] = acc
        return carry

    lax.fori_loop(0, nlt, lane_tile, 0)

    s1 = conv_ref[0]
    for lt in range(1, nlt):
        s1 = s1 + conv_ref[lt]
    mu = jnp.sum(s1, axis=-1, keepdims=True) * (1.0 / d)
    s2 = None
    for lt in range(nlt):
        dv = conv_ref[lt] - mu
        s2 = dv * dv if s2 is None else s2 + dv * dv
    rstd = lax.rsqrt(jnp.sum(s2, axis=-1, keepdims=True) * (1.0 / d) + LN_EPS)
    for lt in range(nlt):
        cols = slice(lt * LANES, (lt + 1) * LANES)
        v = (conv_ref[lt] - mu) * rstd * lg_ref[:, cols] + lb_ref[:, cols]
        out_ref[:, cols] = _silu(v).astype(out_ref.dtype)


def _conf_prompt(g, layer, dw_w, dw_b3, ln_g3, ln_b3, nb, t, out_rows, tt=CONF_TT, rt=CONF_RT, halo=CONF_HALO):
    d = g.shape[1]
    n_layers, kk, _ = dw_w.shape
    assert t % tt == 0 and tt % rt == 0 and halo >= kk - 1 and tt % halo == 0
    nt = t // tt
    nlt = d // LANES
    w_tiles = jnp.transpose(dw_w.reshape(n_layers, kk, nlt, LANES), (0, 2, 1, 3))
    b_tiles = dw_b3.reshape(n_layers, nlt, 1, LANES)
    par = pl.BlockSpec((None, 1, d), lambda b, i: (layer, 0, 0))
    return pl.pallas_call(
        functools.partial(_conf_prompt_kernel, rt=rt),
        grid=(nb, nt),
        in_specs=[pl.BlockSpec((tt, d), lambda b, i: (b * nt + i, 0)),
                  pl.BlockSpec((halo, d), lambda b, i: (jnp.maximum((b * t + i * tt) // halo - 1, 0), 0)),
                  pl.BlockSpec((None, nlt, kk, LANES), lambda b, i: (layer, 0, 0, 0)),
                  pl.BlockSpec((None, nlt, 1, LANES), lambda b, i: (layer, 0, 0, 0)), par, par],
        out_specs=pl.BlockSpec((tt, d), lambda b, i: (b * nt + i, 0)),
        out_shape=jax.ShapeDtypeStruct((out_rows, d), BF16),
        scratch_shapes=[pltpu.VMEM((d // LANES, halo + tt, LANES), F32), pltpu.VMEM((d // LANES, tt, LANES), F32)],
        compiler_params=_cparams(("parallel", "parallel")),
        name="conf_prompt",
    )(g, g, w_tiles, b_tiles, ln_g3, ln_b3)


def _conf_sample_kernel(*refs, n_alias):
    hist_ref, g_ref, w_ref, b_ref, lg_ref, lb_ref = refs[:6]
    c_ref, newh_ref, ext_ref, acc_ref = refs[6 + n_alias:]
    nbt, kh, _ = hist_ref.shape
    t_new = g_ref.shape[0] // nbt
    kk = w_ref.shape[0]
    w = w_ref[...]
    for b in range(nbt):
        ext_ref[0:kh, :] = hist_ref[b]
        ext_ref[kh:kh + t_new, :] = g_ref[b * t_new:(b + 1) * t_new, :]
        newh_ref[b] = ext_ref[t_new:t_new + kh, :]
        for t in range(t_new):
            row = b * t_new + t
            acc_ref[row:row + 1, :] = jnp.sum(ext_ref[t:t + kk, :] * w, axis=0, keepdims=True) + b_ref[...]
    c_ref[...] = _silu(_ln_rows(acc_ref[...], lg_ref[...], lb_ref[...])).astype(c_ref.dtype)


def _conf_sample(gg, hist_all, c_merged, newh_prev, layer, dw_w, dw_b3, ln_g3, ln_b3, row0, nb, nbt=16):
    d = gg.shape[1]
    n_rows_hist, kh, _ = hist_all.shape
    kk = dw_w.shape[1]
    t_new = (gg.shape[0] - row0) // nb
    blk_rows = nbt * t_new
    ext_rows = -(-(kh + t_new) // SUBLANES) * SUBLANES
    assert row0 % blk_rows == 0 and nb % nbt == 0 and kh == kk - 1
    h_off = layer * (nb // nbt)
    par = pl.BlockSpec((None, 1, d), lambda i: (layer, 0, 0))
    anyspace = pl.BlockSpec(memory_space=pl.ANY)
    in_specs = [pl.BlockSpec((nbt, kh, d), lambda i: (i + h_off, 0, 0)),
                pl.BlockSpec((blk_rows, d), lambda i: (row0 // blk_rows + i, 0)),
                pl.BlockSpec((None, kk, d), lambda i: (layer, 0, 0)), par, par, par, anyspace]
    args = [hist_all, gg, dw_w, dw_b3, ln_g3, ln_b3, c_merged]
    aliases = {6: 0}
    if newh_prev is not None:
        in_specs.append(anyspace)
        args.append(newh_prev)
        aliases[7] = 1
    return pl.pallas_call(
        functools.partial(_conf_sample_kernel, n_alias=len(aliases)),
        grid=(nb // nbt,),
        in_specs=in_specs,
        out_specs=[pl.BlockSpec((blk_rows, d), lambda i: (row0 // blk_rows + i, 0)),
                   pl.BlockSpec((nbt, kh, d), lambda i: (i + h_off, 0, 0))],
        out_shape=[jax.ShapeDtypeStruct(c_merged.shape, c_merged.dtype),
                   jax.ShapeDtypeStruct(hist_all.shape, F32)],
        scratch_shapes=[pltpu.VMEM((ext_rows, d), F32), pltpu.VMEM((blk_rows, d), F32)],
        input_output_aliases=aliases,
        compiler_params=_cparams(("parallel",)),
        name="conf_sample",
    )(*args)


def kernel(x_prompt, x_sample, state_s5_re, state_s5_im, state_ssm, state_ssd_conv, state_conformer_conv, w_in_even, s5_lam_re, s5_lam_im, s5_log_dt, s5_b_re, s5_b_im, s5_c_re, s5_c_im, s5_d, s5_w_glu, s5_b_glu, ssd_conv_w, ssd_conv_b, ssd_dt_bias, ssd_a_log, ssd_d, ssd_norm_w, w_out_even, conf_w_pw1, conf_b_pw1, conf_dw_w, conf_dw_b, conf_ln_g, conf_ln_b, conf_w_pw2, conf_b_pw2, ln_g, ln_b, moe_w_group, moe_b_group, moe_w_expert, moe_b_expert, moe_w_gate, moe_w_up, moe_w_down):
    bsz, seq, d = x_prompt.shape
    nb_s, t_s, _ = x_sample.shape
    depth = ln_g.shape[0]
    alpha = (2.0 * depth) ** 0.25
    mp, ms = bsz * seq, nb_s * t_s
    x = jnp.concatenate([x_prompt.reshape(mp, d), x_sample.reshape(ms, d)], axis=0)
    xbf = x.astype(BF16)

    g_a = s5_lam_re.shape[1]
    w_a = s5_d.shape[1]
    w_b = ssd_norm_w.shape[1]
    n_heads = ssd_dt_bias.shape[1]
    conv_dim = ssd_conv_w.shape[2]
    n_st = (conv_dim - w_b) // (2 * SSD_GROUPS)
    o_z, o_x = w_a, w_a + w_b
    o_dt = o_x + conv_dim
    in_even = o_dt + n_heads
    cols = (o_z, o_x, o_dt, w_b, n_st)
    ssd_prm = _ssd_params(ssd_conv_w, ssd_conv_b, ssd_dt_bias, ssd_a_log, ssd_d, ssd_norm_w)
    ssm_in = state_ssm.reshape(state_ssm.shape[0] * nb_s, w_b, n_st)

    n_eg, per_g = moe_w_expert.shape[1], moe_w_expert.shape[3]
    n_route = n_eg + n_eg * per_g
    w_route = jnp.concatenate(
        [moe_w_group, jnp.transpose(moe_w_expert, (0, 2, 1, 3)).reshape(depth, d, n_eg * per_g),
         jnp.zeros((depth, d, LANES - n_route), F32)], axis=-1)
    b_route = jnp.concatenate(
        [moe_b_group, moe_b_expert.reshape(depth, n_eg * per_g), jnp.zeros((depth, LANES - n_route), F32)],
        axis=-1)[:, None, :]

    conf_dw_b3 = conf_dw_b[:, None, :]
    conf_ln_g3 = conf_ln_g[:, None, :]
    conf_ln_b3 = conf_ln_b[:, None, :]
    s5_zero = jnp.zeros((g_a // 2, bsz, 2 * s5_lam_re.shape[2]), F32)

    out = dict(re_p=[], im_p=[], ssm_p=[], sh_p=[], ch_p=[], re_s=[], im_s=[], sh_s=[])
    ssm_s = None
    ch_s = None
    conf_hist_in = state_conformer_conv.reshape((-1,) + state_conformer_conv.shape[2:])
    for layer in range(depth):
        i = layer // 2
        if layer % 2 == 0:
            u = _matmul([xbf], w_in_even, i, n_out=in_even)
            prm = (s5_lam_re[i], s5_lam_im[i], s5_log_dt[i], s5_b_re[i], s5_b_im[i], s5_c_re[i], s5_c_im[i],
                   s5_d[i])
            nk = seq // S5_CHUNK
            h, fr_p, fi_p = _s5_mixer(u, _s5_prepare(*prm, S5_CHUNK), s5_zero, s5_zero, bsz, nk,
                                      natural_rows=mp + ms)
            hc_s, fr_s, fi_s = _s5_mixer(_to_chunks(u[mp:, :w_a], nb_s, 1, t_s, g_a),
                                         _s5_prepare(*prm, t_s), _state_to_pairs(state_s5_re[i]),
                                         _state_to_pairs(state_s5_im[i]), nb_s, 1)
            h = _put_rows(h, _from_chunks(hc_s, nb_s, 1, t_s, g_a), mp)
            ya = _matmul([h], s5_w_glu, i, bias=s5_b_glu, mode="gate", gate=h, out_dtype=BF16)
            out["re_p"].append(_pairs_to_state(fr_p)); out["im_p"].append(_pairs_to_state(fi_p))
            out["re_s"].append(_pairs_to_state(fr_s)); out["im_s"].append(_pairs_to_state(fi_s))
            yb, ssm_p = _ssd_mixer(u, i, ssd_prm, cols, prompt=True, nb=bsz, t=seq, out_rows=mp + ms)
            us3 = u[mp:].reshape(nb_s, t_s, in_even)
            dead = SAMPLE_SEQ_ROWS - t_s
            hist = state_ssd_conv[i]
            rows8 = lambda parts: jnp.concatenate(parts, axis=1).reshape(nb_s * SAMPLE_SEQ_ROWS, -1)
            ext_z = rows8([jnp.zeros((nb_s, dead, w_b), F32), us3[:, :, o_z:o_z + w_b]])
            ext_x = rows8([jnp.zeros((nb_s, dead - hist.shape[1], conv_dim), F32), hist,
                           us3[:, :, o_x:o_x + conv_dim]])
            ext_dt = rows8([jnp.zeros((nb_s, dead, LANES), F32),
                            jnp.pad(us3[:, :, o_dt:], ((0, 0), (0, 0), (0, LANES - n_heads)))])
            yb_s, ssm_s = _ssd_mixer((ext_z, ext_x, ext_dt), i, ssd_prm, (0, 0, 0, w_b, n_st), prompt=False,
                                     nb=nb_s, t=t_s, state_in=ssm_in, state_prev=ssm_s)
            yb_s = yb_s.reshape(nb_s, SAMPLE_SEQ_ROWS, w_b)[:, dead:].reshape(ms, w_b)
            yb = _put_rows(yb, yb_s, mp)
            mix = _matmul([ya, yb], w_out_even, i, out_dtype=BF16)
            kh = hist.shape[1]
            out["ssm_p"].append(ssm_p.reshape(bsz, n_heads, SSD_HEAD_DIM, n_st))
            out["sh_p"].append(jnp.stack([u[(b + 1) * seq - kh:(b + 1) * seq, o_x:o_x + conv_dim]
                                          for b in range(bsz)]))
            out["sh_s"].append(jnp.concatenate([hist, us3[:, :, o_x:o_x + conv_dim]], axis=1)[:, t_s:])
        else:
            gg = _matmul([xbf], conf_w_pw1, i, bias=conf_b_pw1, mode="glu")
            c = _conf_prompt(gg, i, conf_dw_w, conf_dw_b3, conf_ln_g3, conf_ln_b3, bsz, seq, mp + ms)
            c, ch_s = _conf_sample(gg, conf_hist_in, c, ch_s, i, conf_dw_w, conf_dw_b3, conf_ln_g3, conf_ln_b3,
                                   mp, nb_s)
            mix = _matmul([c], conf_w_pw2, i, bias=conf_b_pw2, out_dtype=BF16)
            kh = state_conformer_conv.shape[2]
            out["ch_p"].append(jnp.stack([gg[(b + 1) * seq - kh:(b + 1) * seq] for b in range(bsz)]))
        x, xpk, eid, gates = _ln_router(x, mix, ln_g, ln_b, layer, 0, alpha,
                                        w_route[layer], b_route[layer], n_eg, per_g)
        yb_moe, slots = _hier_moe(xpk, eid[:, :TOP_K], moe_w_gate, moe_w_up, moe_w_down, layer)
        x, xbf = _ln_combine(x, yb_moe, slots, gates, ln_g, ln_b, layer, 1, alpha)

    st = lambda k: jnp.stack(out[k])
    return (x[:mp].reshape(bsz, seq, d), x[mp:].reshape(nb_s, t_s, d),
            st("re_p"), st("im_p"), st("ssm_p"), st("sh_p"), st("ch_p"),
            st("re_s"), st("im_s"), ssm_s.reshape(state_ssm.shape), st("sh_s"),
            ch_s.reshape(state_conformer_conv.shape))
```

```python
import functools
import math

import jax
import jax.numpy as jnp
from jax import lax
from jax.experimental import pallas as pl
from jax.experimental.pallas import tpu as pltpu

F32 = jnp.float32
BF16 = jnp.bfloat16
HIGHEST = lax.Precision.HIGHEST

LN_EPS = 1e-5
V7X_VMEM_LIMIT_BYTES = 56 * 1024 * 1024
LANES = 128
SUBLANES = 8

S5_GROUP = 16
SSD_HEAD_DIM = 64
SSD_GROUPS = 4
SSD_CHUNK = 128
TOP_K = 2

MM_TM = 512
MM_TN = 512
LN_TM = 256
MOE_BM = 256
MOE_ROW_BUFS = 3
S5_CHUNK = 16
S5_GB = 8
S5_RELAYOUT_UNROLL = 4
CONF_TT = 256
CONF_RT = 32
CONF_HALO = 32
SAMPLE_SEQ_ROWS = 8


def _cparams(sem):
    return pltpu.CompilerParams(dimension_semantics=sem, vmem_limit_bytes=V7X_VMEM_LIMIT_BYTES)


def _dot(a, b):
    return jnp.dot(a, b, preferred_element_type=F32)


def _split3(a):
    hi = a.astype(BF16)
    r1 = a - hi.astype(F32)
    mid = r1.astype(BF16)
    lo = (r1 - mid.astype(F32)).astype(BF16)
    return hi, mid, lo


def _dot_select_lhs(a, onehot):
    hi, mid, lo = _split3(a)
    return (_dot(hi, onehot) + _dot(mid, onehot)) + _dot(lo, onehot)


def _dot_select_rhs(onehot, b):
    hi, mid, lo = _split3(b)
    return (_dot(onehot, hi) + _dot(onehot, mid)) + _dot(onehot, lo)


def _dot_nt(a, b):
    return lax.dot_general(a, b, (((1,), (1,)), ((), ())), preferred_element_type=F32)


def _dot_tn(a, b):
    return lax.dot_general(a, b, (((0,), (0,)), ((), ())), preferred_element_type=F32)


def _pack_bf16_pairs(v):
    half = v.shape[1] // 2
    lo = lax.bitcast_convert_type(v[:, :half].astype(BF16).astype(F32), jnp.uint32)
    hi = lax.bitcast_convert_type(v[:, half:].astype(BF16).astype(F32), jnp.uint32)
    return hi | (lo >> 16)


def _unpack_bf16_pairs(w):
    lo = lax.bitcast_convert_type(w << 16, F32)
    hi = lax.bitcast_convert_type(w & jnp.uint32(0xFFFF0000), F32)
    return lo, hi


def _sigmoid(x):
    return 1.0 / (1.0 + jnp.exp(-x))


def _silu(x):
    return x * _sigmoid(x)


def _mm_kernel(*refs, n_lhs, n_col, has_bias, mode):
    pos = 0
    x_refs = refs[pos:pos + n_lhs]; pos += n_lhs
    w_refs = refs[pos:pos + n_lhs * n_col]; pos += n_lhs * n_col
    b_refs = ()
    if has_bias:
        b_refs = refs[pos:pos + n_col]; pos += n_col
    gate_ref = None
    if mode == "gate":
        gate_ref = refs[pos]; pos += 1
    out_ref = refs[pos]; pos += 1
    wbf_refs = refs[pos:]

    @pl.when(pl.program_id(1) == 0)
    def _():
        for w_ref, wbf_ref in zip(w_refs, wbf_refs):
            wbf_ref[...] = w_ref[...].astype(BF16)

    accs = []
    for c in range(n_col):
        acc = None
        for l in range(n_lhs):
            part = _dot(x_refs[l][...].astype(BF16), wbf_refs[l * n_col + c][...])
            acc = part if acc is None else acc + part
        if has_bias:
            acc = acc + b_refs[c][...]
        accs.append(acc)
    if mode == "glu":
        res = accs[0] * _sigmoid(accs[1])
    elif mode == "gate":
        res = gate_ref[...] * _sigmoid(accs[0])
    else:
        res = accs[0]
    out_ref[...] = res.astype(out_ref.dtype)


def _matmul(xs, w, layer, bias=None, mode="plain", gate=None, n_out=None, out_dtype=F32,
            tm=MM_TM, tn=MM_TN):
    n_lhs = len(xs)
    m, k = xs[0].shape
    n_total = w.shape[2]
    n_col = 2 if mode == "glu" else 1
    if n_out is None:
        n_out = n_total // n_col
    assert m % tm == 0 and w.shape[1] == n_lhs * k
    nj = pl.cdiv(n_out, tn)
    glu_off = (n_total // 2) // tn if mode == "glu" else 0
    if mode == "glu":
        assert (n_total // 2) % tn == 0

    in_specs = [pl.BlockSpec((tm, k), lambda j, i: (i, 0)) for _ in range(n_lhs)]
    args = list(xs)
    for l in range(n_lhs):
        for c in range(n_col):
            in_specs.append(pl.BlockSpec((None, k, tn),
                                         functools.partial(lambda j, i, l, c: (layer, l, j + c * glu_off), l=l, c=c)))
            args.append(w)
    if bias is not None:
        b3 = bias.reshape(bias.shape[0], 1, bias.shape[1])
        for c in range(n_col):
            in_specs.append(pl.BlockSpec((None, 1, tn),
                                         functools.partial(lambda j, i, c: (layer, 0, j + c * glu_off), c=c)))
            args.append(b3)
    if mode == "gate":
        in_specs.append(pl.BlockSpec((tm, tn), lambda j, i: (i, j)))
        args.append(gate)
    kern = functools.partial(_mm_kernel, n_lhs=n_lhs, n_col=n_col, has_bias=bias is not None, mode=mode)
    return pl.pallas_call(
        kern,
        grid=(nj, m // tm),
        in_specs=in_specs,
        out_specs=pl.BlockSpec((tm, tn), lambda j, i: (i, j)),
        out_shape=jax.ShapeDtypeStruct((m, n_out), out_dtype),
        scratch_shapes=[pltpu.VMEM((k, tn), BF16) for _ in range(n_lhs * n_col)],
        compiler_params=_cparams(("arbitrary", "arbitrary")),
        name=f"mm_{mode}",
    )(*args)


def _put_rows_kernel(small_ref, big_ref, out_ref):
    del big_ref
    out_ref[...] = small_ref[...]


def _put_rows(big, small, row0):
    n, w = small.shape
    assert row0 % n == 0 and big.shape[1] == w and big.dtype == small.dtype
    return pl.pallas_call(
        _put_rows_kernel, grid=(1,),
        in_specs=[pl.BlockSpec((n, w), lambda i: (0, 0)), pl.BlockSpec(memory_space=pl.ANY)],
        out_specs=pl.BlockSpec((n, w), lambda i: (row0 // n, 0)),
        out_shape=jax.ShapeDtypeStruct(big.shape, big.dtype),
        input_output_aliases={1: 0}, compiler_params=_cparams(("arbitrary",)), name="put_rows",
    )(small, big)


def _ln_rows(y, g, b):
    mu = jnp.mean(y, axis=-1, keepdims=True)
    yc = y - mu
    var = jnp.mean(yc * yc, axis=-1, keepdims=True)
    return yc * lax.rsqrt(var + LN_EPS) * g + b


def _first_argmax(v, lane, big):
    m = jnp.max(v, axis=-1, keepdims=True)
    idx = jnp.min(jnp.where(v == m, lane, big), axis=-1, keepdims=True)
    return m, idx


def _ln_router_kernel(x_ref, mix_ref, g_ref, b_ref, wrh_ref, wrl_ref, br_ref,
                      out_ref, outpk_ref, eid_ref, gate_ref, *, alpha, n_groups, per_group):
    y = alpha * x_ref[...] + mix_ref[...].astype(F32)
    out = _ln_rows(y, g_ref[...], b_ref[...])
    out_ref[...] = out
    outpk_ref[...] = _pack_bf16_pairs(out)
    out_hi = out.astype(BF16)
    out_lo = (out - out_hi.astype(F32)).astype(BF16)
    wr_hi = wrh_ref[...]
    lg = _dot(out_hi, wr_hi) + (_dot(out_lo, wr_hi) + _dot(out_hi, wrl_ref[...])) + br_ref[...]
    lane = lax.broadcasted_iota(jnp.int32, lg.shape, 1)
    neg = jnp.float32(-jnp.inf)
    big = jnp.int32(LANES)
    gl = jnp.where(lane < n_groups, lg, neg)
    gmax, grp = _first_argmax(gl, lane, big)
    pg_top = 1.0 / jnp.sum(jnp.exp(gl - gmax), axis=-1, keepdims=True)
    lo = n_groups + grp * per_group
    el = jnp.where((lane >= lo) & (lane < lo + per_group), lg, neg)
    m1, i1 = _first_argmax(el, lane, big)
    el2 = jnp.where(lane == i1, neg, el)
    m2, i2 = _first_argmax(el2, lane, big)
    e2 = jnp.exp(m2 - m1)
    g1 = pg_top / (1.0 + e2)
    g2 = pg_top * e2 / (1.0 + e2)
    eid_ref[...] = jnp.where(lane == 0, i1 - n_groups, jnp.where(lane == 1, i2 - n_groups, 0))
    gate_ref[...] = jnp.where(lane == 0, g1, jnp.where(lane == 1, g2, 0.0))


def _ln_router(x, mix, ln_g, ln_b, layer, which, alpha, wr, br, n_groups, per_group, tm=LN_TM):
    m, d = x.shape
    assert m % tm == 0
    row = pl.BlockSpec((tm, d), lambda i: (i, 0))
    par = pl.BlockSpec((None, None, 1, d), lambda i: (layer, which, 0, 0))
    small = pl.BlockSpec((tm, LANES), lambda i: (i, 0))
    wspec = pl.BlockSpec((d, LANES), lambda i: (0, 0))
    g4 = ln_g.reshape(ln_g.shape[0], ln_g.shape[1], 1, d)
    b4 = ln_b.reshape(ln_b.shape[0], ln_b.shape[1], 1, d)
    wr_hi = wr.astype(BF16)
    wr_lo = (wr - wr_hi.astype(F32)).astype(BF16)
    kern = functools.partial(_ln_router_kernel, alpha=alpha, n_groups=n_groups, per_group=per_group)
    return pl.pallas_call(
        kern, grid=(m // tm,),
        in_specs=[row, row, par, par, wspec, wspec, pl.BlockSpec((1, LANES), lambda i: (0, 0))],
        out_specs=[row, pl.BlockSpec((tm, d // 2), lambda i: (i, 0)), small, small],
        out_shape=[jax.ShapeDtypeStruct((m, d), F32), jax.ShapeDtypeStruct((m, d // 2), jnp.uint32),
                   jax.ShapeDtypeStruct((m, LANES), jnp.int32), jax.ShapeDtypeStruct((m, LANES), F32)],
        compiler_params=_cparams(("parallel",)), name="ln_router",
    )(x, mix, g4, b4, wr_hi, wr_lo, br)


def _pair_row_copy(y_hbm, buf_ref, sem_ref, slot, k, src_row, dst_row, n_rows):
    return pltpu.make_async_copy(y_hbm.at[pl.ds(src_row, n_rows), :],
                                 buf_ref.at[slot, k, pl.ds(dst_row, n_rows), :], sem_ref.at[slot])


def _ln_combine_kernel(slot_ref, x_ref, gate_ref, y_hbm, g_ref, b_ref, out_ref, outbf_ref, buf_ref, sem_ref, *,
                       alpha):
    i = pl.program_id(0)
    tm = x_ref.shape[0]
    cur = i % 2

    def issue(tile, dst):
        base = tile * (tm * TOP_K)

        def body(r, carry):
            for k in range(TOP_K):
                _pair_row_copy(y_hbm, buf_ref, sem_ref, dst, k, slot_ref[base + TOP_K * r + k], r, 1).start()
            return carry

        lax.fori_loop(0, tm, body, 0, unroll=4)

    @pl.when(i == 0)
    def _():
        issue(0, 0)

    @pl.when(i + 1 < pl.num_programs(0))
    def _():
        base = (i + 1) * (tm * TOP_K)
        for r in range(tm):
            for k in range(TOP_K):
                _pair_row_copy(y_hbm, buf_ref, sem_ref, 1 - cur, k, slot_ref[base + TOP_K * r + k], r, 1).start()

    for k in range(TOP_K):
        _pair_row_copy(y_hbm, buf_ref, sem_ref, cur, k, 0, 0, tm).wait()
    gates = gate_ref[...]
    lo0, hi0 = _unpack_bf16_pairs(buf_ref[cur, 0])
    lo1, hi1 = _unpack_bf16_pairs(buf_ref[cur, 1])
    g0, g1 = gates[:, 0:1], gates[:, 1:2]
    y = alpha * x_ref[...] + jnp.concatenate([lo0 * g0 + lo1 * g1, hi0 * g0 + hi1 * g1], axis=1)
    out = _ln_rows(y, g_ref[...], b_ref[...])
    out_ref[...] = out
    outbf_ref[...] = out.astype(BF16)


def _ln_combine(x, yb, slot_flat, gates, ln_g, ln_b, layer, which, alpha, tm=LN_TM):
    m, d = x.shape
    assert m % tm == 0 and TOP_K == 2
    row = pl.BlockSpec((tm, d), lambda i, s: (i, 0))
    gate_spec = pl.BlockSpec((tm, LANES), lambda i, s: (i, 0))
    par = pl.BlockSpec((None, None, 1, d), lambda i, s: (layer, which, 0, 0))
    g4 = ln_g.reshape(ln_g.shape[0], ln_g.shape[1], 1, d)
    b4 = ln_b.reshape(ln_b.shape[0], ln_b.shape[1], 1, d)
    return pl.pallas_call(
        functools.partial(_ln_combine_kernel, alpha=alpha),
        grid_spec=pltpu.PrefetchScalarGridSpec(
            num_scalar_prefetch=1, grid=(m // tm,),
            in_specs=[row, gate_spec, pl.BlockSpec(memory_space=pl.ANY), par, par],
            out_specs=[row, row],
            scratch_shapes=[pltpu.VMEM((2, TOP_K, tm, d // 2), jnp.uint32), pltpu.SemaphoreType.DMA((2,))]),
        out_shape=[jax.ShapeDtypeStruct((m, d), F32), jax.ShapeDtypeStruct((m, d), BF16)],
        compiler_params=pltpu.CompilerParams(dimension_semantics=("arbitrary",),
                                             vmem_limit_bytes=V7X_VMEM_LIMIT_BYTES,
                                             disable_bounds_checks=True),
        name="ln_combine",
    )(slot_flat, x, gates, yb, g4, b4)


def _moe_row_copy(x_hbm, xbuf_ref, sem_ref, slot, src_row, dst_row, n_rows):
    return pltpu.make_async_copy(x_hbm.at[pl.ds(src_row, n_rows), :],
                                 xbuf_ref.at[slot, pl.ds(dst_row, n_rows), :], sem_ref.at[slot])


def _expert_weight_stream(i, n_used, be_ref, nxt_ref, w_hbms, wbuf_refs, wsem_ref, layer, on_arrival):
    def copies(expert):
        return [pltpu.make_async_copy(w.at[layer, expert], wbuf_ref, wsem_ref.at[0])
                for w, wbuf_ref in zip(w_hbms, wbuf_refs)]

    @pl.when(i == 0)
    def _():
        for c in copies(be_ref[0]):
            c.start()

    prev = be_ref[jnp.maximum(i - 1, 0)]

    @pl.when((i == 0) | ((i < n_used) & (be_ref[i] != prev)))
    def _():
        for c in copies(be_ref[i]):
            c.wait()
        on_arrival()
        nxt = nxt_ref[i]

        @pl.when(nxt >= 0)
        def _():
            for c in copies(nxt):
                c.start()


def _moe_up_kernel(be_ref, nxt_ref, nu_ref, tok_ref, x_hbm, wg_hbm, wu_hbm, h_ref,
                   wgst_ref, wust_ref, wgbf_ref, wubf_ref, xbuf_ref, sem_ref, wsem_ref, *, layer):
    i = pl.program_id(0)
    n_used = nu_ref[0]
    n_buf, bm = xbuf_ref.shape[0], xbuf_ref.shape[1]
    depth = n_buf - 1
    slot = lax.rem(i, n_buf)

    def start_row(blk, r):
        _moe_row_copy(x_hbm, xbuf_ref, sem_ref, lax.rem(blk, n_buf), tok_ref[blk * bm + r], r, 1).start()

    @pl.when(i == 0)
    def _():
        def body(k, carry):
            start_row(k // bm, lax.rem(k, bm))
            return carry

        lax.fori_loop(0, jnp.minimum(depth, n_used) * bm, body, 0)

    @pl.when(i + depth < n_used)
    def _():
        for r in range(bm):
            start_row(i + depth, r)

    def cast_weights():
        wgbf_ref[...] = wgst_ref[...].astype(BF16)
        wubf_ref[...] = wust_ref[...].astype(BF16)

    _expert_weight_stream(i, n_used, be_ref, nxt_ref, (wg_hbm, wu_hbm), (wgst_ref, wust_ref), wsem_ref, layer,
                          cast_weights)

    @pl.when(i < n_used)
    def _():
        _moe_row_copy(x_hbm, xbuf_ref, sem_ref, slot, 0, 0, bm).wait()
        x = jnp.concatenate(_unpack_bf16_pairs(xbuf_ref[slot]), axis=1).astype(BF16)
        h = _silu(_dot(x, wgbf_ref[...])) * _dot(x, wubf_ref[...])
        h_ref[...] = h.astype(BF16)

    @pl.when(i >= n_used)
    def _():
        h_ref[...] = jnp.zeros(h_ref.shape, BF16)


def _moe_down_kernel(be_ref, nxt_ref, nu_ref, h_ref, wd_hbm, out_ref, wdst_ref, wdbf_ref, wsem_ref, *, layer):
    i = pl.program_id(0)

    def cast_weights():
        wdbf_ref[...] = wdst_ref[...].astype(BF16)

    _expert_weight_stream(i, nu_ref[0], be_ref, nxt_ref, (wd_hbm,), (wdst_ref,), wsem_ref, layer, cast_weights)

    @pl.when(i < nu_ref[0])
    def _():
        out_ref[...] = _pack_bf16_pairs(_dot(h_ref[...], wdbf_ref[...]))

    @pl.when(i >= nu_ref[0])
    def _():
        out_ref[...] = jnp.zeros(out_ref.shape, jnp.uint32)


def _moe_experts(xpk, tok_buf, blk_expert, blk_next, n_used, w_gate, w_up, w_down, layer, bm=MOE_BM):
    cap = tok_buf.shape[0]
    d = w_gate.shape[2]
    f = w_gate.shape[3]
    n_blocks = cap // bm
    anyspace = pl.BlockSpec(memory_space=pl.ANY)
    gather_params = pltpu.CompilerParams(dimension_semantics=("arbitrary",),
                                         vmem_limit_bytes=V7X_VMEM_LIMIT_BYTES, disable_bounds_checks=True)
    hidden = pl.pallas_call(
        functools.partial(_moe_up_kernel, layer=layer),
        grid_spec=pltpu.PrefetchScalarGridSpec(
            num_scalar_prefetch=4,
            grid=(n_blocks,),
            in_specs=[anyspace, anyspace, anyspace],
            out_specs=pl.BlockSpec((bm, f), lambda i, be, nx, nu, tok: (i, 0)),
            scratch_shapes=[pltpu.VMEM((d, f), F32), pltpu.VMEM((d, f), F32),
                            pltpu.VMEM((d, f), BF16), pltpu.VMEM((d, f), BF16),
                            pltpu.VMEM((MOE_ROW_BUFS, bm, d // 2), jnp.uint32),
                            pltpu.SemaphoreType.DMA((MOE_ROW_BUFS,)), pltpu.SemaphoreType.DMA((1,))],
        ),
        out_shape=jax.ShapeDtypeStruct((cap, f), BF16),
        compiler_params=gather_params,
        name="moe_up",
    )(blk_expert, blk_next, n_used, tok_buf, xpk, w_gate, w_up)
    return pl.pallas_call(
        functools.partial(_moe_down_kernel, layer=layer),
        grid_spec=pltpu.PrefetchScalarGridSpec(
            num_scalar_prefetch=3,
            grid=(n_blocks,),
            in_specs=[pl.BlockSpec((bm, f), lambda i, be, nx, nu: (jnp.minimum(i, nu[0] - 1), 0)), anyspace],
            out_specs=pl.BlockSpec((bm, d // 2), lambda i, be, nx, nu: (i, 0)),
            scratch_shapes=[pltpu.VMEM((f, d), F32), pltpu.VMEM((f, d), BF16),
                            pltpu.SemaphoreType.DMA((1,))],
        ),
        out_shape=jax.ShapeDtypeStruct((cap, d // 2), jnp.uint32),
        compiler_params=_cparams(("arbitrary",)),
        name="moe_down",
    )(blk_expert, blk_next, n_used, hidden, w_down)


def _moe_dispatch(eid, n_experts, bm):
    t = eid.shape[0]
    n_assign = t * TOP_K
    flat_e = eid.reshape(-1)
    ids = jnp.arange(n_experts, dtype=jnp.int32)
    onehot = (flat_e[:, None] == ids[None, :]).astype(jnp.int32)
    csum = jnp.cumsum(onehot, axis=0)
    counts = csum[-1]
    padded = (counts + bm - 1) // bm * bm
    pad_end = jnp.cumsum(padded)
    pad_start = pad_end - padded
    slot = jnp.sum(onehot * (csum - 1 + pad_start[None, :]), axis=1).astype(jnp.int32)
    n_blocks = -(-n_assign // bm) + n_experts
    cap = n_blocks * bm
    tok_buf = jnp.zeros((cap,), jnp.int32).at[slot].set(jnp.arange(n_assign, dtype=jnp.int32) // TOP_K)
    blk_start = jnp.arange(n_blocks, dtype=jnp.int32) * bm
    blk_expert = jnp.minimum(jnp.sum((pad_end[None, :] <= blk_start[:, None]).astype(jnp.int32), axis=1),
                             n_experts - 1).astype(jnp.int32)
    n_used = (pad_end[-1] // bm).astype(jnp.int32).reshape(1)
    at_or_after = lax.cummin(jnp.where(counts > 0, ids, n_experts), axis=0, reverse=True)
    after = jnp.concatenate([at_or_after[1:], jnp.full((1,), n_experts, jnp.int32)])
    after = jnp.where(after < n_experts, after, -1)
    blk_next = jnp.sum(jnp.where(blk_expert[:, None] == ids[None, :], after[None, :], 0), axis=1).astype(jnp.int32)
    return slot, tok_buf, blk_expert, blk_next, n_used


def _hier_moe(x, eid, w_gate, w_up, w_down, layer):
    n_experts = w_gate.shape[1]
    slot, tok_buf, blk_expert, blk_next, n_used = _moe_dispatch(eid, n_experts, MOE_BM)
    return _moe_experts(x, tok_buf, blk_expert, blk_next, n_used, w_gate, w_up, w_down, layer), slot


def _s5_prepare(lam_re, lam_im, log_dt, b_re, b_im, c_re, c_im, d_skip, lc):
    g, p, c = b_re.shape
    dt = jnp.exp(log_dt.astype(F32))[:, None]
    lr, li = lam_re.astype(F32), lam_im.astype(F32)
    mag = jnp.exp(lr * dt)
    ab_re, ab_im = mag * jnp.cos(li * dt), mag * jnp.sin(li * dt)
    den = lr * lr + li * li
    q_re = ((ab_re - 1.0) * lr + ab_im * li) / den
    q_im = (ab_im * lr - (ab_re - 1.0) * li) / den
    bb_re = q_re[..., None] * b_re - q_im[..., None] * b_im
    bb_im = q_re[..., None] * b_im + q_im[..., None] * b_re
    ks = jnp.arange(lc + 1, dtype=F32)[:, None, None]
    pmag = jnp.exp(lr * dt * ks)
    pw_re, pw_im = pmag * jnp.cos(li * dt * ks), pmag * jnp.sin(li * dt * ks)
    t_re = pw_re[:lc, :, :, None] * bb_re[None] - pw_im[:lc, :, :, None] * bb_im[None]
    t_im = pw_re[:lc, :, :, None] * bb_im[None] + pw_im[:lc, :, :, None] * bb_re[None]
    kt = (jnp.einsum("gop,kgpi->gkoi", c_re, t_re, precision=HIGHEST)
          - jnp.einsum("gop,kgpi->gkoi", c_im, t_im, precision=HIGHEST))
    padk = jnp.pad(kt, ((0, 0), (lc, 1), (0, 0), (0, 0)))
    toe = jnp.tile(padk, (1, lc, 1, 1))[:, :2 * lc * lc].reshape(g, lc, 2 * lc, c, c)[:, :, lc:]
    m_intra = jnp.transpose(toe, (0, 1, 4, 2, 3)).reshape(g, lc * c, lc * c)
    rev = pw_re[:lc][::-1], pw_im[:lc][::-1]
    wst_re = rev[0][:, :, :, None] * bb_re[None] - rev[1][:, :, :, None] * bb_im[None]
    wst_im = rev[0][:, :, :, None] * bb_im[None] + rev[1][:, :, :, None] * bb_re[None]
    wst_re = jnp.transpose(wst_re, (1, 0, 3, 2)).reshape(g, lc * c, p)
    wst_im = jnp.transpose(wst_im, (1, 0, 3, 2)).reshape(g, lc * c, p)
    half = (jnp.arange(g) % 2)[:, None] == jnp.arange(2)[None, :]

    def place_cols(wm):
        return jnp.where(half[:, None, :, None], wm[:, :, None, :], 0.0).reshape(g, lc * c, 2 * p)

    wy_re = c_re[:, None] * pw_re[1:].transpose(1, 0, 2)[:, :, None, :] \
        - c_im[:, None] * pw_im[1:].transpose(1, 0, 2)[:, :, None, :]
    wy_im = -(c_re[:, None] * pw_im[1:].transpose(1, 0, 2)[:, :, None, :]
              + c_im[:, None] * pw_re[1:].transpose(1, 0, 2)[:, :, None, :])
    wy_re = jnp.transpose(wy_re, (0, 3, 1, 2)).reshape(g, p, lc * c)
    wy_im = jnp.transpose(wy_im, (0, 3, 1, 2)).reshape(g, p, lc * c)

    def place_rows(wm):
        return jnp.where(half[:, :, None, None], wm[:, None, :, :], 0.0).reshape(g, 2 * p, lc * c)

    al_re = pw_re[lc].reshape(g // 2, 1, 2 * p)
    al_im = pw_im[lc].reshape(g // 2, 1, 2 * p)
    d_ch = jnp.tile(d_skip.astype(F32).reshape(g, 1, c), (1, lc, 1)).reshape(g, 1, lc * c)
    return dict(m=m_intra.astype(BF16), wst_re=place_cols(wst_re).astype(BF16),
                wst_im=place_cols(wst_im).astype(BF16), wy_re=place_rows(wy_re).astype(BF16),
                wy_im=place_rows(wy_im).astype(BF16), al_re=al_re, al_im=al_im, d=d_ch)


def _block_transpose8(vs, blk):
    vs = list(vs)
    lane_blk = lax.broadcasted_iota(jnp.int32, vs[0].shape, 1) // blk
    for s in (4, 2, 1):
        low = (lane_blk & s) == 0
        for i in range(8):
            if i & s:
                continue
            a, b = vs[i], vs[i + s]
            vs[i] = jnp.where(low, a, pltpu.roll(b, s * blk, axis=1))
            vs[i + s] = jnp.where(low, pltpu.roll(a, LANES - s * blk, axis=1), b)
    return vs


def _rows_to_chunks(u_ref, uc_ref, lc):
    gb, rows, w = uc_ref.shape
    blk = w // lc

    def body(rb, carry):
        r0 = pl.multiple_of(rb * SUBLANES, SUBLANES)
        for half in range(lc // 8):
            vs = [u_ref[pl.ds(r0 * lc + half * 8 + l, SUBLANES, stride=lc), :] for l in range(8)]
            ws = _block_transpose8(vs, blk)
            for g in range(gb):
                uc_ref[g, pl.ds(r0, SUBLANES), half * LANES:(half + 1) * LANES] = ws[g]
        return carry

    lax.fori_loop(0, rows // SUBLANES, body, 0, unroll=S5_RELAYOUT_UNROLL)


def _chunks_to_rows(yc_ref, h_ref, lc):
    gb, rows, w = yc_ref.shape
    blk = w // lc

    def body(rb, carry):
        r0 = pl.multiple_of(rb * SUBLANES, SUBLANES)
        for half in range(lc // 8):
            ws = [yc_ref[g, pl.ds(r0, SUBLANES), half * LANES:(half + 1) * LANES] for g in range(gb)]
            vs = _block_transpose8(ws, blk)
            for l in range(8):
                h_ref[pl.ds(r0 * lc + half * 8 + l, SUBLANES, stride=lc), :] = vs[l]
        return carry

    lax.fori_loop(0, rows // SUBLANES, body, 0, unroll=S5_RELAYOUT_UNROLL)


def _s5_kernel(*refs, gb, nb, nk, lc, natural):
    (u_ref, m_ref, wsr_ref, wsi_ref, wyr_ref, wyi_ref, alr_ref, ali_ref, d_ref, s0r_ref, s0i_ref,
     h_ref, sfr_ref, sfi_ref, locr_ref, loci_ref, str_ref, sti_ref) = refs[:18]
    if natural:
        uc_ref, yc_ref = refs[18:]
        _rows_to_chunks(u_ref, uc_ref, lc)
    else:
        uc_ref, yc_ref = u_ref, h_ref
    for pair in range(gb // 2):
        g0, g1 = 2 * pair, 2 * pair + 1
        u0 = uc_ref[g0]
        u1 = uc_ref[g1]
        ub0 = u0.astype(BF16)
        ub1 = u1.astype(BF16)
        loc_re = _dot(ub0, wsr_ref[g0]) + _dot(ub1, wsr_ref[g1])
        loc_im = _dot(ub0, wsi_ref[g0]) + _dot(ub1, wsi_ref[g1])
        a_re = alr_ref[pair]
        a_im = ali_ref[pair]
        s_re = s0r_ref[pair]
        s_im = s0i_ref[pair]
        if nk == 1:
            st_re, st_im = s_re, s_im
            f_re = a_re * s_re - a_im * s_im + loc_re
            f_im = a_re * s_im + a_im * s_re + loc_im
        else:
            locr_ref[...] = loc_re
            loci_ref[...] = loc_im

            def step(k, carry):
                c_re, c_im = carry
                rows = pl.ds(k, nb, stride=nk)
                str_ref[rows, :] = c_re
                sti_ref[rows, :] = c_im
                l_re = locr_ref[rows, :]
                l_im = loci_ref[rows, :]
                return (a_re * c_re - a_im * c_im + l_re, a_re * c_im + a_im * c_re + l_im)

            f_re, f_im = lax.fori_loop(0, nk, step, (s_re, s_im))
            st_re = str_ref[...]
            st_im = sti_ref[...]
        sfr_ref[pair] = f_re
        sfi_ref[pair] = f_im
        sb_re = st_re.astype(BF16)
        sb_im = st_im.astype(BF16)
        for gi, u, ub in ((g0, u0, ub0), (g1, u1, ub1)):
            y = (_dot(ub, m_ref[gi]) + _dot(sb_re, wyr_ref[gi]) + _dot(sb_im, wyi_ref[gi])
                 + d_ref[gi] * u)
            yc_ref[gi] = jax.nn.gelu(y)
    if natural:
        _chunks_to_rows(yc_ref, h_ref, lc)


def _s5_mixer(u, prep, s0_re, s0_im, nb, nk, gb=S5_GB, natural_rows=None):
    g, w, _ = prep["m"].shape
    p2 = prep["al_re"].shape[2]
    r = nb * nk
    natural = natural_rows is not None
    grp = lambda *shape: pl.BlockSpec((gb,) + shape, lambda i: (i, 0, 0))
    pr = lambda *shape: pl.BlockSpec((gb // 2,) + shape, lambda i: (i, 0, 0))
    scratch = [pltpu.VMEM((r, p2), F32) for _ in range(4)]
    if natural:
        lc = w // S5_GROUP
        assert gb == 8 and gb * S5_GROUP == LANES and lc % 8 == 0 and r % SUBLANES == 0
        io_spec = pl.BlockSpec((r * lc, LANES), lambda i: (0, i))
        out_h = jax.ShapeDtypeStruct((natural_rows, g * S5_GROUP), F32)
        scratch += [pltpu.VMEM((gb, r, w), F32), pltpu.VMEM((gb, r, w), F32)]
    else:
        lc = 0
        io_spec = grp(r, w)
        out_h = jax.ShapeDtypeStruct((g, r, w), F32)
    kern = functools.partial(_s5_kernel, gb=gb, nb=nb, nk=nk, lc=lc, natural=natural)
    return pl.pallas_call(
        kern,
        grid=(g // gb,),
        in_specs=[io_spec, grp(w, w), grp(w, p2), grp(w, p2), grp(p2, w), grp(p2, w),
                  pr(1, p2), pr(1, p2), grp(1, w), pr(nb, p2), pr(nb, p2)],
        out_specs=[io_spec, pr(nb, p2), pr(nb, p2)],
        out_shape=[out_h,
                   jax.ShapeDtypeStruct((g // 2, nb, p2), F32),
                   jax.ShapeDtypeStruct((g // 2, nb, p2), F32)],
        scratch_shapes=scratch,
        compiler_params=_cparams(("parallel",)),
        name="s5_prompt" if natural else "s5_sample",
    )(u, prep["m"], prep["wst_re"], prep["wst_im"], prep["wy_re"], prep["wy_im"],
      prep["al_re"], prep["al_im"], prep["d"], s0_re, s0_im)


def _to_chunks(u2d, nb, nk, lc, g):
    c = u2d.shape[1] // g
    u5 = u2d.reshape(nb, nk, lc, g, c)
    return jnp.transpose(u5, (3, 0, 1, 2, 4)).reshape(g, nb * nk, lc * c)


def _from_chunks(hc, nb, nk, lc, g):
    c = hc.shape[2] // lc
    h5 = hc.reshape(g, nb, nk, lc, c)
    return jnp.transpose(h5, (1, 2, 3, 0, 4)).reshape(nb * nk * lc, g * c)


def _state_to_pairs(s):
    b, g, p = s.shape
    return jnp.transpose(s.reshape(b, g // 2, 2 * p), (1, 0, 2))


def _pairs_to_state(f):
    g2, b, p2 = f.shape
    return jnp.transpose(f, (1, 0, 2)).reshape(b, g2 * 2, p2 // 2)


def _softplus(x):
    return jnp.maximum(x, 0.0) + jnp.log1p(jnp.exp(-jnp.abs(x)))


def _ssd_kernel(*refs, prompt, q, n_heads, hd, has_prev):
    it = iter(refs)
    z_ref, xs_ref, bm_ref, cm_ref, dt_ref = (next(it) for _ in range(5))
    if prompt:
        hx_ref, hb_ref, hc_ref = (next(it) for _ in range(3))
    cwx_ref, cwb_ref, cwc_ref, cbx_ref, cbb_ref, cbc_ref = (next(it) for _ in range(6))
    dtb_ref, a_ref, dsk_ref, nw_ref, sel_ref, e_ref, et_ref = (next(it) for _ in range(7))
    if not prompt:
        sin_ref = next(it)
    if has_prev:
        next(it)
    y_ref, sout_ref = next(it), next(it)
    padx_ref, padb_ref, padc_ref, yacc_ref = (next(it) for _ in range(4))

    r = z_ref.shape[0]
    nseq = r // q
    hpg = xs_ref.shape[1] // hd
    row = lax.broadcasted_iota(jnp.int32, (r, 1), 0)
    if prompt:
        first = pl.program_id(2) == 0
        keep = jnp.where(first, 0.0, 1.0)
        live = None
    else:
        live = (row % q) >= (q // 2)

    def conv(x_ref, halo_ref, pad_ref, w_ref, b_ref):
        if prompt:
            pad_ref[0:SUBLANES, :] = halo_ref[...] * keep
        else:
            pad_ref[0:SUBLANES, :] = jnp.zeros((SUBLANES, pad_ref.shape[1]), F32)
        pad_ref[SUBLANES:SUBLANES + r, :] = x_ref[...]
        kk = w_ref.shape[0]
        acc = b_ref[...]
        for k in range(kk):
            acc = acc + w_ref[k:k + 1, :] * pad_ref[pl.ds(SUBLANES - (kk - 1) + k, r), :]
        return _silu(acc)

    xs = conv(xs_ref, hx_ref if prompt else None, padx_ref, cwx_ref, cbx_ref)
    bc = conv(bm_ref, hb_ref if prompt else None, padb_ref, cwb_ref, cbb_ref)
    cc = conv(cm_ref, hc_ref if prompt else None, padc_ref, cwc_ref, cbc_ref)

    lane = lax.broadcasted_iota(jnp.int32, (r, LANES), 1)
    dtv = jnp.where(lane < n_heads, _softplus(dt_ref[...] + dtb_ref[...]), 0.0)
    if not prompt:
        dtv = jnp.where(live, dtv, 0.0)
    sel = sel_ref[...]
    both = _dot_select_lhs(jnp.concatenate([dtv, dtv * a_ref[...]], axis=0), sel)
    dt8, adt8 = both[0:r], both[r:2 * r]

    ri = lax.broadcasted_iota(jnp.int32, (r, r), 0)
    ci = lax.broadcasted_iota(jnp.int32, (r, r), 1)
    same = (ri // q) == (ci // q)
    tri = same & (ci <= ri)
    masks = jnp.concatenate([jnp.where(tri, 1.0, 0.0), jnp.where(same, 1.0, 0.0)], axis=0).astype(BF16)
    sums = _dot_select_rhs(masks, adt8)
    acum, atot = sums[0:r], sums[r:2 * r]
    acum_t = acum.T
    expand = _dot_select_lhs(jnp.concatenate([jnp.exp(acum), dt8, jnp.exp(atot - acum)], axis=0), e_ref[...])
    eac_x, dt_x, dte_x = expand[0:r], expand[r:2 * r], expand[2 * r:3 * r]
    dec = _dot_select_rhs(et_ref[...], jnp.exp(atot).T)

    cb = _dot_nt(cc.astype(BF16), bc.astype(BF16))
    xdt = xs * dt_x
    neg = jnp.float32(-jnp.inf)
    lane_hd = lax.broadcasted_iota(jnp.int32, (r, 2 * hd), 1)
    for pair in range(hpg // 2):
        xp = xdt[:, pair * 2 * hd:(pair + 1) * 2 * hd]
        acc = None
        for half in range(2):
            hl = 2 * pair + half
            seg = acum[:, hl:hl + 1] - acum_t[hl:hl + 1, :]
            scores = cb * jnp.exp(jnp.where(tri, seg, neg))
            mask = (lane_hd < hd) if half == 0 else (lane_hd >= hd)
            part = _dot(scores.astype(BF16), jnp.where(mask, xp, 0.0).astype(BF16))
            acc = part if acc is None else acc + part
        yacc_ref[:, pair * 2 * hd:(pair + 1) * 2 * hd] = acc

    wgt = (xdt * dte_x).astype(BF16)
    bcb = bc.astype(BF16)
    ccb = cc.astype(BF16)
    if prompt:
        @pl.when(first)
        def _():
            sout_ref[...] = jnp.zeros(sout_ref.shape, F32)

        s_old = sout_ref[...]
        yacc_ref[...] += _dot_nt(ccb, s_old.astype(BF16)) * eac_x
        sout_ref[...] = dec[:, 0:1] * s_old + _dot_tn(wgt, bcb)
    else:
        for j in range(nseq):
            rows = slice(j * q, (j + 1) * q)
            s_old = sin_ref[j]
            yacc_ref[rows, :] += _dot_nt(ccb[rows], s_old.astype(BF16)) * eac_x[rows]
            wj = jnp.where((row // q) == j, wgt, jnp.zeros_like(wgt))
            sout_ref[j] = dec[:, j * q:j * q + 1] * s_old + _dot_tn(wj, bcb)

    y = yacc_ref[...] + dsk_ref[...] * xs
    y = y * _silu(z_ref[...])
    y = y * lax.rsqrt(jnp.mean(y * y, axis=-1, keepdims=True) + LN_EPS)
    y_ref[...] = (y * nw_ref[...]).astype(y_ref.dtype)


def _ssd_constants(n_heads, hd, n_groups):
    hpg = n_heads // n_groups
    lane = jnp.arange(LANES)
    sel = (lane[None, :, None] == (jnp.arange(n_groups)[:, None, None] * hpg + lane[None, None, :])) \
        & (lane[None, None, :] < hpg)
    e = (lane[:, None] == (jnp.arange(hpg * hd) // hd)[None, :])
    return sel.astype(BF16), e.astype(BF16), e.T.astype(BF16)


def _pad_lanes(v):
    return jnp.pad(v, ((0, 0), (0, LANES - v.shape[1])))[:, None, :]


def _ssd_params(conv_w, conv_b, dt_bias, a_log, d_skip, norm_w):
    n_heads = dt_bias.shape[1]
    hd = norm_w.shape[1] // n_heads
    return dict(
        ssd_conv_w=conv_w, ssd_conv_b3=conv_b[:, None, :],
        ssd_dt_bias3=_pad_lanes(dt_bias.astype(F32)),
        ssd_a3=_pad_lanes(-jnp.exp(a_log.astype(F32))),
        ssd_d3=jnp.repeat(d_skip.astype(F32), hd, axis=1)[:, None, :],
        ssd_norm3=norm_w[:, None, :],
        ssd_consts=_ssd_constants(n_heads, hd, SSD_GROUPS))


def _ssd_mixer(u, layer, prm, cols, *, prompt, nb, t, state_in=None, out_rows=None, state_prev=None):
    aliases = {}
    u_z, u_x, u_dt = u if isinstance(u, tuple) else (u, u, u)
    o_z, o_x, o_dt, w_b, n_st = cols
    n_groups = SSD_GROUPS
    hd = SSD_HEAD_DIM
    n_heads = w_b // hd
    gw = w_b // n_groups
    r = SSD_CHUNK
    sel, e, et = prm["ssd_consts"]
    o_b = o_x + w_b
    o_c = o_b + n_groups * n_st
    assert o_z % gw == 0 and o_x % gw == 0 and o_b % n_st == 0 and o_dt % LANES == 0 and n_st == LANES
    cw, cb = prm["ssd_conv_w"], prm["ssd_conv_b3"]

    if prompt:
        nc = t // r
        grid = (nb, n_groups, nc)
        rowblk = lambda b, g, c: b * nc + c
        halo = lambda b, g, c: jnp.maximum((b * t + c * r) // SUBLANES - 1, 0)
        im = lambda colf: (lambda b, g, c: (rowblk(b, g, c), colf(g)))
        hm = lambda colf: (lambda b, g, c: (halo(b, g, c), colf(g)))
        pm = lambda f: (lambda b, g, c: f(g))
        q = r
    else:
        q = SAMPLE_SEQ_ROWS
        grid = (nb * q // r, n_groups)
        im = lambda colf: (lambda i, g: (i, colf(g)))
        pm = lambda f: (lambda i, g: f(g))
    col_z = lambda g: o_z // gw + g
    col_x = lambda g: o_x // gw + g
    col_b = lambda g: o_b // n_st + g
    col_c = lambda g: o_c // n_st + g
    col_dt = lambda g: o_dt // LANES

    in_specs = [pl.BlockSpec((r, gw), im(col_z)), pl.BlockSpec((r, gw), im(col_x)),
                pl.BlockSpec((r, n_st), im(col_b)), pl.BlockSpec((r, n_st), im(col_c)),
                pl.BlockSpec((r, LANES), im(col_dt))]
    args = [u_z, u_x, u_x, u_x, u_dt]
    if prompt:
        in_specs += [pl.BlockSpec((SUBLANES, gw), hm(col_x)), pl.BlockSpec((SUBLANES, n_st), hm(col_b)),
                     pl.BlockSpec((SUBLANES, n_st), hm(col_c))]
        args += [u_x, u_x, u_x]
    kk = cw.shape[1]
    in_specs += [
        pl.BlockSpec((None, kk, gw), pm(lambda g: (layer, 0, g))),
        pl.BlockSpec((None, kk, n_st), pm(lambda g: (layer, 0, w_b // n_st + g))),
        pl.BlockSpec((None, kk, n_st), pm(lambda g: (layer, 0, w_b // n_st + n_groups + g))),
        pl.BlockSpec((None, 1, gw), pm(lambda g: (layer, 0, g))),
        pl.BlockSpec((None, 1, n_st), pm(lambda g: (layer, 0, w_b // n_st + g))),
        pl.BlockSpec((None, 1, n_st), pm(lambda g: (layer, 0, w_b // n_st + n_groups + g))),
        pl.BlockSpec((None, 1, LANES), pm(lambda g: (layer, 0, 0))),
        pl.BlockSpec((None, 1, LANES), pm(lambda g: (layer, 0, 0))),
        pl.BlockSpec((None, 1, gw), pm(lambda g: (layer, 0, g))),
        pl.BlockSpec((None, 1, gw), pm(lambda g: (layer, 0, g))),
        pl.BlockSpec((None, LANES, LANES), pm(lambda g: (g, 0, 0))),
        pl.BlockSpec((LANES, gw), pm(lambda g: (0, 0))),
        pl.BlockSpec((gw, LANES), pm(lambda g: (0, 0))),
    ]
    args += [cw, cw, cw, cb, cb, cb, prm["ssd_dt_bias3"], prm["ssd_a3"], prm["ssd_d3"], prm["ssd_norm3"],
             sel, e, et]
    if prompt:
        y_rows = out_rows
        out_specs = [pl.BlockSpec((r, gw), lambda b, g, c: (b * nc + c, g)),
                     pl.BlockSpec((None, gw, n_st), lambda b, g, c: (b, g, 0))]
        s_shape = (nb, w_b, n_st)
    else:
        nseq = r // q
        s_off = layer * (nb // nseq)
        in_specs.append(pl.BlockSpec((nseq, gw, n_st), lambda i, g: (i + s_off, g, 0)))
        args.append(state_in)
        y_rows = nb * q
        out_specs = [pl.BlockSpec((r, gw), lambda i, g: (i, g)),
                     pl.BlockSpec((nseq, gw, n_st), lambda i, g: (i + s_off, g, 0))]
        s_shape = state_in.shape
        if state_prev is not None:
            in_specs.append(pl.BlockSpec(memory_space=pl.ANY))
            args.append(state_prev)
            aliases = {len(args) - 1: 1}
    kern = functools.partial(_ssd_kernel, prompt=prompt, q=q, n_heads=n_heads, hd=hd,
                             has_prev=state_prev is not None)
    return pl.pallas_call(
        kern, grid=grid, in_specs=in_specs, out_specs=out_specs, input_output_aliases=aliases,
        out_shape=[jax.ShapeDtypeStruct((y_rows, w_b), BF16), jax.ShapeDtypeStruct(s_shape, F32)],
        scratch_shapes=[pltpu.VMEM((r + SUBLANES, gw), F32), pltpu.VMEM((r + SUBLANES, n_st), F32),
                        pltpu.VMEM((r + SUBLANES, n_st), F32), pltpu.VMEM((r, gw), F32)],
        compiler_params=_cparams(("parallel", "parallel", "arbitrary") if prompt else ("parallel", "parallel")),
        name="ssd_prompt" if prompt else "ssd_sample",
    )(*args)


def _conf_prompt_kernel(g_ref, halo_ref, w_ref, b_ref, lg_ref, lb_ref, out_ref, pad_ref, conv_ref, *, rt):
    tt, d = g_ref.shape
    halo = halo_ref.shape[0]
    kk = w_ref.shape[1]
    nlt = d // LANES
    keep = jnp.where(pl.program_id(1) == 0, 0.0, 1.0)
    for lt in range(nlt):
        cols = slice(lt * LANES, (lt + 1) * LANES)
        pad_ref[lt, 0:halo, :] = halo_ref[:, cols] * keep
        pad_ref[lt, halo:halo + tt, :] = g_ref[:, cols]
    base = halo - (kk - 1)

    def lane_tile(lt, carry):
        wt = w_ref[lt]
        bt = b_ref[lt]
        for r0 in range(0, tt, rt):
            acc = jnp.broadcast_to(bt, (rt, LANES))
            for k in range(kk):
                acc = acc + wt[k:k + 1, :] * pad_ref[lt, base + r0 + k:base + r0 + k + rt, :]
            conv_ref[lt, r0:r0 + rt, :] = acc
        return carry

    lax.fori_loop(0, nlt, lane_tile, 0)

    s1 = conv_ref[0]
    for lt in range(1, nlt):
        s1 = s1 + conv_ref[lt]
    mu = jnp.sum(s1, axis=-1, keepdims=True) * (1.0 / d)
    s2 = None
    for lt in range(nlt):
        dv = conv_ref[lt] - mu
        s2 = dv * dv if s2 is None else s2 + dv * dv
    rstd = lax.rsqrt(jnp.sum(s2, axis=-1, keepdims=True) * (1.0 / d) + LN_EPS)
    for lt in range(nlt):
        cols = slice(lt * LANES, (lt + 1) * LANES)
        v = (conv_ref[lt] - mu) * rstd * lg_ref[:, cols] + lb_ref[:, cols]
        out_ref[:, cols] = _silu(v).astype(out_ref.dtype)


def _conf_prompt(g, layer, dw_w, dw_b3, ln_g3, ln_b3, nb, t, out_rows, tt=CONF_TT, rt=CONF_RT, halo=CONF_HALO):
    d = g.shape[1]
    n_layers, kk, _ = dw_w.shape
    assert t % tt == 0 and tt % rt == 0 and halo >= kk - 1 and tt % halo == 0
    nt = t // tt
    nlt = d // LANES
    w_tiles = jnp.transpose(dw_w.reshape(n_layers, kk, nlt, LANES), (0, 2, 1, 3))
    b_tiles = dw_b3.reshape(n_layers, nlt, 1, LANES)
    par = pl.BlockSpec((None, 1, d), lambda b, i: (layer, 0, 0))
    return pl.pallas_call(
        functools.partial(_conf_prompt_kernel, rt=rt),
        grid=(nb, nt),
        in_specs=[pl.BlockSpec((tt, d), lambda b, i: (b * nt + i, 0)),
                  pl.BlockSpec((halo, d), lambda b, i: (jnp.maximum((b * t + i * tt) // halo - 1, 0), 0)),
                  pl.BlockSpec((None, nlt, kk, LANES), lambda b, i: (layer, 0, 0, 0)),
                  pl.BlockSpec((None, nlt, 1, LANES), lambda b, i: (layer, 0, 0, 0)), par, par],
        out_specs=pl.BlockSpec((tt, d), lambda b, i: (b * nt + i, 0)),
        out_shape=jax.ShapeDtypeStruct((out_rows, d), BF16),
        scratch_shapes=[pltpu.VMEM((d // LANES, halo + tt, LANES), F32), pltpu.VMEM((d // LANES, tt, LANES), F32)],
        compiler_params=_cparams(("parallel", "parallel")),
        name="conf_prompt",
    )(g, g, w_tiles, b_tiles, ln_g3, ln_b3)


def _conf_sample_kernel(*refs, n_alias):
    hist_ref, g_ref, w_ref, b_ref, lg_ref, lb_ref = refs[:6]
    c_ref, newh_ref, ext_ref, acc_ref = refs[6 + n_alias:]
    nbt, kh, _ = hist_ref.shape
    t_new = g_ref.shape[0] // nbt
    kk = w_ref.shape[0]
    w = w_ref[...]
    for b in range(nbt):
        ext_ref[0:kh, :] = hist_ref[b]
        ext_ref[kh:kh + t_new, :] = g_ref[b * t_new:(b + 1) * t_new, :]
        newh_ref[b] = ext_ref[t_new:t_new + kh, :]
        for t in range(t_new):
            row = b * t_new + t
            acc_ref[row:row + 1, :] = jnp.sum(ext_ref[t:t + kk, :] * w, axis=0, keepdims=True) + b_ref[...]
    c_ref[...] = _silu(_ln_rows(acc_ref[...], lg_ref[...], lb_ref[...])).astype(c_ref.dtype)


def _conf_sample(gg, hist_all, c_merged, newh_prev, layer, dw_w, dw_b3, ln_g3, ln_b3, row0, nb, nbt=16):
    d = gg.shape[1]
    n_rows_hist, kh, _ = hist_all.shape
    kk = dw_w.shape[1]
    t_new = (gg.shape[0] - row0) // nb
    blk_rows = nbt * t_new
    ext_rows = -(-(kh + t_new) // SUBLANES) * SUBLANES
    assert row0 % blk_rows == 0 and nb % nbt == 0 and kh == kk - 1
    h_off = layer * (nb // nbt)
    par = pl.BlockSpec((None, 1, d), lambda i: (layer, 0, 0))
    anyspace = pl.BlockSpec(memory_space=pl.ANY)
    in_specs = [pl.BlockSpec((nbt, kh, d), lambda i: (i + h_off, 0, 0)),
                pl.BlockSpec((blk_rows, d), lambda i: (row0 // blk_rows + i, 0)),
                pl.BlockSpec((None, kk, d), lambda i: (layer, 0, 0)), par, par, par, anyspace]
    args = [hist_all, gg, dw_w, dw_b3, ln_g3, ln_b3, c_merged]
    aliases = {6: 0}
    if newh_prev is not None:
        in_specs.append(anyspace)
        args.append(newh_prev)
        aliases[7] = 1
    return pl.pallas_call(
        functools.partial(_conf_sample_kernel, n_alias=len(aliases)),
        grid=(nb // nbt,),
        in_specs=in_specs,
        out_specs=[pl.BlockSpec((blk_rows, d), lambda i: (row0 // blk_rows + i, 0)),
                   pl.BlockSpec((nbt, kh, d), lambda i: (i + h_off, 0, 0))],
        out_shape=[jax.ShapeDtypeStruct(c_merged.shape, c_merged.dtype),
                   jax.ShapeDtypeStruct(hist_all.shape, F32)],
        scratch_shapes=[pltpu.VMEM((ext_rows, d), F32), pltpu.VMEM((blk_rows, d), F32)],
        input_output_aliases=aliases,
        compiler_params=_cparams(("parallel",)),
        name="conf_sample",
    )(*args)


def kernel(x_prompt, x_sample, state_s5_re, state_s5_im, state_ssm, state_ssd_conv, state_conformer_conv, w_in_even, s5_lam_re, s5_lam_im, s5_log_dt, s5_b_re, s5_b_im, s5_c_re, s5_c_im, s5_d, s5_w_glu, s5_b_glu, ssd_conv_w, ssd_conv_b, ssd_dt_bias, ssd_a_log, ssd_d, ssd_norm_w, w_out_even, conf_w_pw1, conf_b_pw1, conf_dw_w, conf_dw_b, conf_ln_g, conf_ln_b, conf_w_pw2, conf_b_pw2, ln_g, ln_b, moe_w_group, moe_b_group, moe_w_expert, moe_b_expert, moe_w_gate, moe_w_up, moe_w_down):
    bsz, seq, d = x_prompt.shape
    nb_s, t_s, _ = x_sample.shape
    depth = ln_g.shape[0]
    alpha = (2.0 * depth) ** 0.25
    mp, ms = bsz * seq, nb_s * t_s
    x = jnp.concatenate([x_prompt.reshape(mp, d), x_sample.reshape(ms, d)], axis=0)
    xbf = x.astype(BF16)

    g_a = s5_lam_re.shape[1]
    w_a = s5_d.shape[1]
    w_b = ssd_norm_w.shape[1]
    n_heads = ssd_dt_bias.shape[1]
    conv_dim = ssd_conv_w.shape[2]
    n_st = (conv_dim - w_b) // (2 * SSD_GROUPS)
    o_z, o_x = w_a, w_a + w_b
    o_dt = o_x + conv_dim
    in_even = o_dt + n_heads
    cols = (o_z, o_x, o_dt, w_b, n_st)
    ssd_prm = _ssd_params(ssd_conv_w, ssd_conv_b, ssd_dt_bias, ssd_a_log, ssd_d, ssd_norm_w)
    ssm_in = state_ssm.reshape(state_ssm.shape[0] * nb_s, w_b, n_st)

    n_eg, per_g = moe_w_expert.shape[1], moe_w_expert.shape[3]
    n_route = n_eg + n_eg * per_g
    w_route = jnp.concatenate(
        [moe_w_group, jnp.transpose(moe_w_expert, (0, 2, 1, 3)).reshape(depth, d, n_eg * per_g),
         jnp.zeros((depth, d, LANES - n_route), F32)], axis=-1)
    b_route = jnp.concatenate(
        [moe_b_group, moe_b_expert.reshape(depth, n_eg * per_g), jnp.zeros((depth, LANES - n_route), F32)],
        axis=-1)[:, None, :]

    conf_dw_b3 = conf_dw_b[:, None, :]
    conf_ln_g3 = conf_ln_g[:, None, :]
    conf_ln_b3 = conf_ln_b[:, None, :]
    s5_zero = jnp.zeros((g_a // 2, bsz, 2 * s5_lam_re.shape[2]), F32)

    out = dict(re_p=[], im_p=[], ssm_p=[], sh_p=[], ch_p=[], re_s=[], im_s=[], sh_s=[])
    ssm_s = None
    ch_s = None
    conf_hist_in = state_conformer_conv.reshape((-1,) + state_conformer_conv.shape[2:])
    for layer in range(depth):
        i = layer // 2
        if layer % 2 == 0:
            u = _matmul([xbf], w_in_even, i, n_out=in_even)
            prm = (s5_lam_re[i], s5_lam_im[i], s5_log_dt[i], s5_b_re[i], s5_b_im[i], s5_c_re[i], s5_c_im[i],
                   s5_d[i])
            nk = seq // S5_CHUNK
            h, fr_p, fi_p = _s5_mixer(u, _s5_prepare(*prm, S5_CHUNK), s5_zero, s5_zero, bsz, nk,
                                      natural_rows=mp + ms)
            hc_s, fr_s, fi_s = _s5_mixer(_to_chunks(u[mp:, :w_a], nb_s, 1, t_s, g_a),
                                         _s5_prepare(*prm, t_s), _state_to_pairs(state_s5_re[i]),
                                         _state_to_pairs(state_s5_im[i]), nb_s, 1)
            h = _put_rows(h, _from_chunks(hc_s, nb_s, 1, t_s, g_a), mp)
            ya = _matmul([h], s5_w_glu, i, bias=s5_b_glu, mode="gate", gate=h, out_dtype=BF16)
            out["re_p"].append(_pairs_to_state(fr_p)); out["im_p"].append(_pairs_to_state(fi_p))
            out["re_s"].append(_pairs_to_state(fr_s)); out["im_s"].append(_pairs_to_state(fi_s))
            yb, ssm_p = _ssd_mixer(u, i, ssd_prm, cols, prompt=True, nb=bsz, t=seq, out_rows=mp + ms)
            us3 = u[mp:].reshape(nb_s, t_s, in_even)
            dead = SAMPLE_SEQ_ROWS - t_s
            hist = state_ssd_conv[i]
            rows8 = lambda parts: jnp.concatenate(parts, axis=1).reshape(nb_s * SAMPLE_SEQ_ROWS, -1)
            ext_z = rows8([jnp.zeros((nb_s, dead, w_b), F32), us3[:, :, o_z:o_z + w_b]])
            ext_x = rows8([jnp.zeros((nb_s, dead - hist.shape[1], conv_dim), F32), hist,
                           us3[:, :, o_x:o_x + conv_dim]])
            ext_dt = rows8([jnp.zeros((nb_s, dead, LANES), F32),
                            jnp.pad(us3[:, :, o_dt:], ((0, 0), (0, 0), (0, LANES - n_heads)))])
            yb_s, ssm_s = _ssd_mixer((ext_z, ext_x, ext_dt), i, ssd_prm, (0, 0, 0, w_b, n_st), prompt=False,
                                     nb=nb_s, t=t_s, state_in=ssm_in, state_prev=ssm_s)
            yb_s = yb_s.reshape(nb_s, SAMPLE_SEQ_ROWS, w_b)[:, dead:].reshape(ms, w_b)
            yb = _put_rows(yb, yb_s, mp)
            mix = _matmul([ya, yb], w_out_even, i, out_dtype=BF16)
            kh = hist.shape[1]
            out["ssm_p"].append(ssm_p.reshape(bsz, n_heads, SSD_HEAD_DIM, n_st))
            out["sh_p"].append(jnp.stack([u[(b + 1) * seq - kh:(b + 1) * seq, o_x:o_x + conv_dim]
                                          for b in range(bsz)]))
            out["sh_s"].append(jnp.concatenate([hist, us3[:, :, o_x:o_x + conv_dim]], axis=1)[:, t_s:])
        else:
            gg = _matmul([xbf], conf_w_pw1, i, bias=conf_b_pw1, mode="glu")
            c = _conf_prompt(gg, i, conf_dw_w, conf_dw_b3, conf_ln_g3, conf_ln_b3, bsz, seq, mp + ms)
            c, ch_s = _conf_sample(gg, conf_hist_in, c, ch_s, i, conf_dw_w, conf_dw_b3, conf_ln_g3, conf_ln_b3,
                                   mp, nb_s)
            mix = _matmul([c], conf_w_pw2, i, bias=conf_b_pw2, out_dtype=BF16)
            kh = state_conformer_conv.shape[2]
            out["ch_p"].append(jnp.stack([gg[(b + 1) * seq - kh:(b + 1) * seq] for b in range(bsz)]))
        x, xpk, eid, gates = _ln_router(x, mix, ln_g, ln_b, layer, 0, alpha,
                                        w_route[layer], b_route[layer], n_eg, per_g)
        yb_moe, slots = _hier_moe(xpk, eid[:, :TOP_K], moe_w_gate, moe_w_up, moe_w_down, layer)
        x, xbf = _ln_combine(x, yb_moe, slots, gates, ln_g, ln_b, layer, 1, alpha)

    st = lambda k: jnp.stack(out[k])
    return (x[:mp].reshape(bsz, seq, d), x[mp:].reshape(nb_s, t_s, d),
            st("re_p"), st("im_p"), st("ssm_p"), st("sh_p"), st("ch_p"),
            st("re_s"), st("im_s"), ssm_s.reshape(state_ssm.shape), st("sh_s"),
            ch_s.reshape(state_conformer_conv.shape))
```
